```python
import numpy as np
import jax, jax.numpy as jnp
from jax import lax

D_MODEL = 1024
BATCH = 16
SEQ = 4096
DEPTH = 4

CTX_LEN = 256
GRID_W = 64
D_MIX = D_MODEL
HEAD_DIM = 64
ATTN_HEADS = D_MIX // 2 // HEAD_DIM
ATTN_KV_HEADS = ATTN_HEADS // 4
ATTN_GROUP = ATTN_HEADS // ATTN_KV_HEADS
WINDOW = 128
ATTN_BLOCK = 128
ROPE_BASE = 10000.0
ROPE_AXIS_DIM = HEAD_DIM // 2
M_HEADS = D_MIX // 4 // HEAD_DIM
M_WIDTH = M_HEADS * HEAD_DIM
M_CHUNK = 64
G_HEADS = D_MIX // 4 // HEAD_DIM
G_WIDTH = G_HEADS * HEAD_DIM
G_CHUNK = 64
GLA_RANK = 16
GLA_TAU = 16.0
D_FF = 4 * D_MODEL
EPS = 1e-6
IN_SPLITS = (ATTN_HEADS * HEAD_DIM, ATTN_KV_HEADS * HEAD_DIM, ATTN_KV_HEADS * HEAD_DIM,
             M_WIDTH, M_WIDTH, M_WIDTH, M_WIDTH, 2 * M_HEADS, 2 * M_HEADS,
             G_WIDTH, G_WIDTH, G_WIDTH, G_WIDTH, 2 * GLA_RANK)
IN_COLS = sum(IN_SPLITS)

kernel_name = 'hymba_style_flow_backbone'

F32 = jnp.float32


def rmsnorm(x, g):
    xf = x.astype(F32)
    xf = xf * lax.rsqrt(jnp.mean(xf * xf, axis=-1, keepdims=True) + EPS)
    return (xf * g.astype(F32)).astype(x.dtype)


def head_rmsnorm(x, g, n_heads):
    B, N, _ = x.shape
    xh = x.reshape(B, N, n_heads, HEAD_DIM).astype(F32)
    xh = xh * lax.rsqrt(jnp.mean(xh * xh, axis=-1, keepdims=True) + EPS)
    return (xh.reshape(B, N, n_heads * HEAD_DIM) * g.astype(F32)).astype(x.dtype)


def modulate(h, shift, scale):
    return h * (1.0 + scale) + shift


def squared_relu_mlp(h, w1, w2):
    return jnp.square(jax.nn.relu(h @ w1)) @ w2


def split_cols(p):
    idx = np.cumsum(IN_SPLITS)[:-1].tolist()
    return jnp.split(p, idx, axis=-1)


def to_heads(t, n_heads):
    B, N, _ = t.shape
    return t.reshape(B, N, n_heads, HEAD_DIM).transpose(0, 2, 1, 3)


def from_heads(t):
    B, H, N, D = t.shape
    return t.transpose(0, 2, 1, 3).reshape(B, N, H * D)


def axial_rope_tables(rows):
    t_row = jnp.repeat(jnp.arange(rows), GRID_W).astype(F32)
    t_col = jnp.tile(jnp.arange(GRID_W), rows).astype(F32)
    inv = ROPE_BASE ** (-jnp.arange(0, ROPE_AXIS_DIM, 2, dtype=F32) / ROPE_AXIS_DIM)
    ang_r = t_row[:, None] * inv[None, :]
    ang_c = t_col[:, None] * inv[None, :]
    return (jnp.cos(ang_r), jnp.sin(ang_r), jnp.cos(ang_c), jnp.sin(ang_c))


def rotate_axis(xa, cos, sin):
    half = xa.shape[-1] // 2
    x1, x2 = xa[..., :half], xa[..., half:]
    cos = cos[None, :, None, :].astype(xa.dtype)
    sin = sin[None, :, None, :].astype(xa.dtype)
    return jnp.concatenate([x1 * cos - x2 * sin, x2 * cos + x1 * sin], axis=-1)


def apply_axial_rope(x, rope):
    cos_r, sin_r, cos_c, sin_c = rope
    return jnp.concatenate([rotate_axis(x[..., :ROPE_AXIS_DIM], cos_r, sin_r),
                            rotate_axis(x[..., ROPE_AXIS_DIM:], cos_c, sin_c)], axis=-1)


def sink_softmax(s, sink):
    sk = sink.astype(F32)[None, :, :, None, None]
    m = jnp.maximum(jnp.max(s, axis=-1, keepdims=True), sk)
    e = jnp.exp(s - m)
    return e / (jnp.sum(e, axis=-1, keepdims=True) + jnp.exp(sk - m))


def context_attention(q, k, v, sink):
    B, Nc = q.shape[:2]
    s = jnp.einsum('bqhgd,bkhd->bhgqk', q, k).astype(F32)
    p = sink_softmax(s, sink).astype(v.dtype)
    return jnp.einsum('bhgqk,bkhd->bqhgd', p, v).reshape(B, Nc, -1)


def window_attention(q, k, v, kc, vc, sink):
    B, N, KV, G, D = q.shape
    nb = N // ATTN_BLOCK
    pad = ((0, 0), (ATTN_BLOCK, ATTN_BLOCK), (0, 0), (0, 0))
    kp = jnp.pad(k, pad)
    vp = jnp.pad(v, pad)
    qb = q.reshape(B, nb, ATTN_BLOCK, KV, G, D).transpose(1, 0, 2, 3, 4, 5)
    q_offs = jnp.arange(ATTN_BLOCK)
    k_offs = jnp.arange(3 * ATTN_BLOCK)

    def block(args):
        j, qj = args
        kj = lax.dynamic_slice_in_dim(kp, j * ATTN_BLOCK, 3 * ATTN_BLOCK, axis=1)
        vj = lax.dynamic_slice_in_dim(vp, j * ATTN_BLOCK, 3 * ATTN_BLOCK, axis=1)
        qpos = j * ATTN_BLOCK + q_offs
        kpos = (j - 1) * ATTN_BLOCK + k_offs
        valid = ((jnp.abs(qpos[:, None] - kpos[None, :]) <= WINDOW)
                 & (kpos >= 0)[None, :] & (kpos < N)[None, :])
        s_loc = jnp.einsum('bqhgd,bkhd->bhgqk', qj, kj).astype(F32)
        s_loc = jnp.where(valid, s_loc, -jnp.inf)
        s_ctx = jnp.einsum('bqhgd,bkhd->bhgqk', qj, kc).astype(F32)
        p = sink_softmax(jnp.concatenate([s_loc, s_ctx], axis=-1), sink).astype(v.dtype)
        return (jnp.einsum('bhgqk,bkhd->bqhgd', p[..., :3 * ATTN_BLOCK], vj)
                + jnp.einsum('bhgqk,bkhd->bqhgd', p[..., 3 * ATTN_BLOCK:], vc))

    o = lax.map(block, (jnp.arange(nb), qb))
    return o.transpose(1, 0, 2, 3, 4, 5).reshape(B, N, KV * G * D)


def mlstm_chunked(q, k, v, logi, logf, state, with_h):
    dt = v.dtype
    B, H, N, D = q.shape
    nc = N // M_CHUNK
    q = q.astype(F32).reshape(B, H, nc, M_CHUNK, D)
    k = (k.astype(F32) * D ** -0.5).reshape(B, H, nc, M_CHUNK, D)
    v = v.astype(F32).reshape(B, H, nc, M_CHUNK, D)
    logi = logi.astype(F32).reshape(B, H, nc, M_CHUNK)
    b = jnp.cumsum(logf.astype(F32).reshape(B, H, nc, M_CHUNK), axis=-1)
    a = b[..., -1]
    g = a[..., None] - b + logi
    m_chunk = jnp.max(g, axis=-1)
    w = jnp.exp(g - m_chunk[..., None])
    c_chunk = jnp.einsum('bhcs,bhcsd,bhcse->bhcde', w, v, k)
    n_chunk = jnp.einsum('bhcs,bhcse->bhce', w, k)
    if state is None:
        state = (jnp.zeros((B, H, D, D), F32), jnp.zeros((B, H, D), F32), jnp.zeros((B, H), F32))

    def step(carry, inp):
        C, n, m = carry
        a_i, m_i, c_i, n_i = inp
        m_new = jnp.maximum(a_i + m, m_i)
        s_old = jnp.exp(a_i + m - m_new)
        s_new = jnp.exp(m_i - m_new)
        C_new = s_old[..., None, None] * C + s_new[..., None, None] * c_i
        n_new = s_old[..., None] * n + s_new[..., None] * n_i
        return (C_new, n_new, m_new), (C, n, m)

    xs = tuple(jnp.moveaxis(t, 2, 0) for t in (a, m_chunk, c_chunk, n_chunk))
    final, prev = lax.scan(step, state, xs)
    if not with_h:
        return None, final
    c_prev, n_prev, m_prev = (jnp.moveaxis(t, 0, 2) for t in prev)
    lower = jnp.tril(jnp.ones((M_CHUNK, M_CHUNK), bool))
    dmat = jnp.where(lower, b[..., :, None] - b[..., None, :] + logi[..., None, :], -jnp.inf)
    inter = b + m_prev[..., None]
    m_r = jnp.maximum(inter, jnp.max(dmat, axis=-1))
    p = jnp.exp(dmat - m_r[..., None]) * jnp.einsum('bhcrd,bhcsd->bhcrs', q, k)
    w_inter = jnp.exp(inter - m_r)
    num = (w_inter[..., None] * jnp.einsum('bhcde,bhcre->bhcrd', c_prev, q)
           + jnp.einsum('bhcrs,bhcsd->bhcrd', p, v))
    den = w_inter * jnp.einsum('bhce,bhcre->bhcr', n_prev, q) + jnp.sum(p, axis=-1)
    h = num / jnp.maximum(jnp.abs(den), jnp.exp(-m_r))[..., None]
    return h.reshape(B, H, N, D).astype(dt), final


def gla_chunked(q, k, v, glog, state, with_h):
    dt = v.dtype
    B, H, N, D = q.shape
    nc = N // G_CHUNK
    q = (q.astype(F32) * D ** -0.5).reshape(B, H, nc, G_CHUNK, D)
    k = k.astype(F32).reshape(B, H, nc, G_CHUNK, D)
    v = v.astype(F32).reshape(B, H, nc, G_CHUNK, D)
    bc = jnp.cumsum(glog.astype(F32).reshape(B, H, nc, G_CHUNK, D), axis=3)
    b_end = bc[:, :, :, -1:, :]
    u = jnp.einsum('bhcsk,bhcsv->bhckv', k * jnp.exp(b_end - bc), v)
    decay = jnp.exp(b_end[:, :, :, 0, :])
    if state is None:
        state = jnp.zeros((B, H, D, D), F32)

    def step(S, inp):
        d_i, u_i = inp
        return d_i[..., None] * S + u_i, S

    final, s_prev = lax.scan(step, state, (jnp.moveaxis(decay, 2, 0), jnp.moveaxis(u, 2, 0)))
    if not with_h:
        return None, final
    s_prev = jnp.moveaxis(s_prev, 0, 2)
    qd = q * jnp.exp(bc)
    kd = k * jnp.exp(-bc)
    lower = jnp.tril(jnp.ones((G_CHUNK, G_CHUNK), bool))
    att = jnp.where(lower, jnp.einsum('bhcrk,bhcsk->bhcrs', qd, kd), 0.0)
    o = jnp.einsum('bhcrs,bhcsv->bhcrv', att, v) + jnp.einsum('bhcrk,bhckv->bhcrv', qd, s_prev)
    return o.reshape(B, H, N, D).astype(dt), final


def run_direction(scan_fn, ctx_args, lat_args, flip, with_ctx_out):
    if flip:
        ctx_args = tuple(jnp.flip(t, axis=2) for t in ctx_args)
        lat_args = tuple(jnp.flip(t, axis=2) for t in lat_args)
    h_ctx, ctx_state = scan_fn(*ctx_args, None, with_ctx_out)
    h_lat, _ = scan_fn(*lat_args, ctx_state, True)
    if flip:
        h_lat = jnp.flip(h_lat, axis=2)
        if h_ctx is not None:
            h_ctx = jnp.flip(h_ctx, axis=2)
    return h_ctx, h_lat


def mixer_inputs(p, m_i_bias, m_f_bias, g_wa2, g_ba):
    aq, ak, av, mq, mk, mv, mo, mi, mf, gq, gk, gv, gg, glr = split_cols(p)
    B, N, _ = p.shape
    attn = (aq.reshape(B, N, ATTN_HEADS, HEAD_DIM),
            ak.reshape(B, N, ATTN_KV_HEADS, HEAD_DIM),
            av.reshape(B, N, ATTN_KV_HEADS, HEAD_DIM))
    logi = (mi.reshape(B, N, 2, M_HEADS).astype(F32) + m_i_bias.astype(F32)).transpose(2, 0, 3, 1)
    logf = jax.nn.log_sigmoid(mf.reshape(B, N, 2, M_HEADS).astype(F32)
                              + m_f_bias.astype(F32)).transpose(2, 0, 3, 1)
    mlstm = (to_heads(mq, M_HEADS), to_heads(mk, M_HEADS), to_heads(mv, M_HEADS), logi, logf)
    glog = jax.nn.log_sigmoid(jnp.einsum('bnjr,jrw->jbnw', glr.reshape(B, N, 2, GLA_RANK).astype(F32),
                                         g_wa2.astype(F32))
                              + g_ba.astype(F32)[:, None, None, :]) / GLA_TAU
    glog = glog.reshape(2, B, N, G_HEADS, HEAD_DIM).transpose(0, 1, 3, 2, 4)
    gla = (to_heads(gq, G_HEADS), to_heads(gk, G_HEADS), to_heads(gv, G_HEADS), glog)
    return attn, mlstm, jax.nn.sigmoid(mo), gla, jax.nn.silu(gg)


def token_mixer(h, hc, rope, w_in, sink, m_i_bias, m_f_bias, m_norm_g, g_wa2, g_ba, g_norm_g, w_out,
                with_ctx_out):
    B, N, _ = h.shape
    Nc = hc.shape[1]
    (aq, ak, av), (mq, mk, mv, mi, mf), m_o, (gq, gk, gv, gl), g_gate = mixer_inputs(
        h @ w_in, m_i_bias, m_f_bias, g_wa2, g_ba)
    (aqc, akc, avc), (mqc, mkc, mvc, mic, mfc), m_oc, (gqc, gkc, gvc, glc), g_gatec = mixer_inputs(
        hc @ w_in, m_i_bias, m_f_bias, g_wa2, g_ba)
    sink_g = sink.reshape(ATTN_KV_HEADS, ATTN_GROUP)
    q = (apply_axial_rope(aq, rope) * HEAD_DIM ** -0.5).reshape(B, N, ATTN_KV_HEADS, ATTN_GROUP, HEAD_DIM)
    k = apply_axial_rope(ak, rope)
    out_a = window_attention(q, k, av, akc, avc, sink_g)
    hm = [run_direction(mlstm_chunked, (mqc, mkc, mvc, mic[d], mfc[d]), (mq, mk, mv, mi[d], mf[d]),
                        flip, with_ctx_out) for d, flip in ((0, False), (1, True))]
    out_m = head_rmsnorm(from_heads(hm[0][1] + hm[1][1]), m_norm_g, M_HEADS) * m_o
    hg = [run_direction(gla_chunked, (gqc, gkc, gvc, glc[d]), (gq, gk, gv, gl[d]),
                        flip, with_ctx_out) for d, flip in ((0, False), (1, True))]
    out_g = head_rmsnorm(from_heads(hg[0][1] + hg[1][1]), g_norm_g, G_HEADS) * g_gate
    o = jnp.concatenate([out_a, out_m, out_g], axis=-1) @ w_out
    if not with_ctx_out:
        return o, None
    qc = (aqc * HEAD_DIM ** -0.5).reshape(B, Nc, ATTN_KV_HEADS, ATTN_GROUP, HEAD_DIM)
    out_ac = context_attention(qc, akc, avc, sink_g)
    out_mc = head_rmsnorm(from_heads(hm[0][0] + hm[1][0]), m_norm_g, M_HEADS) * m_oc
    out_gc = head_rmsnorm(from_heads(hg[0][0] + hg[1][0]), g_norm_g, G_HEADS) * g_gatec
    oc = jnp.concatenate([out_ac, out_mc, out_gc], axis=-1) @ w_out
    return o, oc


def setup_inputs(seed: int = 0) -> dict:
    key = jax.random.key(seed)
    ks = jax.random.split(key, 20)

    def nrm(k, shape, scale):
        return jax.random.normal(k, shape, F32) * scale

    return {
        'x': nrm(ks[0], (BATCH, SEQ, D_MODEL), 1.0),
        'c': nrm(ks[1], (BATCH, D_MODEL), 1.0),
        'ctx': nrm(ks[2], (BATCH, CTX_LEN, D_MODEL), 1.0),
        'c_ctx': nrm(ks[3], (D_MODEL,), 1.0),
        'w_ada': nrm(ks[4], (DEPTH, D_MODEL, 6 * D_MODEL), 0.5 * D_MODEL ** -0.5),
        'b_ada': nrm(ks[5], (DEPTH, 6 * D_MODEL), 0.02),
        'norm1_g': 1.0 + nrm(ks[6], (DEPTH, D_MODEL), 0.05),
        'norm2_g': 1.0 + nrm(ks[7], (DEPTH, D_MODEL), 0.05),
        'w_in': nrm(ks[8], (DEPTH, D_MODEL, IN_COLS), D_MODEL ** -0.5),
        'attn_sink': nrm(ks[9], (DEPTH, ATTN_HEADS), 0.5),
        'm_i_bias': nrm(ks[10], (DEPTH, 2, M_HEADS), 0.1),
        'm_f_bias': 3.0 + 3.0 * jax.random.uniform(ks[11], (DEPTH, 2, M_HEADS), F32),
        'm_norm_g': 1.0 + nrm(ks[12], (DEPTH, M_WIDTH), 0.05),
        'g_wa2': nrm(ks[13], (DEPTH, 2, GLA_RANK, G_WIDTH), GLA_RANK ** -0.5),
        'g_ba': nrm(ks[14], (DEPTH, 2, G_WIDTH), 0.1),
        'g_norm_g': 1.0 + nrm(ks[15], (DEPTH, G_WIDTH), 0.05),
        'w_out': nrm(ks[16], (DEPTH, D_MIX, D_MODEL), D_MIX ** -0.5),
        'w_mlp1': nrm(ks[17], (DEPTH, D_MODEL, D_FF), D_MODEL ** -0.5),
        'w_mlp2': nrm(ks[18], (DEPTH, D_FF, D_MODEL), D_FF ** -0.5),
        'final_g': 1.0 + nrm(ks[19], (D_MODEL,), 0.05),
    }


def reference(x, c, ctx, c_ctx, w_ada, b_ada, norm1_g, norm2_g, w_in, attn_sink, m_i_bias, m_f_bias,
              m_norm_g, g_wa2, g_ba, g_norm_g, w_out, w_mlp1, w_mlp2, final_g):
    B, N, D = x.shape
    ROWS = N // GRID_W
    rope = axial_rope_tables(ROWS)
    sc = jax.nn.silu(c)
    sctx = jax.nn.silu(c_ctx)
    xc = ctx
    for l in range(DEPTH):
        last = l == DEPTH - 1
        mx = (sc @ w_ada[l] + b_ada[l]).reshape(B, 6, 1, D)
        mc = (sctx @ w_ada[l] + b_ada[l]).reshape(6, D)
        h = modulate(rmsnorm(x, norm1_g[l]), mx[:, 0], mx[:, 1])
        hc = modulate(rmsnorm(xc, norm1_g[l]), mc[0], mc[1])
        o, oc = token_mixer(h, hc, rope, w_in[l], attn_sink[l], m_i_bias[l], m_f_bias[l], m_norm_g[l],
                            g_wa2[l], g_ba[l], g_norm_g[l], w_out[l], not last)
        x = x + mx[:, 2] * o
        x = x + mx[:, 5] * squared_relu_mlp(modulate(rmsnorm(x, norm2_g[l]), mx[:, 3], mx[:, 4]),
                                            w_mlp1[l], w_mlp2[l])
        if not last:
            xc = xc + mc[2] * oc
            xc = xc + mc[5] * squared_relu_mlp(modulate(rmsnorm(xc, norm2_g[l]), mc[3], mc[4]),
                                               w_mlp1[l], w_mlp2[l])
    return rmsnorm(x, final_g)
```

```python
import functools

import numpy as np
import jax
import jax.numpy as jnp
from jax import lax
from jax.experimental import pallas as pl
from jax.experimental.pallas import tpu as pltpu

F32 = jnp.float32
BF16 = jnp.bfloat16

HEAD_DIM = 64
GRID_W = 64
ATTN_HEADS = 8
ATTN_KV_HEADS = 2
ATTN_GROUP = ATTN_HEADS // ATTN_KV_HEADS
WINDOW = 128
ATTN_BLOCK = 128
ROPE_BASE = 10000.0
ROPE_AXIS_DIM = HEAD_DIM // 2
M_HEADS = 4
G_HEADS = 4
CHUNK = 64
GLA_RANK = 16
GLA_TAU = 16.0
EPS = 1e-6
NEG_BIG = -1e30

LANES = 128
Q_COLS = ATTN_HEADS * HEAD_DIM
KV_COLS = 2 * ATTN_KV_HEADS * HEAD_DIM
MIX_COLS = 4 * M_HEADS * HEAD_DIM
SMALL_COLS = LANES
VMEM_LIMIT = 56 * 1024 * 1024


def _cparams(sem):
    return pltpu.CompilerParams(dimension_semantics=sem, vmem_limit_bytes=VMEM_LIMIT)


def _pick_tile(n, pref):
    t = pref
    while n % t:
        t //= 2
    return t


def _split3(a):
    a1 = a.astype(BF16)
    r1 = a - a1.astype(F32)
    a2 = r1.astype(BF16)
    a3 = (r1 - a2.astype(F32)).astype(BF16)
    return a1, a2, a3


def _dot(a, b):
    return jnp.dot(a, b, preferred_element_type=F32)


def _dot_nt(a, b):
    return lax.dot_general(a, b, (((1,), (1,)), ((), ())), preferred_element_type=F32)


def _dot_hi(a, b):
    a1, a2, _ = _split3(a)
    b1, b2, _ = _split3(b)
    return _dot(a1, b1) + (_dot(a1, b2) + _dot(a2, b1))


def _dot_exact_lhs(a_bf16, b):
    b1, b2, b3 = _split3(b)
    return _dot(a_bf16, b1) + (_dot(a_bf16, b2) + _dot(a_bf16, b3))


def _log_sigmoid(x):
    return jnp.minimum(x, 0.0) - jnp.log1p(jnp.exp(-jnp.abs(x)))


def _ada_kernel(cc_ref, w_ref, b_ref, o_ref):
    cc = cc_ref[...]
    s = cc * jax.nn.sigmoid(cc)
    o_ref[0] = _dot_hi(s, w_ref[0]) + b_ref[0]


def _ada_call(cc, w_ada, b_ada):
    depth, d, six_d = w_ada.shape
    rows = cc.shape[0]
    tn = _pick_tile(six_d, 1024)
    return pl.pallas_call(
        _ada_kernel,
        grid=(depth, six_d // tn),
        in_specs=[pl.BlockSpec((rows, d), lambda l, j: (0, 0)),
                  pl.BlockSpec((1, d, tn), lambda l, j: (l, 0, j)),
                  pl.BlockSpec((1, 1, tn), lambda l, j: (l, 0, j))],
        out_specs=pl.BlockSpec((1, rows, tn), lambda l, j: (l, 0, j)),
        out_shape=jax.ShapeDtypeStruct((depth, rows, six_d), F32),
        compiler_params=_cparams(("arbitrary", "arbitrary")),
        name="ada_mod",
    )(cc, w_ada, b_ada.reshape(depth, 1, six_d))


def _norm_mod(x, g, shift, scale):
    ms = jnp.mean(x * x, axis=-1, keepdims=True)
    h = x * lax.rsqrt(ms + EPS) * g
    return h * (1.0 + scale) + shift


def _inproj_kernel(*refs, use_rope):
    if use_rope:
        x_ref, mod_ref, g_ref, w_ref, cos_ref, sa_ref, sb_ref, q_ref, kv_ref, m_ref, gl_ref, s_ref = refs
    else:
        x_ref, mod_ref, g_ref, w_ref, q_ref, kv_ref, m_ref, gl_ref, s_ref = refs
    mod = mod_ref[0]
    hb = _norm_mod(x_ref[0], g_ref[...], mod[0:1], mod[1:2]).astype(BF16)

    def proj(lo, width):
        return _dot(hb, w_ref[:, lo:lo + width])

    pa = proj(0, Q_COLS + KV_COLS)
    if use_rope:
        cos, sa, sb = cos_ref[...], sa_ref[...], sb_ref[...]
        segs = []
        for j in range((Q_COLS + KV_COLS // 2) // LANES):
            seg = pa[:, j * LANES:(j + 1) * LANES]
            segs.append(seg * cos + pltpu.roll(seg, ROPE_AXIS_DIM // 2, 1) * sa
                        + pltpu.roll(seg, LANES - ROPE_AXIS_DIM // 2, 1) * sb)
        qk = jnp.concatenate(segs, axis=1)
    else:
        qk = pa[:, :Q_COLS + KV_COLS // 2]
    q_ref[0] = qk[:, :Q_COLS] * (HEAD_DIM ** -0.5)
    kv_ref[0, :, :KV_COLS // 2] = qk[:, Q_COLS:]
    kv_ref[0, :, KV_COLS // 2:] = pa[:, Q_COLS + KV_COLS // 2:]
    lo = Q_COLS + KV_COLS
    m_ref[0] = proj(lo, MIX_COLS)
    gl_ref[0] = proj(lo + MIX_COLS, MIX_COLS)
    s_ref[0] = proj(lo + 2 * MIX_COLS, SMALL_COLS)


def _inproj_call(x, mod, g, w, rope):
    bx, n, d = x.shape
    tm = _pick_tile(n, 512)
    cols = w.shape[1]
    in_specs = [pl.BlockSpec((1, tm, d), lambda b, i: (b, i, 0)),
                pl.BlockSpec((1, 6, d), lambda b, i: (b, 0, 0)),
                pl.BlockSpec((1, d), lambda b, i: (0, 0)),
                pl.BlockSpec((d, cols), lambda b, i: (0, 0))]
    args = [x, mod, g.reshape(1, d), w]
    if rope is not None:
        in_specs += [pl.BlockSpec((tm, LANES), lambda b, i: (i, 0))] * 3
        args += list(rope)
    widths = (Q_COLS, KV_COLS, MIX_COLS, MIX_COLS, SMALL_COLS)
    return pl.pallas_call(
        functools.partial(_inproj_kernel, use_rope=rope is not None),
        grid=(bx, n // tm),
        in_specs=in_specs,
        out_specs=[pl.BlockSpec((1, tm, wd), lambda b, i: (b, i, 0)) for wd in widths],
        out_shape=[jax.ShapeDtypeStruct((bx, n, wd), F32) for wd in widths],
        compiler_params=_cparams(("parallel", "parallel")),
        name="inproj_rope" if rope is not None else "inproj",
    )(*args)


def _attn_kernel(*refs, local, tq):
    if local:
        sink_ref, q_ref, kv_ref, kvc_ref, o_ref = refs
    else:
        sink_ref, q_ref, kvc_ref, o_ref = refs
    rows = ATTN_GROUP * tq
    q = q_ref[0]
    if local:
        n = kv_ref.shape[1]
        span = 3 * ATTN_BLOCK
        j = pl.program_id(1)
        start = pl.multiple_of(jnp.clip((j - 1) * ATTN_BLOCK, 0, n - span), ATTN_BLOCK)
        qpos = j * ATTN_BLOCK + (lax.broadcasted_iota(jnp.int32, (rows, span), 0) & (tq - 1))
        kpos = start + lax.broadcasted_iota(jnp.int32, (rows, span), 1)
        valid = jnp.abs(qpos - kpos) <= WINDOW
    for kvh in range(ATTN_KV_HEADS):
        heads = [kvh * ATTN_GROUP + g for g in range(ATTN_GROUP)]
        qs = jnp.concatenate([q[:, h * HEAD_DIM:(h + 1) * HEAD_DIM] for h in heads], axis=0).astype(BF16)
        sink = jnp.concatenate([jnp.full((tq, 1), sink_ref[h], F32) for h in heads], axis=0)
        k_lo, v_lo = kvh * HEAD_DIM, KV_COLS // 2 + kvh * HEAD_DIM
        kc = kvc_ref[0, :, k_lo:k_lo + HEAD_DIM].astype(BF16)
        vc = kvc_ref[0, :, v_lo:v_lo + HEAD_DIM].astype(BF16)
        s_ctx = _dot_nt(qs, kc)
        m = jnp.maximum(jnp.max(s_ctx, axis=-1, keepdims=True), sink)
        if local:
            kl = kv_ref[0, pl.ds(start, span), k_lo:k_lo + HEAD_DIM].astype(BF16)
            vl = kv_ref[0, pl.ds(start, span), v_lo:v_lo + HEAD_DIM].astype(BF16)
            s_loc = jnp.where(valid, _dot_nt(qs, kl), NEG_BIG)
            m = jnp.maximum(m, jnp.max(s_loc, axis=-1, keepdims=True))
        e_ctx = jnp.exp(s_ctx - m)
        den = jnp.sum(e_ctx, axis=-1, keepdims=True) + jnp.exp(sink - m)
        acc = _dot(e_ctx.astype(BF16), vc)
        if local:
            e_loc = jnp.exp(s_loc - m)
            den = den + jnp.sum(e_loc, axis=-1, keepdims=True)
            acc = acc + _dot(e_loc.astype(BF16), vl)
        o = acc / den
        for g, h in enumerate(heads):
            o_ref[0, :, h * HEAD_DIM:(h + 1) * HEAD_DIM] = o[g * tq:(g + 1) * tq]


def _attn_call(sink, q, kv, kvc):
    b, nq, _ = q.shape
    nc = kvc.shape[1]
    local = kv is not None
    tq = ATTN_BLOCK if local else nq
    in_specs = [pl.BlockSpec(memory_space=pltpu.SMEM),
                pl.BlockSpec((1, tq, Q_COLS), lambda bi, j: (bi, j, 0))]
    args = [sink, q]
    if local:
        in_specs.append(pl.BlockSpec((1, nq, KV_COLS), lambda bi, j: (bi, 0, 0)))
        args.append(kv)
    in_specs.append(pl.BlockSpec((1, nc, KV_COLS), lambda bi, j: (bi, 0, 0)))
    args.append(kvc)
    return pl.pallas_call(
        functools.partial(_attn_kernel, local=local, tq=tq),
        grid=(b, nq // tq),
        in_specs=in_specs,
        out_specs=pl.BlockSpec((1, tq, Q_COLS), lambda bi, j: (bi, j, 0)),
        out_shape=jax.ShapeDtypeStruct((b, nq, Q_COLS), F32),
        compiler_params=_cparams(("parallel", "arbitrary")),
        name="attn_window" if local else "attn_ctx",
    )(*args)


def _chunk_masks(rev):
    ri = lax.broadcasted_iota(jnp.int32, (CHUNK, CHUNK), 0)
    ci = lax.broadcasted_iota(jnp.int32, (CHUNK, CHUNK), 1)
    return (ci >= ri, ci <= ri) if rev else (ci <= ri, ci >= ri)


def _head_norm_gate(hsum, g, gate):
    sq = hsum * hsum
    lane = lax.broadcasted_iota(jnp.int32, hsum.shape, 1)
    first = lane < HEAD_DIM
    s0 = jnp.sum(jnp.where(first, sq, 0.0), axis=-1, keepdims=True)
    s1 = jnp.sum(jnp.where(first, 0.0, sq), axis=-1, keepdims=True)
    ms = jnp.where(first, s0, s1) * (1.0 / HEAD_DIM)
    return hsum * lax.rsqrt(ms + EPS) * g * gate


def _finish(hf_ref, hb_ref, gate_ref, g_ref, o_ref, gate_fn):
    n = hf_ref.shape[0]
    tr = _pick_tile(n, 256)

    def body(i, carry):
        sl = pl.ds(pl.multiple_of(i * tr, tr), tr)
        o_ref[0, sl, :] = _head_norm_gate(hf_ref[sl, :] + hb_ref[sl, :], g_ref[...], gate_fn(gate_ref[0, sl, :]))
        return carry

    lax.fori_loop(0, n // tr, body, 0)


def _mlstm_chunk(q, k, v, li_c, lf_c, li_r, lf_r, C, nvec, m, masks, need_h):
    incl, incl_t = masks
    b_c = jnp.sum(jnp.where(incl, lf_r, 0.0), axis=1, keepdims=True)
    a = jnp.sum(lf_r, axis=1, keepdims=True)
    g_c = a - b_c + li_c
    m_chunk = jnp.max(g_c, axis=0, keepdims=True)
    w_c = jnp.exp(g_c - m_chunk)
    kb = k.astype(BF16)
    c_chunk = _dot((w_c * v).T.astype(BF16), kb)
    n_chunk = jnp.sum(w_c * k, axis=0, keepdims=True)
    m_new = jnp.maximum(a + m, m_chunk)
    s_old = jnp.exp(a + m - m_new)
    s_new = jnp.exp(m_chunk - m_new)
    C_new = s_old * C + s_new * c_chunk
    n_new = s_old * nvec + s_new * n_chunk
    h = None
    if need_h:
        b_r = jnp.sum(jnp.where(incl_t, lf_c, 0.0), axis=0, keepdims=True)
        dmat = jnp.where(incl, b_c - b_r + li_r, NEG_BIG)
        inter = b_c + m
        m_r = jnp.maximum(inter, jnp.max(dmat, axis=1, keepdims=True))
        qb = q.astype(BF16)
        p = jnp.exp(dmat - m_r) * _dot_nt(qb, kb)
        w_inter = jnp.exp(inter - m_r)
        num = w_inter * _dot_nt(qb, C.astype(BF16)) + _dot(p.astype(BF16), v.astype(BF16))
        den = w_inter * jnp.sum(q * nvec, axis=1, keepdims=True) + jnp.sum(p, axis=1, keepdims=True)
        h = num / jnp.maximum(jnp.abs(den), jnp.exp(-m_r))
    return h, C_new, n_new, m_new


def _mlstm_kernel(*refs, with_ctx_out):
    (qc_ref, kc_ref, vc_ref, oc_gate_ref, sc_ref, grc_ref,
     ql_ref, kl_ref, vl_ref, ol_gate_ref, sl_ref, grl_ref,
     bias_c_ref, bias_r_ref, g_ref) = refs[:15]
    if with_ctx_out:
        o_lat_ref, o_ctx_ref, hf_l, hb_l, hf_c, hb_c, C_ref, n_ref, m_ref = refs[15:]
    else:
        o_lat_ref, hf_l, hb_l, C_ref, n_ref, m_ref = refs[15:]
        o_ctx_ref = hf_c = hb_c = None
    hp = pl.program_id(1)
    C_ref[...] = jnp.zeros_like(C_ref)
    n_ref[...] = jnp.zeros_like(n_ref)
    m_ref[...] = jnp.zeros_like(m_ref)
    masks = (_chunk_masks(False), _chunk_masks(True))
    lane16 = lax.broadcasted_iota(jnp.int32, (1, 16), 1)
    sub16 = lax.broadcasted_iota(jnp.int32, (16, 1), 0)

    def gate_cols(tile):
        t = tile + bias_c_ref[...]
        return jnp.where(lane16 < 2 * M_HEADS, t, _log_sigmoid(t))

    def gate_rows(tile):
        t = tile + bias_r_ref[...]
        return jnp.where(sub16 < 2 * M_HEADS, t, _log_sigmoid(t))

    def run(q_ref, k_ref, v_ref, s_ref, gr_ref, hf, hb, need_h):
        nchunks = q_ref.shape[1] // CHUNK

        def body(t, carry):
            for d in range(2):
                c = t if d == 0 else nchunks - 1 - t
                rows = pl.ds(pl.multiple_of(c * CHUNK, CHUNK), CHUNK)
                gc = gate_cols(s_ref[0, rows, 0:16])
                gr = gate_rows(gr_ref[0, c])
                hs = []
                for hh in range(2):
                    chain = d * 2 + hh
                    lanes = slice(hh * HEAD_DIM, (hh + 1) * HEAD_DIM)
                    col = d * M_HEADS + hp * 2 + hh
                    li_c = jnp.sum(jnp.where(lane16 == col, gc, 0.0), axis=1, keepdims=True)
                    lf_c = jnp.sum(jnp.where(lane16 == col + 2 * M_HEADS, gc, 0.0), axis=1, keepdims=True)
                    li_r = jnp.sum(jnp.where(sub16 == col, gr, 0.0), axis=0, keepdims=True)
                    lf_r = jnp.sum(jnp.where(sub16 == col + 2 * M_HEADS, gr, 0.0), axis=0, keepdims=True)
                    h, C_new, n_new, m_new = _mlstm_chunk(
                        q_ref[0, rows, lanes], k_ref[0, rows, lanes] * (HEAD_DIM ** -0.5), v_ref[0, rows, lanes],
                        li_c, lf_c, li_r, lf_r, C_ref[chain], n_ref[chain], m_ref[chain], masks[d], need_h)
                    C_ref[chain] = C_new
                    n_ref[chain] = n_new
                    m_ref[chain] = m_new
                    hs.append(h)
                if need_h:
                    (hf if d == 0 else hb)[rows, :] = jnp.concatenate(hs, axis=1)
            return carry

        lax.fori_loop(0, nchunks, body, 0)

    run(qc_ref, kc_ref, vc_ref, sc_ref, grc_ref, hf_c, hb_c, with_ctx_out)
    run(ql_ref, kl_ref, vl_ref, sl_ref, grl_ref, hf_l, hb_l, True)
    _finish(hf_l, hb_l, ol_gate_ref, g_ref, o_lat_ref, jax.nn.sigmoid)
    if with_ctx_out:
        _finish(hf_c, hb_c, oc_gate_ref, g_ref, o_ctx_ref, jax.nn.sigmoid)


def _seq_specs(n, ncols_first):
    mk = lambda off: pl.BlockSpec((1, n, LANES), lambda b, hp, off=off: (b, 0, off + hp))
    return [mk(0), mk(2), mk(4), mk(6), pl.BlockSpec((1, n, SMALL_COLS), lambda b, hp: (b, 0, 0))]


def _gate_rows(s, lo, width):
    b, n, _ = s.shape
    return s[:, :, lo:lo + width].reshape(b, n // CHUNK, CHUNK, width).transpose(0, 1, 3, 2)


def _mlstm_call(mc, sc, ml, sl, i_bias, f_bias, g, with_ctx_out):
    b, n, _ = ml.shape
    nc = mc.shape[1]
    grc, grl = _gate_rows(sc, 0, 16), _gate_rows(sl, 0, 16)
    bias = jnp.concatenate([i_bias.reshape(-1), f_bias.reshape(-1)]).astype(F32)
    gr_spec = lambda nn: pl.BlockSpec((1, nn // CHUNK, 16, CHUNK), lambda bi, hp: (bi, 0, 0, 0))
    in_specs = (_seq_specs(nc, 0) + [gr_spec(nc)] + _seq_specs(n, 0) + [gr_spec(n)]
                + [pl.BlockSpec((1, 16), lambda bi, hp: (0, 0)),
                   pl.BlockSpec((16, 1), lambda bi, hp: (0, 0)),
                   pl.BlockSpec((1, LANES), lambda bi, hp: (0, hp))])
    out_specs = [pl.BlockSpec((1, n, LANES), lambda bi, hp: (bi, 0, hp))]
    out_shape = [jax.ShapeDtypeStruct((b, n, M_HEADS * HEAD_DIM), F32)]
    scratch = [pltpu.VMEM((n, LANES), F32), pltpu.VMEM((n, LANES), F32)]
    if with_ctx_out:
        out_specs.append(pl.BlockSpec((1, nc, LANES), lambda bi, hp: (bi, 0, hp)))
        out_shape.append(jax.ShapeDtypeStruct((b, nc, M_HEADS * HEAD_DIM), F32))
        scratch += [pltpu.VMEM((nc, LANES), F32), pltpu.VMEM((nc, LANES), F32)]
    scratch += [pltpu.VMEM((4, HEAD_DIM, HEAD_DIM), F32), pltpu.VMEM((4, 1, HEAD_DIM), F32),
                pltpu.VMEM((4, 1, 1), F32)]
    outs = pl.pallas_call(
        functools.partial(_mlstm_kernel, with_ctx_out=with_ctx_out),
        grid=(b, M_HEADS // 2),
        in_specs=in_specs,
        out_specs=out_specs,
        out_shape=out_shape,
        scratch_shapes=scratch,
        compiler_params=_cparams(("parallel", "arbitrary")),
        name="mlstm",
    )(mc, mc, mc, mc, sc, grc, ml, ml, ml, ml, sl, grl,
      bias.reshape(1, 16), bias.reshape(16, 1), g.reshape(1, -1))
    return (outs[0], outs[1]) if with_ctx_out else (outs[0], None)


def _gla_chunk(q, k, v, glog, bc, ST, mask, need_o):
    b_end = jnp.sum(glog, axis=0, keepdims=True)
    vb = v.astype(BF16)
    uT = _dot(v.T.astype(BF16), (k * jnp.exp(b_end - bc)).astype(BF16))
    ST_new = ST * jnp.exp(b_end) + uT
    o = None
    if need_o:
        qd = (q * jnp.exp(bc)).astype(BF16)
        kd = (k * jnp.exp(-bc)).astype(BF16)
        att = jnp.where(mask, _dot_nt(qd, kd), 0.0)
        o = _dot(att.astype(BF16), vb) + _dot_nt(qd, ST.astype(BF16))
    return o, ST_new


def _gla_kernel(*refs, with_ctx_out):
    (qc_ref, kc_ref, vc_ref, oc_gate_ref, sc_ref,
     ql_ref, kl_ref, vl_ref, ol_gate_ref, sl_ref,
     wa_ref, ba_ref, g_ref) = refs[:13]
    if with_ctx_out:
        o_lat_ref, o_ctx_ref, hf_l, hb_l, hf_c, hb_c, S_ref = refs[13:]
    else:
        o_lat_ref, hf_l, hb_l, S_ref = refs[13:]
        o_ctx_ref = hf_c = hb_c = None
    S_ref[...] = jnp.zeros_like(S_ref)
    masks = (_chunk_masks(False)[0], _chunk_masks(True)[0])
    tris = tuple(mk.astype(BF16) for mk in masks)
    lo = 2 * M_HEADS * 2

    def run(q_ref, k_ref, v_ref, s_ref, hf, hb, need_o):
        nchunks = q_ref.shape[1] // CHUNK

        def body(t, carry):
            for d in range(2):
                c = t if d == 0 else nchunks - 1 - t
                rows = pl.ds(pl.multiple_of(c * CHUNK, CHUNK), CHUNK)
                lr = s_ref[0, rows, lo + d * GLA_RANK:lo + (d + 1) * GLA_RANK]
                glog2 = _log_sigmoid(_dot_hi(lr, wa_ref[d]) + ba_ref[d]) * (1.0 / GLA_TAU)
                bc2 = _dot_exact_lhs(tris[d], glog2)
                os_ = []
                for hh in range(2):
                    chain = d * 2 + hh
                    lanes = slice(hh * HEAD_DIM, (hh + 1) * HEAD_DIM)
                    o, ST_new = _gla_chunk(
                        q_ref[0, rows, lanes] * (HEAD_DIM ** -0.5), k_ref[0, rows, lanes], v_ref[0, rows, lanes],
                        glog2[:, lanes], bc2[:, lanes], S_ref[chain], masks[d], need_o)
                    S_ref[chain] = ST_new
                    os_.append(o)
                if need_o:
                    (hf if d == 0 else hb)[rows, :] = jnp.concatenate(os_, axis=1)
            return carry

        lax.fori_loop(0, nchunks, body, 0)

    run(qc_ref, kc_ref, vc_ref, sc_ref, hf_c, hb_c, with_ctx_out)
    run(ql_ref, kl_ref, vl_ref, sl_ref, hf_l, hb_l, True)
    silu = lambda t: t * jax.nn.sigmoid(t)
    _finish(hf_l, hb_l, ol_gate_ref, g_ref, o_lat_ref, silu)
    if with_ctx_out:
        _finish(hf_c, hb_c, oc_gate_ref, g_ref, o_ctx_ref, silu)


def _gla_call(gc, sc, gl, sl, wa2, ba, g, with_ctx_out):
    b, n, _ = gl.shape
    nc = gc.shape[1]
    in_specs = (_seq_specs(nc, 0) + _seq_specs(n, 0)
                + [pl.BlockSpec((2, GLA_RANK, LANES), lambda bi, hp: (0, 0, hp)),
                   pl.BlockSpec((2, 1, LANES), lambda bi, hp: (0, 0, hp)),
                   pl.BlockSpec((1, LANES), lambda bi, hp: (0, hp))])
    out_specs = [pl.BlockSpec((1, n, LANES), lambda bi, hp: (bi, 0, hp))]
    out_shape = [jax.ShapeDtypeStruct((b, n, G_HEADS * HEAD_DIM), F32)]
    scratch = [pltpu.VMEM((n, LANES), F32), pltpu.VMEM((n, LANES), F32)]
    if with_ctx_out:
        out_specs.append(pl.BlockSpec((1, nc, LANES), lambda bi, hp: (bi, 0, hp)))
        out_shape.append(jax.ShapeDtypeStruct((b, nc, G_HEADS * HEAD_DIM), F32))
        scratch += [pltpu.VMEM((nc, LANES), F32), pltpu.VMEM((nc, LANES), F32)]
    scratch += [pltpu.VMEM((4, HEAD_DIM, HEAD_DIM), F32)]
    outs = pl.pallas_call(
        functools.partial(_gla_kernel, with_ctx_out=with_ctx_out),
        grid=(b, G_HEADS // 2),
        in_specs=in_specs,
        out_specs=out_specs,
        out_shape=out_shape,
        scratch_shapes=scratch,
        compiler_params=_cparams(("parallel", "arbitrary")),
        name="gla",
    )(gc, gc, gc, gc, sc, gl, gl, gl, gl, sl, wa2, ba.reshape(2, 1, -1), g.reshape(1, -1))
    return (outs[0], outs[1]) if with_ctx_out else (outs[0], None)


def _outmlp_kernel(*refs, final, ff_tile):
    if final:
        x_ref, oa_ref, om_ref, og_ref, mod_ref, g_ref, wo_ref, w1_ref, w2_ref, fg_ref, y_ref = refs
    else:
        x_ref, oa_ref, om_ref, og_ref, mod_ref, g_ref, wo_ref, w1_ref, w2_ref, y_ref = refs
    mod = mod_ref[0]
    a_w, m_w = oa_ref.shape[2], om_ref.shape[2]
    o = (_dot(oa_ref[0].astype(BF16), wo_ref[0:a_w, :])
         + _dot(om_ref[0].astype(BF16), wo_ref[a_w:a_w + m_w, :])
         + _dot(og_ref[0].astype(BF16), wo_ref[a_w + m_w:, :]))
    x1 = x_ref[0] + mod[2:3] * o
    hb = _norm_mod(x1, g_ref[...], mod[3:4], mod[4:5]).astype(BF16)
    d_ff = w1_ref.shape[1]
    acc = jnp.zeros(x1.shape, F32)
    for j in range(d_ff // ff_tile):
        t = jnp.maximum(_dot(hb, w1_ref[:, j * ff_tile:(j + 1) * ff_tile]), 0.0)
        acc = acc + _dot((t * t).astype(BF16), w2_ref[j * ff_tile:(j + 1) * ff_tile, :])
    x2 = x1 + mod[5:6] * acc
    if final:
        ms = jnp.mean(x2 * x2, axis=-1, keepdims=True)
        x2 = x2 * lax.rsqrt(ms + EPS) * fg_ref[...]
    y_ref[0] = x2


def _outmlp_call(x, oa, om, og, mod, g, wo, w1, w2, final_g):
    bx, n, d = x.shape
    tm = _pick_tile(n, 256)
    d_ff = w1.shape[1]
    row = lambda wd: pl.BlockSpec((1, tm, wd), lambda b, i: (b, i, 0))
    whole = lambda arr: pl.BlockSpec(arr.shape, lambda b, i: (0,) * arr.ndim)
    g2 = g.reshape(1, d)
    in_specs = [row(d), row(oa.shape[2]), row(om.shape[2]), row(og.shape[2]),
                pl.BlockSpec((1, 6, d), lambda b, i: (b, 0, 0)), whole(g2), whole(wo), whole(w1), whole(w2)]
    args = [x, oa, om, og, mod, g2, wo, w1, w2]
    if final_g is not None:
        fg = final_g.reshape(1, d)
        in_specs.append(whole(fg))
        args.append(fg)
    return pl.pallas_call(
        functools.partial(_outmlp_kernel, final=final_g is not None, ff_tile=_pick_tile(d_ff, 1024)),
        grid=(bx, n // tm),
        in_specs=in_specs,
        out_specs=row(d),
        out_shape=jax.ShapeDtypeStruct((bx, n, d), F32),
        compiler_params=_cparams(("parallel", "parallel")),
        name="outproj_mlp",
    )(*args)


def _rope_tables(n):
    t = jnp.arange(n)
    inv = ROPE_BASE ** (-jnp.arange(0, ROPE_AXIS_DIM, 2, dtype=F32) / ROPE_AXIS_DIM)
    ang_r = (t // GRID_W).astype(F32)[:, None] * inv[None, :]
    ang_c = (t % GRID_W).astype(F32)[:, None] * inv[None, :]
    half = ROPE_AXIS_DIM // 2
    cos = jnp.concatenate([jnp.cos(ang_r)] * 2 + [jnp.cos(ang_c)] * 2, axis=1)
    zero = jnp.zeros((n, half), F32)
    sin_a = jnp.concatenate([zero, jnp.sin(ang_r), zero, jnp.sin(ang_c)], axis=1)
    sin_b = jnp.concatenate([-jnp.sin(ang_r), zero, -jnp.sin(ang_c), zero], axis=1)
    return tuple(jnp.tile(a, (1, LANES // HEAD_DIM)).astype(F32) for a in (cos, sin_a, sin_b))


def _permute_w_in(w_in):
    aq, ak, av, mq, mk, mv, mo, mi, mf, gq, gk, gv, gg, glr = jnp.split(
        w_in, np.cumsum([Q_COLS, 128, 128, 256, 256, 256, 256, 8, 8, 256, 256, 256, 256])[:13].tolist(), axis=-1)
    pad = jnp.zeros(w_in.shape[:-1] + (SMALL_COLS - 16 - 2 * GLA_RANK,), w_in.dtype)
    return jnp.concatenate([aq, ak, av, mq, mk, mv, mo, gq, gk, gv, gg, mi, mf, glr, pad], axis=-1).astype(BF16)


def kernel(x, c, ctx, c_ctx, w_ada, b_ada, norm1_g, norm2_g, w_in, attn_sink, m_i_bias, m_f_bias, m_norm_g,
           g_wa2, g_ba, g_norm_g, w_out, w_mlp1, w_mlp2, final_g):
    B, N, D = x.shape
    Nc = ctx.shape[1]
    depth = w_ada.shape[0]
    rope = _rope_tables(N)
    rows = -(-(B + 1) // 8) * 8
    cc = jnp.zeros((rows, D), F32).at[:B].set(c).at[B].set(c_ctx)
    mods = _ada_call(cc, w_ada, b_ada)
    mods_x = mods[:, :B].reshape(depth, B, 6, D)
    mods_c = mods[:, B:B + 1].reshape(depth, 1, 6, D)
    w_in_p = _permute_w_in(w_in)
    wo, w1, w2 = w_out.astype(BF16), w_mlp1.astype(BF16), w_mlp2.astype(BF16)

    xc = ctx
    for l in range(depth):
        last = l == depth - 1
        q, kv, ml, gl, sl = _inproj_call(x, mods_x[l], norm1_g[l], w_in_p[l], rope)
        ctx_parts = _inproj_call(xc.reshape(1, B * Nc, D), mods_c[l], norm1_g[l], w_in_p[l], None)
        qc, kvc, mc, gc, sc = (t.reshape(B, Nc, -1) for t in ctx_parts)
        oa = _attn_call(attn_sink[l], q, kv, kvc)
        om, omc = _mlstm_call(mc, sc, ml, sl, m_i_bias[l], m_f_bias[l], m_norm_g[l], not last)
        og, ogc = _gla_call(gc, sc, gl, sl, g_wa2[l], g_ba[l], g_norm_g[l], not last)
        x = _outmlp_call(x, oa, om, og, mods_x[l], norm2_g[l], wo[l], w1[l], w2[l], final_g if last else None)
        if not last:
            oac = _attn_call(attn_sink[l], qc, None, kvc)
            flat = lambda t: t.reshape(1, B * Nc, -1)
            xc = _outmlp_call(flat(xc), flat(oac), flat(omc), flat(ogc), mods_c[l], norm2_g[l],
                              wo[l], w1[l], w2[l], None).reshape(B, Nc, D)
    return x
```

```python
import functools

import numpy as np
import jax
import jax.numpy as jnp
from jax import lax
from jax.experimental import pallas as pl
from jax.experimental.pallas import tpu as pltpu

F32 = jnp.float32
BF16 = jnp.bfloat16

HEAD_DIM = 64
GRID_W = 64
ATTN_HEADS = 8
ATTN_KV_HEADS = 2
ATTN_GROUP = ATTN_HEADS // ATTN_KV_HEADS
WINDOW = 128
ATTN_BLOCK = 128
ROPE_BASE = 10000.0
ROPE_AXIS_DIM = HEAD_DIM // 2
M_HEADS = 4
G_HEADS = 4
CHUNK = 64
GLA_GROUP = 8
M_CHUNK = 128
M_GROUP = 2
M_AUG = HEAD_DIM + 8
KT_CHUNKS = 128 // 8
GLA_RANK = 16
GLA_TAU = 16.0
EPS = 1e-6
NEG_BIG = -1e30

LANES = 128
Q_COLS = ATTN_HEADS * HEAD_DIM
KV_COLS = 2 * ATTN_KV_HEADS * HEAD_DIM
MIX_COLS = 4 * M_HEADS * HEAD_DIM
SMALL_COLS = LANES
N_MGATES = 2 * 2 * 2
VMEM_LIMIT = 56 * 1024 * 1024


def _cparams(sem):
    return pltpu.CompilerParams(dimension_semantics=sem, vmem_limit_bytes=VMEM_LIMIT)


def _pick_tile(n, pref):
    t = pref
    while n % t:
        t //= 2
    return t


def _split3(a):
    a1 = a.astype(BF16)
    r1 = a - a1.astype(F32)
    a2 = r1.astype(BF16)
    a3 = (r1 - a2.astype(F32)).astype(BF16)
    return a1, a2, a3


def _dot(a, b):
    return jnp.dot(a, b, preferred_element_type=F32)


def _dot_nt(a, b):
    return lax.dot_general(a, b, (((1,), (1,)), ((), ())), preferred_element_type=F32)


def _dot_hi(a, b):
    a1, a2, _ = _split3(a)
    b1, b2, _ = _split3(b)
    return _dot(a1, b1) + (_dot(a1, b2) + _dot(a2, b1))


def _log_sigmoid(x):
    return jnp.minimum(x, 0.0) - jnp.log1p(jnp.exp(-jnp.abs(x)))


def _ada_kernel(cc_ref, w_ref, b_ref, o_ref):
    cc = cc_ref[...]
    s = cc * jax.nn.sigmoid(cc)
    o_ref[0] = _dot_hi(s, w_ref[0]) + b_ref[0]


def _ada_call(cc, w_ada, b_ada):
    depth, d, six_d = w_ada.shape
    rows = cc.shape[0]
    tn = _pick_tile(six_d, 1024)
    return pl.pallas_call(
        _ada_kernel,
        grid=(depth, six_d // tn),
        in_specs=[pl.BlockSpec((rows, d), lambda l, j: (0, 0)),
                  pl.BlockSpec((1, d, tn), lambda l, j: (l, 0, j)),
                  pl.BlockSpec((1, 1, tn), lambda l, j: (l, 0, j))],
        out_specs=pl.BlockSpec((1, rows, tn), lambda l, j: (l, 0, j)),
        out_shape=jax.ShapeDtypeStruct((depth, rows, six_d), F32),
        compiler_params=_cparams(("arbitrary", "arbitrary")),
        name="ada_mod",
    )(cc, w_ada, b_ada.reshape(depth, 1, six_d))


def _norm_mod(x, g, shift, scale):
    ms = jnp.mean(x * x, axis=-1, keepdims=True)
    h = x * lax.rsqrt(ms + EPS) * g
    return h * (1.0 + scale) + shift


def _inproj_kernel(*refs, use_rope):
    if use_rope:
        x_ref, mod_ref, g_ref, w_ref, cos_ref, sa_ref, sb_ref, q_ref, kv_ref, m_ref, gl_ref, s_ref = refs
    else:
        x_ref, mod_ref, g_ref, w_ref, q_ref, kv_ref, m_ref, gl_ref, s_ref = refs
    mod = mod_ref[0]
    hb = _norm_mod(x_ref[0], g_ref[...], mod[0:1], mod[1:2]).astype(BF16)

    def proj(lo, width):
        return _dot(hb, w_ref[:, lo:lo + width])

    pa = proj(0, Q_COLS + KV_COLS)
    if use_rope:
        cos, sa, sb = cos_ref[...], sa_ref[...], sb_ref[...]
        segs = []
        for j in range((Q_COLS + KV_COLS // 2) // LANES):
            seg = pa[:, j * LANES:(j + 1) * LANES]
            segs.append(seg * cos + pltpu.roll(seg, ROPE_AXIS_DIM // 2, 1) * sa
                        + pltpu.roll(seg, LANES - ROPE_AXIS_DIM // 2, 1) * sb)
        qk = jnp.concatenate(segs, axis=1)
    else:
        qk = pa[:, :Q_COLS + KV_COLS // 2]
    q_ref[0] = qk[:, :Q_COLS] * (HEAD_DIM ** -0.5)
    kv_ref[0, :, :KV_COLS // 2] = qk[:, Q_COLS:]
    kv_ref[0, :, KV_COLS // 2:] = pa[:, Q_COLS + KV_COLS // 2:]
    lo = Q_COLS + KV_COLS
    m_ref[0] = proj(lo, MIX_COLS)
    gl_ref[0] = proj(lo + MIX_COLS, MIX_COLS)
    s_ref[0] = proj(lo + 2 * MIX_COLS, SMALL_COLS)


def _inproj_call(x, mod, g, w, rope):
    bx, n, d = x.shape
    tm = _pick_tile(n, 512)
    cols = w.shape[1]
    in_specs = [pl.BlockSpec((1, tm, d), lambda b, i: (b, i, 0)),
                pl.BlockSpec((1, 6, d), lambda b, i: (b, 0, 0)),
                pl.BlockSpec((1, d), lambda b, i: (0, 0)),
                pl.BlockSpec((d, cols), lambda b, i: (0, 0))]
    args = [x, mod, g.reshape(1, d), w]
    if rope is not None:
        in_specs += [pl.BlockSpec((tm, LANES), lambda b, i: (i, 0))] * 3
        args += list(rope)
    widths = (Q_COLS, KV_COLS, MIX_COLS, MIX_COLS, SMALL_COLS)
    return pl.pallas_call(
        functools.partial(_inproj_kernel, use_rope=rope is not None),
        grid=(bx, n // tm),
        in_specs=in_specs,
        out_specs=[pl.BlockSpec((1, tm, wd), lambda b, i: (b, i, 0)) for wd in widths],
        out_shape=[jax.ShapeDtypeStruct((bx, n, wd), F32) for wd in widths],
        compiler_params=_cparams(("parallel", "parallel")),
        name="inproj_rope" if rope is not None else "inproj",
    )(*args)


def _attn_kernel(*refs, local, tq):
    if local:
        sink_ref, q_ref, kv_ref, kvc_ref, o_ref = refs
    else:
        sink_ref, q_ref, kvc_ref, o_ref = refs
    rows = ATTN_GROUP * tq
    q = q_ref[0]
    if local:
        n = kv_ref.shape[1]
        span = 3 * ATTN_BLOCK
        j = pl.program_id(1)
        start = pl.multiple_of(jnp.clip((j - 1) * ATTN_BLOCK, 0, n - span), ATTN_BLOCK)
        qpos = j * ATTN_BLOCK + (lax.broadcasted_iota(jnp.int32, (rows, span), 0) & (tq - 1))
        kpos = start + lax.broadcasted_iota(jnp.int32, (rows, span), 1)
        valid = jnp.abs(qpos - kpos) <= WINDOW
    for kvh in range(ATTN_KV_HEADS):
        heads = [kvh * ATTN_GROUP + g for g in range(ATTN_GROUP)]
        qs = jnp.concatenate([q[:, h * HEAD_DIM:(h + 1) * HEAD_DIM] for h in heads], axis=0).astype(BF16)
        sink = jnp.concatenate([jnp.full((tq, 1), sink_ref[h], F32) for h in heads], axis=0)
        k_lo, v_lo = kvh * HEAD_DIM, KV_COLS // 2 + kvh * HEAD_DIM
        kc = kvc_ref[0, :, k_lo:k_lo + HEAD_DIM].astype(BF16)
        vc = kvc_ref[0, :, v_lo:v_lo + HEAD_DIM].astype(BF16)
        s_ctx = _dot_nt(qs, kc)
        m = jnp.maximum(jnp.max(s_ctx, axis=-1, keepdims=True), sink)
        if local:
            kl = kv_ref[0, pl.ds(start, span), k_lo:k_lo + HEAD_DIM].astype(BF16)
            vl = kv_ref[0, pl.ds(start, span), v_lo:v_lo + HEAD_DIM].astype(BF16)
            s_loc = jnp.where(valid, _dot_nt(qs, kl), NEG_BIG)
            m = jnp.maximum(m, jnp.max(s_loc, axis=-1, keepdims=True))
        e_ctx = jnp.exp(s_ctx - m)
        den = jnp.sum(e_ctx, axis=-1, keepdims=True) + jnp.exp(sink - m)
        acc = _dot(e_ctx.astype(BF16), vc)
        if local:
            e_loc = jnp.exp(s_loc - m)
            den = den + jnp.sum(e_loc, axis=-1, keepdims=True)
            acc = acc + _dot(e_loc.astype(BF16), vl)
        o = acc / den
        for g, h in enumerate(heads):
            o_ref[0, :, h * HEAD_DIM:(h + 1) * HEAD_DIM] = o[g * tq:(g + 1) * tq]


def _attn_call(sink, q, kv, kvc):
    b, nq, _ = q.shape
    nc = kvc.shape[1]
    local = kv is not None
    tq = ATTN_BLOCK if local else nq
    in_specs = [pl.BlockSpec(memory_space=pltpu.SMEM),
                pl.BlockSpec((1, tq, Q_COLS), lambda bi, j: (bi, j, 0))]
    args = [sink, q]
    if local:
        in_specs.append(pl.BlockSpec((1, nq, KV_COLS), lambda bi, j: (bi, 0, 0)))
        args.append(kv)
    in_specs.append(pl.BlockSpec((1, nc, KV_COLS), lambda bi, j: (bi, 0, 0)))
    args.append(kvc)
    return pl.pallas_call(
        functools.partial(_attn_kernel, local=local, tq=tq),
        grid=(b, nq // tq),
        in_specs=in_specs,
        out_specs=pl.BlockSpec((1, tq, Q_COLS), lambda bi, j: (bi, j, 0)),
        out_shape=jax.ShapeDtypeStruct((b, nq, Q_COLS), F32),
        compiler_params=_cparams(("parallel", "arbitrary")),
        name="attn_window" if local else "attn_ctx",
    )(*args)


def _tri(n, rev):
    ri = lax.broadcasted_iota(jnp.int32, (n, n), 0)
    ci = lax.broadcasted_iota(jnp.int32, (n, n), 1)
    return ci >= ri if rev else ci <= ri


def _seq_specs(n):
    return [pl.BlockSpec((1, n, LANES), lambda b, hp, off=off: (b, 0, off + hp)) for off in (0, 2, 4, 6)]


def _group(nchunks, pref):
    g = min(pref, nchunks)
    assert nchunks % g == 0
    return g


def _mlstm_kernel(*refs, with_ctx_out):
    (qc_ref, kc_ref, vc_ref, oc_gate_ref, grc_ref,
     ql_ref, kl_ref, vl_ref, ol_gate_ref, grl_ref, bias_ref, g_ref) = refs[:12]
    if with_ctx_out:
        o_lat_ref, o_ctx_ref = refs[12:14]
        scratch = refs[14:]
    else:
        o_lat_ref, o_ctx_ref = refs[12], None
        scratch = refs[13:]
    w_ref, b_ref, ck_ref, tot_ref, mc_ref, mp_ref, kt_ref, cc_ref, vt_ref = scratch
    L = M_CHUNK
    ncc, ncl = qc_ref.shape[1] // L, ql_ref.shape[1] // L
    kt_c = -(-ncc // KT_CHUNKS)

    mask_t = (_tri(L, True), _tri(L, False))
    tri_f, tri_b = (mk.astype(BF16) for mk in mask_t)
    ones_m = jnp.ones((L, L), BF16)
    row8 = lax.broadcasted_iota(jnp.int32, (8, L), 0)
    rev8 = (row8 & 2) != 0
    ones8 = jnp.where(row8 == 0, 1.0, 0.0).astype(F32)

    def prep(gr_ref, v_ref, base, kt_base, v_base):
        nch = gr_ref.shape[3]
        li = (gr_ref[0, 0, 0] + bias_ref[0, 0]).reshape(nch * 8, L)
        lf = _log_sigmoid(gr_ref[0, 0, 1] + bias_ref[0, 1]).reshape(nch * 8, L)
        rev = (lax.broadcasted_iota(jnp.int32, li.shape, 0) & 2) != 0
        lane = lax.broadcasted_iota(jnp.int32, li.shape, 1)
        l1, l2, l3 = _split3(lf)
        scan = lambda m: _dot(l1, m) + (_dot(l2, m) + _dot(l3, m))
        b = jnp.where(rev, scan(tri_b), scan(tri_f))
        tot = scan(ones_m)
        g = tot - b + li
        mc = jnp.max(g, axis=1, keepdims=True)
        kap = li - b
        ckf, ckb = kap, kap
        s = 1
        while s < L:
            ckf = jnp.maximum(ckf, jnp.where(lane >= s, pltpu.roll(ckf, s, 1), NEG_BIG))
            ckb = jnp.maximum(ckb, jnp.where(lane < L - s, pltpu.roll(ckb, L - s, 1), NEG_BIG))
            s *= 2
        sl = pl.ds(base, nch)
        w_ref[sl] = jnp.exp(g - mc).reshape(nch, 8, L)
        b_ref[sl] = b.reshape(nch, 8, L)
        ck_ref[sl] = jnp.where(rev, ckb, ckf).reshape(nch, 8, L)
        tot_ref[sl] = tot.reshape(nch, 8, L)
        mc_ref[sl] = jnp.broadcast_to(mc, (nch * 8, L)).reshape(nch, 8, L)
        for gi in range(-(-nch // KT_CHUNKS)):
            blk = kap[gi * LANES:(gi + 1) * LANES]
            if blk.shape[0] < LANES:
                blk = jnp.concatenate([blk, jnp.zeros((LANES - blk.shape[0], L), F32)], axis=0)
            kt_ref[kt_base + gi] = blk.T

        def vt_body(i, carry):
            rows = pl.ds(pl.multiple_of(i * L, L), L)
            vt_ref[:, pl.ds(pl.multiple_of(v_base + i * L, L), L)] = v_ref[0, rows, :].T
            return carry

        lax.fori_loop(0, nch, vt_body, 0)

    def vaug(vt2, hh):
        return jnp.concatenate([vt2[hh * HEAD_DIM:(hh + 1) * HEAD_DIM], ones8], axis=0)

    def summarize(k_ref, base, v_base):
        nch = k_ref.shape[1] // L
        grp = _group(nch, M_GROUP)

        def body(t, carry):
            lhs, rhs = {}, {}
            for gi in range(grp):
                cl = t * grp + gi
                k2 = (k_ref[0, pl.ds(pl.multiple_of(cl * L, L), L), :] * (HEAD_DIM ** -0.5)).astype(BF16)
                vt2 = vt_ref[:, pl.ds(pl.multiple_of(v_base + cl * L, L), L)]
                w8 = w_ref[base + cl]
                for hh in range(2):
                    va = vaug(vt2, hh)
                    lhs[gi, hh] = jnp.concatenate([va * w8[d * 2 + hh:d * 2 + hh + 1] for d in range(2)],
                                                  axis=0).astype(BF16)
                    rhs[gi, hh] = k2[:, hh * HEAD_DIM:(hh + 1) * HEAD_DIM]
            cc = {key: _dot(lhs[key], rhs[key]) for key in lhs}
            for (gi, hh), val in cc.items():
                for d in range(2):
                    cc_ref[base + t * grp + gi, d * 2 + hh] = val[d * M_AUG:(d + 1) * M_AUG]
            return carry

        lax.fori_loop(0, nch // grp, body, 0)

    def scan_states(base, nch, state):
        def body(t, st):
            m8, cs = st[0], list(st[1:])
            cf, cb = base + t, base + nch - 1 - t
            tot8 = jnp.where(rev8, tot_ref[cb], tot_ref[cf])
            mc8 = jnp.where(rev8, mc_ref[cb], mc_ref[cf])
            m_new = jnp.maximum(tot8 + m8, mc8)
            s_old = jnp.exp(tot8 + m8 - m_new)
            s_new = jnp.exp(mc8 - m_new)
            mp_ref[cf, 0:2, :] = m8[0:2]
            mp_ref[cb, 2:4, :] = m8[2:4]
            for j in range(4):
                c = cf if j < 2 else cb
                summary = cc_ref[c, j]
                cc_ref[c, j] = cs[j]
                cs[j] = s_old[j:j + 1, 0:HEAD_DIM] * cs[j] + s_new[j:j + 1, 0:HEAD_DIM] * summary
            return (m_new, *cs)

        return lax.fori_loop(0, nch, body, state)

    def emit(q_ref, k_ref, gate_ref, o_ref, base, kt_base, v_base):
        nch = q_ref.shape[1] // L
        grp = _group(nch, M_GROUP)

        def body(t, carry):
            units = [(gi, hh) for gi in range(grp) for hh in range(2)]
            rows, vas, rho, alpha, floor, kcol, lhs, rhs = {}, {}, {}, {}, {}, {}, {}, {}
            for gi in range(grp):
                cl = t * grp + gi
                c = base + cl
                rows[gi] = pl.ds(pl.multiple_of(cl * L, L), L)
                q2 = q_ref[0, rows[gi], :].astype(BF16)
                k2 = k_ref[0, rows[gi], :] * (HEAD_DIM ** -0.5)
                vt2 = vt_ref[:, pl.ds(pl.multiple_of(v_base + cl * L, L), L)]
                mp8 = mp_ref[c]
                rho[gi] = -jnp.maximum(mp8, ck_ref[c])
                alpha[gi] = jnp.exp(mp8 + rho[gi])
                floor[gi] = jnp.exp(rho[gi] - b_ref[c])
                kt = kt_ref[kt_base + (cl >> 4)]
                kcol[gi] = pltpu.roll(kt, (LANES - (cl & (KT_CHUNKS - 1)) * 8) & (LANES - 1), 1)
                for hh in range(2):
                    ln = slice(hh * HEAD_DIM, (hh + 1) * HEAD_DIM)
                    vas[gi, hh] = vaug(vt2, hh).astype(BF16)
                    lhs[gi, hh] = jnp.concatenate([k2[:, ln], cc_ref[c, hh], cc_ref[c, 2 + hh]], axis=0).astype(BF16)
                    rhs[gi, hh] = q2[:, ln]
            prod = {u: _dot_nt(lhs[u], rhs[u]) for u in units}
            pt = {}
            for gi, hh in units:
                st = prod[gi, hh][0:L]
                for d in range(2):
                    j = d * 2 + hh
                    e = kcol[gi][:, j:j + 1] + rho[gi][j:j + 1, :]
                    pt[gi, hh, d] = (jnp.where(mask_t[d], jnp.exp(e), 0.0) * st).astype(BF16)
            pv = {key: _dot(vas[key[0], key[1]], val) for key, val in pt.items()}
            for gi in range(grp):
                halves = []
                for hh in range(2):
                    hsum = None
                    for d in range(2):
                        j = d * 2 + hh
                        lo = L + d * M_AUG
                        num = alpha[gi][j:j + 1, :] * prod[gi, hh][lo:lo + M_AUG] + pv[gi, hh, d]
                        h = num[0:HEAD_DIM] / jnp.maximum(jnp.abs(num[HEAD_DIM:HEAD_DIM + 1]), floor[gi][j:j + 1, :])
                        hsum = h if hsum is None else hsum + h
                    ms = jnp.sum(hsum * hsum, axis=0, keepdims=True) * (1.0 / HEAD_DIM)
                    halves.append(hsum * lax.rsqrt(ms + EPS))
                y = jnp.concatenate(halves, axis=0).T
                o_ref[0, rows[gi], :] = y * g_ref[...] * jax.nn.sigmoid(gate_ref[0, rows[gi], :])
            return carry

        lax.fori_loop(0, nch // grp, body, 0)

    n_ctx = qc_ref.shape[1]
    prep(grc_ref, vc_ref, 0, 0, 0)
    prep(grl_ref, vl_ref, ncc, kt_c, n_ctx)
    summarize(kc_ref, 0, 0)
    summarize(kl_ref, ncc, n_ctx)
    zero = (jnp.zeros((8, L), F32),) + (jnp.zeros((M_AUG, HEAD_DIM), F32),) * 4
    state = scan_states(0, ncc, zero)
    scan_states(ncc, ncl, state)
    if with_ctx_out:
        emit(qc_ref, kc_ref, oc_gate_ref, o_ctx_ref, 0, 0, 0)
    emit(ql_ref, kl_ref, ol_gate_ref, o_lat_ref, ncc, kt_c, n_ctx)


def _mlstm_gates(s):
    b, n, _ = s.shape
    g = s[:, :, :2 * N_MGATES].reshape(b, n, 2, 2, 2, 2)
    g = g.transpose(0, 4, 2, 3, 5, 1).reshape(b, 2, 2, 4, n // M_CHUNK, M_CHUNK).transpose(0, 1, 2, 4, 3, 5)
    return jnp.concatenate([g, g], axis=4)


def _mlstm_call(mc, sc, ml, sl, i_bias, f_bias, g, with_ctx_out):
    b, n, _ = ml.shape
    nc = mc.shape[1]
    assert n % M_CHUNK == 0 and nc % M_CHUNK == 0
    ncc, ncl = nc // M_CHUNK, n // M_CHUNK
    bias = jnp.stack([i_bias, f_bias]).astype(F32).reshape(2, 2, 2, 2).transpose(2, 0, 1, 3).reshape(2, 2, 4)
    bias = jnp.concatenate([bias, bias], axis=2)[..., None]
    gr_spec = lambda nch: pl.BlockSpec((1, 1, 2, nch, 8, M_CHUNK), lambda bi, hp: (bi, hp, 0, 0, 0, 0))
    in_specs = (_seq_specs(nc) + [gr_spec(ncc)] + _seq_specs(n) + [gr_spec(ncl)]
                + [pl.BlockSpec((1, 2, 8, 1), lambda bi, hp: (hp, 0, 0, 0)),
                   pl.BlockSpec((1, LANES), lambda bi, hp: (0, hp))])
    out_specs = [pl.BlockSpec((1, n, LANES), lambda bi, hp: (bi, 0, hp))]
    out_shape = [jax.ShapeDtypeStruct((b, n, M_HEADS * HEAD_DIM), F32)]
    if with_ctx_out:
        out_specs.append(pl.BlockSpec((1, nc, LANES), lambda bi, hp: (bi, 0, hp)))
        out_shape.append(jax.ShapeDtypeStruct((b, nc, M_HEADS * HEAD_DIM), F32))
    tot = ncc + ncl
    per_chunk = pltpu.VMEM((tot, 8, M_CHUNK), F32)
    scratch = [per_chunk] * 6 + [
        pltpu.VMEM((-(-ncc // KT_CHUNKS) + -(-ncl // KT_CHUNKS), M_CHUNK, LANES), F32),
        pltpu.VMEM((tot, 4, M_AUG, HEAD_DIM), F32),
        pltpu.VMEM((LANES, nc + n), F32)]
    outs = pl.pallas_call(
        functools.partial(_mlstm_kernel, with_ctx_out=with_ctx_out),
        grid=(b, M_HEADS // 2),
        in_specs=in_specs,
        out_specs=out_specs,
        out_shape=out_shape,
        scratch_shapes=scratch,
        compiler_params=_cparams(("parallel", "arbitrary")),
        name="mlstm",
    )(mc, mc, mc, mc, _mlstm_gates(sc), ml, ml, ml, ml, _mlstm_gates(sl), bias, g.reshape(1, -1))
    return (outs[0], outs[1]) if with_ctx_out else (outs[0], None)


def _head_norm_gate(hsum, g, gate):
    sq = hsum * hsum
    lane = lax.broadcasted_iota(jnp.int32, hsum.shape, 1)
    first = lane < HEAD_DIM
    s0 = jnp.sum(jnp.where(first, sq, 0.0), axis=-1, keepdims=True)
    s1 = jnp.sum(jnp.where(first, 0.0, sq), axis=-1, keepdims=True)
    ms = jnp.where(first, s0, s1) * (1.0 / HEAD_DIM)
    return hsum * lax.rsqrt(ms + EPS) * g * gate


def _seg_scan_sum(x, rev):
    rows = x.shape[0]
    pos = lax.broadcasted_iota(jnp.int32, x.shape, 0) & (CHUNK - 1)
    s = 1
    while s < CHUNK:
        if rev:
            x = x + jnp.where(pos < CHUNK - s, pltpu.roll(x, rows - s, 0), 0.0)
        else:
            x = x + jnp.where(pos >= s, pltpu.roll(x, s, 0), 0.0)
        s *= 2
    return x


def _gla_kernel(*refs, with_ctx_out):
    (qc_ref, kc_ref, vc_ref, oc_gate_ref, sc_ref,
     ql_ref, kl_ref, vl_ref, ol_gate_ref, sl_ref, wa_ref, ba_ref, g_ref) = refs[:13]
    if with_ctx_out:
        o_lat_ref, o_ctx_ref, bc_ref, u_ref, dec_ref = refs[13:]
    else:
        o_lat_ref, bc_ref, u_ref, dec_ref = refs[13:]
        o_ctx_ref = None
    ncc, ncl = qc_ref.shape[1] // CHUNK, ql_ref.shape[1] // CHUNK
    lo = 2 * N_MGATES
    mask_f, mask_b = _tri(CHUNK, False), _tri(CHUNK, True)
    first = lax.broadcasted_iota(jnp.int32, (HEAD_DIM, LANES), 1) < HEAD_DIM
    halves = (slice(0, HEAD_DIM), slice(HEAD_DIM, 2 * HEAD_DIM))
    silu = lambda t: t * jax.nn.sigmoid(t)

    def slab_of(t, grp):
        return pl.multiple_of(t * (grp * CHUNK), grp * CHUNK)

    def summarize(k_ref, v_ref, s_ref, base):
        nch = k_ref.shape[1] // CHUNK
        grp = _group(nch, GLA_GROUP)
        assert grp % 2 == 0

        def body(t, carry):
            r0 = slab_of(t, grp)
            slab = pl.ds(r0, grp * CHUNK)
            k3 = k_ref[0, slab, :].reshape(grp, CHUNK, LANES)
            v2 = v_ref[0, slab, :]
            kws, ends = [], []
            for d in range(2):
                lr = s_ref[0, slab, lo + d * GLA_RANK:lo + (d + 1) * GLA_RANK]
                glog = _log_sigmoid(_dot_hi(lr, wa_ref[d]) + ba_ref[d]) * (1.0 / GLA_TAU)
                bc = _seg_scan_sum(glog, d == 1)
                bc_ref[d, pl.ds(pl.multiple_of(base * CHUNK + r0, CHUNK), grp * CHUNK), :] = bc
                bc3 = bc.reshape(grp, CHUNK, LANES)
                end = bc3[:, 0:1, :] if d == 1 else bc3[:, CHUNK - 1:CHUNK, :]
                kws.append((k3 * jnp.exp(end - bc3)).astype(BF16))
                ends.append(end)
            vts = [v2[p * LANES:(p + 1) * LANES, :].T.astype(BF16) for p in range(grp // 2)]
            us = {}
            for gi in range(grp):
                for hh in range(2):
                    vt = vts[gi // 2][halves[hh], halves[gi % 2]]
                    for d in range(2):
                        us[gi, hh, d] = _dot(vt, kws[d][gi][:, halves[hh]])
            for gi in range(grp):
                c = base + t * grp + gi
                for hh in range(2):
                    u_ref[hh, c] = jnp.concatenate([us[gi, hh, 0], us[gi, hh, 1]], axis=1)
                    end2 = jnp.concatenate([ends[0][gi][:, halves[hh]], ends[1][gi][:, halves[hh]]], axis=1)
                    dec_ref[hh, c] = jnp.broadcast_to(jnp.exp(end2), (8, LANES))
            return carry

        lax.fori_loop(0, nch // grp, body, 0)

    def scan_states(base, nch, state):
        def body(t, st):
            cf, cb = base + t, base + nch - 1 - t
            new = []
            for hh in range(2):
                u = jnp.where(first, u_ref[hh, cf], u_ref[hh, cb])
                dec = jnp.where(first[0:1], dec_ref[hh, cf][0:1], dec_ref[hh, cb][0:1])
                u_ref[hh, cf, :, 0:HEAD_DIM] = st[hh][:, 0:HEAD_DIM]
                u_ref[hh, cb, :, HEAD_DIM:] = st[hh][:, HEAD_DIM:]
                new.append(st[hh] * dec + u)
            return tuple(new)

        return lax.fori_loop(0, nch, body, state)

    def emit(q_ref, k_ref, v_ref, gate_ref, o_ref, base):
        nch = q_ref.shape[1] // CHUNK
        grp = _group(nch, GLA_GROUP)

        def body(t, carry):
            r0 = slab_of(t, grp)
            slab = pl.ds(r0, grp * CHUNK)
            q2 = q_ref[0, slab, :] * (HEAD_DIM ** -0.5)
            k2 = k_ref[0, slab, :]
            vb = v_ref[0, slab, :].astype(BF16)
            qd, kd = [], []
            for d in range(2):
                bc = bc_ref[d, pl.ds(pl.multiple_of(base * CHUNK + r0, CHUNK), grp * CHUNK), :]
                qd.append((q2 * jnp.exp(bc)).astype(BF16))
                kd.append((k2 * jnp.exp(-bc)).astype(BF16))
            units = [(gi, hh) for gi in range(grp) for hh in range(2)]
            rs = lambda gi: slice(gi * CHUNK, (gi + 1) * CHUNK)
            att = {}
            for gi, hh in units:
                a_f = jnp.where(mask_f, _dot_nt(qd[0][rs(gi), halves[hh]], kd[0][rs(gi), halves[hh]]), 0.0)
                a_b = jnp.where(mask_b, _dot_nt(qd[1][rs(gi), halves[hh]], kd[1][rs(gi), halves[hh]]), 0.0)
                att[gi, hh] = (a_f + a_b).astype(BF16)
            outs = []
            for gi in range(grp):
                c = base + t * grp + gi
                parts = []
                for hh in range(2):
                    qcat = jnp.concatenate([qd[0][rs(gi), halves[hh]], qd[1][rs(gi), halves[hh]]], axis=1)
                    parts.append(_dot(att[gi, hh], vb[rs(gi), halves[hh]])
                                 + _dot_nt(qcat, u_ref[hh, c].astype(BF16)))
                outs.append(jnp.concatenate(parts, axis=1))
            o2 = jnp.concatenate(outs, axis=0)
            o_ref[0, slab, :] = _head_norm_gate(o2, g_ref[...], silu(gate_ref[0, slab, :]))
            return carry

        lax.fori_loop(0, nch // grp, body, 0)

    summarize(kc_ref, vc_ref, sc_ref, 0)
    summarize(kl_ref, vl_ref, sl_ref, ncc)
    state = scan_states(0, ncc, (jnp.zeros((HEAD_DIM, LANES), F32),) * 2)
    scan_states(ncc, ncl, state)
    if with_ctx_out:
        emit(qc_ref, kc_ref, vc_ref, oc_gate_ref, o_ctx_ref, 0)
    emit(ql_ref, kl_ref, vl_ref, ol_gate_ref, o_lat_ref, ncc)


def _gla_call(gc, sc, gl, sl, wa2, ba, g, with_ctx_out):
    b, n, _ = gl.shape
    nc = gc.shape[1]
    tot = (nc + n) // CHUNK
    small = lambda nn: pl.BlockSpec((1, nn, SMALL_COLS), lambda bi, hp: (bi, 0, 0))
    in_specs = (_seq_specs(nc) + [small(nc)] + _seq_specs(n) + [small(n)]
                + [pl.BlockSpec((2, GLA_RANK, LANES), lambda bi, hp: (0, 0, hp)),
                   pl.BlockSpec((2, 1, LANES), lambda bi, hp: (0, 0, hp)),
                   pl.BlockSpec((1, LANES), lambda bi, hp: (0, hp))])
    out_specs = [pl.BlockSpec((1, n, LANES), lambda bi, hp: (bi, 0, hp))]
    out_shape = [jax.ShapeDtypeStruct((b, n, G_HEADS * HEAD_DIM), F32)]
    if with_ctx_out:
        out_specs.append(pl.BlockSpec((1, nc, LANES), lambda bi, hp: (bi, 0, hp)))
        out_shape.append(jax.ShapeDtypeStruct((b, nc, G_HEADS * HEAD_DIM), F32))
    scratch = [pltpu.VMEM((2, nc + n, LANES), F32),
               pltpu.VMEM((2, tot, HEAD_DIM, LANES), F32),
               pltpu.VMEM((2, tot, 8, LANES), F32)]
    outs = pl.pallas_call(
        functools.partial(_gla_kernel, with_ctx_out=with_ctx_out),
        grid=(b, G_HEADS // 2),
        in_specs=in_specs,
        out_specs=out_specs,
        out_shape=out_shape,
        scratch_shapes=scratch,
        compiler_params=_cparams(("parallel", "arbitrary")),
        name="gla",
    )(gc, gc, gc, gc, sc, gl, gl, gl, gl, sl, wa2, ba.reshape(2, 1, -1), g.reshape(1, -1))
    return (outs[0], outs[1]) if with_ctx_out else (outs[0], None)


def _outmlp_kernel(*refs, final, ff_tile):
    if final:
        x_ref, oa_ref, om_ref, og_ref, mod_ref, g_ref, wo_ref, w1_ref, w2_ref, fg_ref, y_ref = refs
    else:
        x_ref, oa_ref, om_ref, og_ref, mod_ref, g_ref, wo_ref, w1_ref, w2_ref, y_ref = refs
    mod = mod_ref[0]
    a_w, m_w = oa_ref.shape[2], om_ref.shape[2]
    o = (_dot(oa_ref[0].astype(BF16), wo_ref[0:a_w, :])
         + _dot(om_ref[0].astype(BF16), wo_ref[a_w:a_w + m_w, :])
         + _dot(og_ref[0].astype(BF16), wo_ref[a_w + m_w:, :]))
    x1 = x_ref[0] + mod[2:3] * o
    hb = _norm_mod(x1, g_ref[...], mod[3:4], mod[4:5]).astype(BF16)
    d_ff = w1_ref.shape[1]
    acc = jnp.zeros(x1.shape, F32)
    for j in range(d_ff // ff_tile):
        t = jnp.maximum(_dot(hb, w1_ref[:, j * ff_tile:(j + 1) * ff_tile]), 0.0)
        acc = acc + _dot((t * t).astype(BF16), w2_ref[j * ff_tile:(j + 1) * ff_tile, :])
    x2 = x1 + mod[5:6] * acc
    if final:
        ms = jnp.mean(x2 * x2, axis=-1, keepdims=True)
        x2 = x2 * lax.rsqrt(ms + EPS) * fg_ref[...]
    y_ref[0] = x2


def _outmlp_call(x, oa, om, og, mod, g, wo, w1, w2, final_g):
    bx, n, d = x.shape
    tm = _pick_tile(n, 256)
    d_ff = w1.shape[1]
    row = lambda wd: pl.BlockSpec((1, tm, wd), lambda b, i: (b, i, 0))
    whole = lambda arr: pl.BlockSpec(arr.shape, lambda b, i: (0,) * arr.ndim)
    g2 = g.reshape(1, d)
    in_specs = [row(d), row(oa.shape[2]), row(om.shape[2]), row(og.shape[2]),
                pl.BlockSpec((1, 6, d), lambda b, i: (b, 0, 0)), whole(g2), whole(wo), whole(w1), whole(w2)]
    args = [x, oa, om, og, mod, g2, wo, w1, w2]
    if final_g is not None:
        fg = final_g.reshape(1, d)
        in_specs.append(whole(fg))
        args.append(fg)
    return pl.pallas_call(
        functools.partial(_outmlp_kernel, final=final_g is not None, ff_tile=_pick_tile(d_ff, 1024)),
        grid=(bx, n // tm),
        in_specs=in_specs,
        out_specs=row(d),
        out_shape=jax.ShapeDtypeStruct((bx, n, d), F32),
        compiler_params=_cparams(("parallel", "parallel")),
        name="outproj_mlp",
    )(*args)


def _rope_tables(n):
    t = jnp.arange(n)
    inv = ROPE_BASE ** (-jnp.arange(0, ROPE_AXIS_DIM, 2, dtype=F32) / ROPE_AXIS_DIM)
    ang_r = (t // GRID_W).astype(F32)[:, None] * inv[None, :]
    ang_c = (t % GRID_W).astype(F32)[:, None] * inv[None, :]
    half = ROPE_AXIS_DIM // 2
    cos = jnp.concatenate([jnp.cos(ang_r)] * 2 + [jnp.cos(ang_c)] * 2, axis=1)
    zero = jnp.zeros((n, half), F32)
    sin_a = jnp.concatenate([zero, jnp.sin(ang_r), zero, jnp.sin(ang_c)], axis=1)
    sin_b = jnp.concatenate([-jnp.sin(ang_r), zero, -jnp.sin(ang_c), zero], axis=1)
    return tuple(jnp.tile(a, (1, LANES // HEAD_DIM)).astype(F32) for a in (cos, sin_a, sin_b))


def _permute_w_in(w_in):
    aq, ak, av, mq, mk, mv, mo, mi, mf, gq, gk, gv, gg, glr = jnp.split(
        w_in, np.cumsum([Q_COLS, 128, 128, 256, 256, 256, 256, 8, 8, 256, 256, 256, 256])[:13].tolist(), axis=-1)
    pad = jnp.zeros(w_in.shape[:-1] + (SMALL_COLS - 16 - 2 * GLA_RANK,), w_in.dtype)
    return jnp.concatenate([aq, ak, av, mq, mk, mv, mo, gq, gk, gv, gg, mi, mf, glr, pad], axis=-1).astype(BF16)


def kernel(x, c, ctx, c_ctx, w_ada, b_ada, norm1_g, norm2_g, w_in, attn_sink, m_i_bias, m_f_bias, m_norm_g,
           g_wa2, g_ba, g_norm_g, w_out, w_mlp1, w_mlp2, final_g):
    B, N, D = x.shape
    Nc = ctx.shape[1]
    depth = w_ada.shape[0]
    rope = _rope_tables(N)
    rows = -(-(B + 1) // 8) * 8
    cc = jnp.zeros((rows, D), F32).at[:B].set(c).at[B].set(c_ctx)
    mods = _ada_call(cc, w_ada, b_ada)
    mods_x = mods[:, :B].reshape(depth, B, 6, D)
    mods_c = mods[:, B:B + 1].reshape(depth, 1, 6, D)
    w_in_p = _permute_w_in(w_in)
    wo, w1, w2 = w_out.astype(BF16), w_mlp1.astype(BF16), w_mlp2.astype(BF16)

    xc = ctx
    for l in range(depth):
        last = l == depth - 1
        q, kv, ml, gl, sl = _inproj_call(x, mods_x[l], norm1_g[l], w_in_p[l], rope)
        ctx_parts = _inproj_call(xc.reshape(1, B * Nc, D), mods_c[l], norm1_g[l], w_in_p[l], None)
        qc, kvc, mc, gc, sc = (t.reshape(B, Nc, -1) for t in ctx_parts)
        oa = _attn_call(attn_sink[l], q, kv, kvc)
        om, omc = _mlstm_call(mc, sc, ml, sl, m_i_bias[l], m_f_bias[l], m_norm_g[l], not last)
        og, ogc = _gla_call(gc, sc, gl, sl, g_wa2[l], g_ba[l], g_norm_g[l], not last)
        x = _outmlp_call(x, oa, om, og, mods_x[l], norm2_g[l], wo[l], w1[l], w2[l], final_g if last else None)
        if not last:
            oac = _attn_call(attn_sink[l], qc, None, kvc)
            flat = lambda t: t.reshape(1, B * Nc, -1)
            xc = _outmlp_call(flat(xc), flat(oac), flat(omc), flat(ogc), mods_c[l], norm2_g[l],
                              wo[l], w1[l], w2[l], None).reshape(B, Nc, D)
    return x
```

```python
import functools

import numpy as np
import jax
import jax.numpy as jnp
from jax import lax
from jax.experimental import pallas as pl
from jax.experimental.pallas import tpu as pltpu

F32 = jnp.float32
BF16 = jnp.bfloat16

HEAD_DIM = 64
GRID_W = 64
ATTN_HEADS = 8
ATTN_KV_HEADS = 2
ATTN_GROUP = ATTN_HEADS // ATTN_KV_HEADS
WINDOW = 128
ATTN_BLOCK = 128
ROPE_BASE = 10000.0
ROPE_AXIS_DIM = HEAD_DIM // 2
M_HEADS = 4
G_HEADS = 4
CHUNK = 64
GLA_GROUP = 8
M_CHUNK = 128
M_GROUP = 2
M_AUG = HEAD_DIM + 8
KT_CHUNKS = 128 // 8
GLA_RANK = 16
GLA_TAU = 16.0
EPS = 1e-6
NEG_BIG = -1e30

LANES = 128
Q_COLS = ATTN_HEADS * HEAD_DIM
KV_COLS = 2 * ATTN_KV_HEADS * HEAD_DIM
MIX_COLS = 4 * M_HEADS * HEAD_DIM
SMALL_COLS = LANES
N_MGATES = 2 * 2 * 2
VMEM_LIMIT = 56 * 1024 * 1024


def _cparams(sem):
    return pltpu.CompilerParams(dimension_semantics=sem, vmem_limit_bytes=VMEM_LIMIT)


def _pick_tile(n, pref):
    t = pref
    while n % t:
        t //= 2
    return t


def _split3(a):
    a1 = a.astype(BF16)
    r1 = a - a1.astype(F32)
    a2 = r1.astype(BF16)
    a3 = (r1 - a2.astype(F32)).astype(BF16)
    return a1, a2, a3


def _dot(a, b):
    return jnp.dot(a, b, preferred_element_type=F32)


def _dot_nt(a, b):
    return lax.dot_general(a, b, (((1,), (1,)), ((), ())), preferred_element_type=F32)


def _dot_hi(a, b):
    a1, a2, _ = _split3(a)
    b1, b2, _ = _split3(b)
    return _dot(a1, b1) + (_dot(a1, b2) + _dot(a2, b1))


def _log_sigmoid(x):
    return jnp.minimum(x, 0.0) - jnp.log1p(jnp.exp(-jnp.abs(x)))


def _ada_kernel(cc_ref, w_ref, b_ref, o_ref):
    cc = cc_ref[...]
    s = cc * jax.nn.sigmoid(cc)
    o_ref[0] = _dot_hi(s, w_ref[0]) + b_ref[0]


def _ada_call(cc, w_ada, b_ada):
    depth, d, six_d = w_ada.shape
    rows = cc.shape[0]
    tn = _pick_tile(six_d, 1024)
    return pl.pallas_call(
        _ada_kernel,
        grid=(depth, six_d // tn),
        in_specs=[pl.BlockSpec((rows, d), lambda l, j: (0, 0)),
                  pl.BlockSpec((1, d, tn), lambda l, j: (l, 0, j)),
                  pl.BlockSpec((1, 1, tn), lambda l, j: (l, 0, j))],
        out_specs=pl.BlockSpec((1, rows, tn), lambda l, j: (l, 0, j)),
        out_shape=jax.ShapeDtypeStruct((depth, rows, six_d), F32),
        compiler_params=_cparams(("arbitrary", "arbitrary")),
        name="ada_mod",
    )(cc, w_ada, b_ada.reshape(depth, 1, six_d))


def _norm_mod(x, g, shift, scale):
    ms = jnp.mean(x * x, axis=-1, keepdims=True)
    h = x * lax.rsqrt(ms + EPS) * g
    return h * (1.0 + scale) + shift


def _inproj_kernel(*refs, use_rope):
    if use_rope:
        x_ref, mod_ref, g_ref, w_ref, cos_ref, sa_ref, sb_ref, q_ref, kv_ref, m_ref, gl_ref, s_ref = refs
    else:
        x_ref, mod_ref, g_ref, w_ref, q_ref, kv_ref, m_ref, gl_ref, s_ref = refs
    mod = mod_ref[0]
    hb = _norm_mod(x_ref[0], g_ref[...], mod[0:1], mod[1:2]).astype(BF16)

    def proj(lo, width):
        return _dot(hb, w_ref[:, lo:lo + width])

    pa = proj(0, Q_COLS + KV_COLS)
    if use_rope:
        cos, sa, sb = cos_ref[...], sa_ref[...], sb_ref[...]
        segs = []
        for j in range((Q_COLS + KV_COLS // 2) // LANES):
            seg = pa[:, j * LANES:(j + 1) * LANES]
            segs.append(seg * cos + pltpu.roll(seg, ROPE_AXIS_DIM // 2, 1) * sa
                        + pltpu.roll(seg, LANES - ROPE_AXIS_DIM // 2, 1) * sb)
        qk = jnp.concatenate(segs, axis=1)
    else:
        qk = pa[:, :Q_COLS + KV_COLS // 2]
    q_ref[0] = qk[:, :Q_COLS] * (HEAD_DIM ** -0.5)
    kv_ref[0, :, :KV_COLS // 2] = qk[:, Q_COLS:]
    kv_ref[0, :, KV_COLS // 2:] = pa[:, Q_COLS + KV_COLS // 2:]
    lo = Q_COLS + KV_COLS
    m_ref[0] = proj(lo, MIX_COLS)
    gl_ref[0] = proj(lo + MIX_COLS, MIX_COLS)
    s_ref[0] = proj(lo + 2 * MIX_COLS, SMALL_COLS)


def _inproj_call(x, mod, g, w, rope):
    bx, n, d = x.shape
    tm = _pick_tile(n, 512)
    cols = w.shape[1]
    in_specs = [pl.BlockSpec((1, tm, d), lambda b, i: (b, i, 0)),
                pl.BlockSpec((1, 6, d), lambda b, i: (b, 0, 0)),
                pl.BlockSpec((1, d), lambda b, i: (0, 0)),
                pl.BlockSpec((d, cols), lambda b, i: (0, 0))]
    args = [x, mod, g.reshape(1, d), w]
    if rope is not None:
        in_specs += [pl.BlockSpec((tm, LANES), lambda b, i: (i, 0))] * 3
        args += list(rope)
    widths = (Q_COLS, KV_COLS, MIX_COLS, MIX_COLS, SMALL_COLS)
    return pl.pallas_call(
        functools.partial(_inproj_kernel, use_rope=rope is not None),
        grid=(bx, n // tm),
        in_specs=in_specs,
        out_specs=[pl.BlockSpec((1, tm, wd), lambda b, i: (b, i, 0)) for wd in widths],
        out_shape=[jax.ShapeDtypeStruct((bx, n, wd), F32) for wd in widths],
        compiler_params=_cparams(("parallel", "parallel")),
        name="inproj_rope" if rope is not None else "inproj",
    )(*args)


def _attn_kernel(*refs, local, tq):
    if local:
        sink_ref, q_ref, kv_ref, kvc_ref, o_ref, km_ref, vt_ref, kmc_ref, vtc_ref = refs
    else:
        sink_ref, q_ref, kvc_ref, o_ref, kmc_ref, vtc_ref = refs
    npairs = ATTN_GROUP
    cols = npairs * tq
    lane = lax.broadcasted_iota(jnp.int32, (1, LANES), 1)
    head_lanes = (lane < HEAD_DIM, lane >= HEAD_DIM)
    row8 = lax.broadcasted_iota(jnp.int32, (8, LANES), 0)
    ones8 = jnp.where(row8 == 0, 1.0, 0.0).astype(F32)

    def prepare(src_ref, km, vt):
        def body(i, carry):
            rows = pl.ds(pl.multiple_of(i * LANES, LANES), LANES)
            k128 = src_ref[0, rows, 0:LANES]
            v_t = src_ref[0, rows, LANES:2 * LANES].T
            for kvh in range(ATTN_KV_HEADS):
                km[kvh, rows, :] = jnp.where(head_lanes[kvh], k128, 0.0).astype(BF16)
                vt[kvh, 0:HEAD_DIM, rows] = v_t[kvh * HEAD_DIM:(kvh + 1) * HEAD_DIM]
                vt[kvh, HEAD_DIM:M_AUG, rows] = ones8
            return carry

        lax.fori_loop(0, src_ref.shape[1] // LANES, body, 0)

    @pl.when(pl.program_id(1) == 0)
    def _():
        prepare(kvc_ref, kmc_ref, vtc_ref)
        if local:
            prepare(kv_ref, km_ref, vt_ref)

    q = q_ref[0].astype(BF16)
    qall = jnp.concatenate([q[:, p * LANES:(p + 1) * LANES] for p in range(npairs)], axis=0)
    if local:
        n = kv_ref.shape[1]
        span = 3 * ATTN_BLOCK
        j = pl.program_id(1)
        start = pl.multiple_of(jnp.clip((j - 1) * ATTN_BLOCK, 0, n - span), ATTN_BLOCK)
        kpos = start + lax.broadcasted_iota(jnp.int32, (span, cols), 0)
        qpos = j * ATTN_BLOCK + (lax.broadcasted_iota(jnp.int32, (span, cols), 1) & (tq - 1))
        valid = jnp.abs(qpos - kpos) <= WINDOW
    s_ctx, s_loc, sink, m = {}, {}, {}, {}
    for kvh in range(ATTN_KV_HEADS):
        s_ctx[kvh] = _dot_nt(kmc_ref[kvh], qall)
        if local:
            s_loc[kvh] = jnp.where(valid, _dot_nt(km_ref[kvh, pl.ds(start, span), :], qall), NEG_BIG)
        sink[kvh] = jnp.concatenate([jnp.full((1, tq), sink_ref[kvh * ATTN_GROUP + p], F32)
                                     for p in range(npairs)], axis=1)
    for kvh in range(ATTN_KV_HEADS):
        m[kvh] = jnp.maximum(jnp.max(s_ctx[kvh], axis=0, keepdims=True), sink[kvh])
        if local:
            m[kvh] = jnp.maximum(m[kvh], jnp.max(s_loc[kvh], axis=0, keepdims=True))
    acc = {}
    for kvh in range(ATTN_KV_HEADS):
        acc[kvh] = _dot(vtc_ref[kvh].astype(BF16), jnp.exp(s_ctx[kvh] - m[kvh]).astype(BF16))
        if local:
            acc[kvh] = acc[kvh] + _dot(vt_ref[kvh, :, pl.ds(start, span)].astype(BF16),
                                       jnp.exp(s_loc[kvh] - m[kvh]).astype(BF16))
    o_t = [acc[kvh][0:HEAD_DIM] / (acc[kvh][HEAD_DIM:HEAD_DIM + 1] + jnp.exp(sink[kvh] - m[kvh]))
           for kvh in range(ATTN_KV_HEADS)]
    for p in range(npairs):
        for c in range(tq // LANES):
            sl = slice(p * tq + c * LANES, p * tq + (c + 1) * LANES)
            tile = jnp.concatenate([o_t[kvh][:, sl] for kvh in range(ATTN_KV_HEADS)], axis=0)
            o_ref[0, c * LANES:(c + 1) * LANES, p * LANES:(p + 1) * LANES] = tile.T


def _attn_call(sink, q, kv, kvc):
    b, nq, _ = q.shape
    nc = kvc.shape[1]
    local = kv is not None
    tq = ATTN_BLOCK if local else nq
    assert tq % LANES == 0 and nc % LANES == 0 and tq & (tq - 1) == 0
    in_specs = [pl.BlockSpec(memory_space=pltpu.SMEM),
                pl.BlockSpec((1, tq, Q_COLS), lambda bi, j: (bi, j, 0))]
    args = [sink, q]
    scratch = []
    if local:
        in_specs.append(pl.BlockSpec((1, nq, KV_COLS), lambda bi, j: (bi, 0, 0)))
        args.append(kv)
        scratch += [pltpu.VMEM((ATTN_KV_HEADS, nq, LANES), BF16), pltpu.VMEM((ATTN_KV_HEADS, M_AUG, nq), F32)]
    in_specs.append(pl.BlockSpec((1, nc, KV_COLS), lambda bi, j: (bi, 0, 0)))
    args.append(kvc)
    scratch += [pltpu.VMEM((ATTN_KV_HEADS, nc, LANES), BF16), pltpu.VMEM((ATTN_KV_HEADS, M_AUG, nc), F32)]
    return pl.pallas_call(
        functools.partial(_attn_kernel, local=local, tq=tq),
        grid=(b, nq // tq),
        in_specs=in_specs,
        out_specs=pl.BlockSpec((1, tq, Q_COLS), lambda bi, j: (bi, j, 0)),
        out_shape=jax.ShapeDtypeStruct((b, nq, Q_COLS), F32),
        scratch_shapes=scratch,
        compiler_params=_cparams(("arbitrary", "arbitrary")),
        name="attn_window" if local else "attn_ctx",
    )(*args)


def _tri(n, rev):
    ri = lax.broadcasted_iota(jnp.int32, (n, n), 0)
    ci = lax.broadcasted_iota(jnp.int32, (n, n), 1)
    return ci >= ri if rev else ci <= ri


def _seq_specs(n):
    return [pl.BlockSpec((1, n, LANES), lambda b, hp, off=off: (b, 0, off + hp)) for off in (0, 2, 4, 6)]


def _group(nchunks, pref):
    g = min(pref, nchunks)
    assert nchunks % g == 0
    return g


def _mlstm_kernel(*refs, with_ctx_out):
    (qc_ref, kc_ref, vc_ref, oc_gate_ref, grc_ref,
     ql_ref, kl_ref, vl_ref, ol_gate_ref, grl_ref, bias_ref, g_ref) = refs[:12]
    if with_ctx_out:
        o_lat_ref, o_ctx_ref = refs[12:14]
        scratch = refs[14:]
    else:
        o_lat_ref, o_ctx_ref = refs[12], None
        scratch = refs[13:]
    w_ref, b_ref, ck_ref, tot_ref, mc_ref, mp_ref, kt_ref, cc_ref, vt_ref = scratch
    L = M_CHUNK
    ncc, ncl = qc_ref.shape[1] // L, ql_ref.shape[1] // L
    kt_c = -(-ncc // KT_CHUNKS)

    mask_t = (_tri(L, True), _tri(L, False))
    tri_f, tri_b = (mk.astype(BF16) for mk in mask_t)
    ones_m = jnp.ones((L, L), BF16)
    row8 = lax.broadcasted_iota(jnp.int32, (8, L), 0)
    rev8 = (row8 & 2) != 0
    ones8 = jnp.where(row8 == 0, 1.0, 0.0).astype(F32)

    def prep(gr_ref, v_ref, base, kt_base, v_base):
        nch = gr_ref.shape[3]
        li = (gr_ref[0, 0, 0] + bias_ref[0, 0]).reshape(nch * 8, L)
        lf = _log_sigmoid(gr_ref[0, 0, 1] + bias_ref[0, 1]).reshape(nch * 8, L)
        rev = (lax.broadcasted_iota(jnp.int32, li.shape, 0) & 2) != 0
        lane = lax.broadcasted_iota(jnp.int32, li.shape, 1)
        l1, l2, l3 = _split3(lf)
        scan = lambda m: _dot(l1, m) + (_dot(l2, m) + _dot(l3, m))
        b = jnp.where(rev, scan(tri_b), scan(tri_f))
        tot = scan(ones_m)
        g = tot - b + li
        mc = jnp.max(g, axis=1, keepdims=True)
        kap = li - b
        ckf, ckb = kap, kap
        s = 1
        while s < L:
            ckf = jnp.maximum(ckf, jnp.where(lane >= s, pltpu.roll(ckf, s, 1), NEG_BIG))
            ckb = jnp.maximum(ckb, jnp.where(lane < L - s, pltpu.roll(ckb, L - s, 1), NEG_BIG))
            s *= 2
        sl = pl.ds(base, nch)
        w_ref[sl] = jnp.exp(g - mc).reshape(nch, 8, L)
        b_ref[sl] = b.reshape(nch, 8, L)
        ck_ref[sl] = jnp.where(rev, ckb, ckf).reshape(nch, 8, L)
        tot_ref[sl] = tot.reshape(nch, 8, L)
        mc_ref[sl] = jnp.broadcast_to(mc, (nch * 8, L)).reshape(nch, 8, L)
        for gi in range(-(-nch // KT_CHUNKS)):
            blk = kap[gi * LANES:(gi + 1) * LANES]
            if blk.shape[0] < LANES:
                blk = jnp.concatenate([blk, jnp.zeros((LANES - blk.shape[0], L), F32)], axis=0)
            kt_ref[kt_base + gi] = blk.T

        def vt_body(i, carry):
            rows = pl.ds(pl.multiple_of(i * L, L), L)
            vt_ref[:, pl.ds(pl.multiple_of(v_base + i * L, L), L)] = v_ref[0, rows, :].T
            return carry

        lax.fori_loop(0, nch, vt_body, 0)

    def vaug(vt2, hh):
        return jnp.concatenate([vt2[hh * HEAD_DIM:(hh + 1) * HEAD_DIM], ones8], axis=0)

    def summarize(k_ref, base, v_base):
        nch = k_ref.shape[1] // L
        grp = _group(nch, M_GROUP)

        def body(t, carry):
            lhs, rhs = {}, {}
            for gi in range(grp):
                cl = t * grp + gi
                k2 = (k_ref[0, pl.ds(pl.multiple_of(cl * L, L), L), :] * (HEAD_DIM ** -0.5)).astype(BF16)
                vt2 = vt_ref[:, pl.ds(pl.multiple_of(v_base + cl * L, L), L)]
                w8 = w_ref[base + cl]
                for hh in range(2):
                    va = vaug(vt2, hh)
                    lhs[gi, hh] = jnp.concatenate([va * w8[d * 2 + hh:d * 2 + hh + 1] for d in range(2)],
                                                  axis=0).astype(BF16)
                    rhs[gi, hh] = k2[:, hh * HEAD_DIM:(hh + 1) * HEAD_DIM]
            cc = {key: _dot(lhs[key], rhs[key]) for key in lhs}
            for (gi, hh), val in cc.items():
                for d in range(2):
                    cc_ref[base + t * grp + gi, d * 2 + hh] = val[d * M_AUG:(d + 1) * M_AUG]
            return carry

        lax.fori_loop(0, nch // grp, body, 0)

    def scan_states(base, nch, state):
        def body(t, st):
            m8, cs = st[0], list(st[1:])
            cf, cb = base + t, base + nch - 1 - t
            tot8 = jnp.where(rev8, tot_ref[cb], tot_ref[cf])
            mc8 = jnp.where(rev8, mc_ref[cb], mc_ref[cf])
            m_new = jnp.maximum(tot8 + m8, mc8)
            s_old = jnp.exp(tot8 + m8 - m_new)
            s_new = jnp.exp(mc8 - m_new)
            mp_ref[cf, 0:2, :] = m8[0:2]
            mp_ref[cb, 2:4, :] = m8[2:4]
            for j in range(4):
                c = cf if j < 2 else cb
                summary = cc_ref[c, j]
                cc_ref[c, j] = cs[j]
                cs[j] = s_old[j:j + 1, 0:HEAD_DIM] * cs[j] + s_new[j:j + 1, 0:HEAD_DIM] * summary
            return (m_new, *cs)

        return lax.fori_loop(0, nch, body, state)

    def emit(q_ref, k_ref, gate_ref, o_ref, base, kt_base, v_base):
        nch = q_ref.shape[1] // L
        grp = _group(nch, M_GROUP)

        def body(t, carry):
            units = [(gi, hh) for gi in range(grp) for hh in range(2)]
            rows, vas, rho, alpha, floor, kcol, lhs, rhs = {}, {}, {}, {}, {}, {}, {}, {}
            for gi in range(grp):
                cl = t * grp + gi
                c = base + cl
                rows[gi] = pl.ds(pl.multiple_of(cl * L, L), L)
                q2 = q_ref[0, rows[gi], :].astype(BF16)
                k2 = k_ref[0, rows[gi], :] * (HEAD_DIM ** -0.5)
                vt2 = vt_ref[:, pl.ds(pl.multiple_of(v_base + cl * L, L), L)]
                mp8 = mp_ref[c]
                rho[gi] = -jnp.maximum(mp8, ck_ref[c])
                alpha[gi] = jnp.exp(mp8 + rho[gi])
                floor[gi] = jnp.exp(rho[gi] - b_ref[c])
                kt = kt_ref[kt_base + (cl >> 4)]
                kcol[gi] = pltpu.roll(kt, (LANES - (cl & (KT_CHUNKS - 1)) * 8) & (LANES - 1), 1)
                for hh in range(2):
                    ln = slice(hh * HEAD_DIM, (hh + 1) * HEAD_DIM)
                    vas[gi, hh] = vaug(vt2, hh).astype(BF16)
                    lhs[gi, hh] = jnp.concatenate([k2[:, ln], cc_ref[c, hh], cc_ref[c, 2 + hh]], axis=0).astype(BF16)
                    rhs[gi, hh] = q2[:, ln]
            prod = {u: _dot_nt(lhs[u], rhs[u]) for u in units}
            pt = {}
            for gi, hh in units:
                st = prod[gi, hh][0:L]
                for d in range(2):
                    j = d * 2 + hh
                    e = kcol[gi][:, j:j + 1] + rho[gi][j:j + 1, :]
                    pt[gi, hh, d] = (jnp.where(mask_t[d], jnp.exp(e), 0.0) * st).astype(BF16)
            pv = {key: _dot(vas[key[0], key[1]], val) for key, val in pt.items()}
            for gi in range(grp):
                halves = []
                for hh in range(2):
                    hsum = None
                    for d in range(2):
                        j = d * 2 + hh
                        lo = L + d * M_AUG
                        num = alpha[gi][j:j + 1, :] * prod[gi, hh][lo:lo + M_AUG] + pv[gi, hh, d]
                        h = num[0:HEAD_DIM] / jnp.maximum(jnp.abs(num[HEAD_DIM:HEAD_DIM + 1]), floor[gi][j:j + 1, :])
                        hsum = h if hsum is None else hsum + h
                    ms = jnp.sum(hsum * hsum, axis=0, keepdims=True) * (1.0 / HEAD_DIM)
                    halves.append(hsum * lax.rsqrt(ms + EPS))
                y = jnp.concatenate(halves, axis=0).T
                o_ref[0, rows[gi], :] = y * g_ref[...] * jax.nn.sigmoid(gate_ref[0, rows[gi], :])
            return carry

        lax.fori_loop(0, nch // grp, body, 0)

    n_ctx = qc_ref.shape[1]
    prep(grc_ref, vc_ref, 0, 0, 0)
    prep(grl_ref, vl_ref, ncc, kt_c, n_ctx)
    summarize(kc_ref, 0, 0)
    summarize(kl_ref, ncc, n_ctx)
    zero = (jnp.zeros((8, L), F32),) + (jnp.zeros((M_AUG, HEAD_DIM), F32),) * 4
    state = scan_states(0, ncc, zero)
    scan_states(ncc, ncl, state)
    if with_ctx_out:
        emit(qc_ref, kc_ref, oc_gate_ref, o_ctx_ref, 0, 0, 0)
    emit(ql_ref, kl_ref, ol_gate_ref, o_lat_ref, ncc, kt_c, n_ctx)


def _mlstm_gates(s):
    b, n, _ = s.shape
    g = s[:, :, :2 * N_MGATES].reshape(b, n, 2, 2, 2, 2)
    g = g.transpose(0, 4, 2, 3, 5, 1).reshape(b, 2, 2, 4, n // M_CHUNK, M_CHUNK).transpose(0, 1, 2, 4, 3, 5)
    return jnp.concatenate([g, g], axis=4)


def _mlstm_call(mc, sc, ml, sl, i_bias, f_bias, g, with_ctx_out):
    b, n, _ = ml.shape
    nc = mc.shape[1]
    assert n % M_CHUNK == 0 and nc % M_CHUNK == 0
    ncc, ncl = nc // M_CHUNK, n // M_CHUNK
    bias = jnp.stack([i_bias, f_bias]).astype(F32).reshape(2, 2, 2, 2).transpose(2, 0, 1, 3).reshape(2, 2, 4)
    bias = jnp.concatenate([bias, bias], axis=2)[..., None]
    gr_spec = lambda nch: pl.BlockSpec((1, 1, 2, nch, 8, M_CHUNK), lambda bi, hp: (bi, hp, 0, 0, 0, 0))
    in_specs = (_seq_specs(nc) + [gr_spec(ncc)] + _seq_specs(n) + [gr_spec(ncl)]
                + [pl.BlockSpec((1, 2, 8, 1), lambda bi, hp: (hp, 0, 0, 0)),
                   pl.BlockSpec((1, LANES), lambda bi, hp: (0, hp))])
    out_specs = [pl.BlockSpec((1, n, LANES), lambda bi, hp: (bi, 0, hp))]
    out_shape = [jax.ShapeDtypeStruct((b, n, M_HEADS * HEAD_DIM), F32)]
    if with_ctx_out:
        out_specs.append(pl.BlockSpec((1, nc, LANES), lambda bi, hp: (bi, 0, hp)))
        out_shape.append(jax.ShapeDtypeStruct((b, nc, M_HEADS * HEAD_DIM), F32))
    tot = ncc + ncl
    per_chunk = pltpu.VMEM((tot, 8, M_CHUNK), F32)
    scratch = [per_chunk] * 6 + [
        pltpu.VMEM((-(-ncc // KT_CHUNKS) + -(-ncl // KT_CHUNKS), M_CHUNK, LANES), F32),
        pltpu.VMEM((tot, 4, M_AUG, HEAD_DIM), F32),
        pltpu.VMEM((LANES, nc + n), F32)]
    outs = pl.pallas_call(
        functools.partial(_mlstm_kernel, with_ctx_out=with_ctx_out),
        grid=(b, M_HEADS // 2),
        in_specs=in_specs,
        out_specs=out_specs,
        out_shape=out_shape,
        scratch_shapes=scratch,
        compiler_params=_cparams(("parallel", "arbitrary")),
        name="mlstm",
    )(mc, mc, mc, mc, _mlstm_gates(sc), ml, ml, ml, ml, _mlstm_gates(sl), bias, g.reshape(1, -1))
    return (outs[0], outs[1]) if with_ctx_out else (outs[0], None)


def _head_norm_gate(hsum, g, gate):
    sq = hsum * hsum
    lane = lax.broadcasted_iota(jnp.int32, hsum.shape, 1)
    first = lane < HEAD_DIM
    s0 = jnp.sum(jnp.where(first, sq, 0.0), axis=-1, keepdims=True)
    s1 = jnp.sum(jnp.where(first, 0.0, sq), axis=-1, keepdims=True)
    ms = jnp.where(first, s0, s1) * (1.0 / HEAD_DIM)
    return hsum * lax.rsqrt(ms + EPS) * g * gate


def _seg_scan_sum(x, rev):
    rows = x.shape[0]
    pos = lax.broadcasted_iota(jnp.int32, x.shape, 0) & (CHUNK - 1)
    s = 1
    while s < CHUNK:
        if rev:
            x = x + jnp.where(pos < CHUNK - s, pltpu.roll(x, rows - s, 0), 0.0)
        else:
            x = x + jnp.where(pos >= s, pltpu.roll(x, s, 0), 0.0)
        s *= 2
    return x


def _gla_kernel(*refs, with_ctx_out):
    (qc_ref, kc_ref, vc_ref, oc_gate_ref, sc_ref,
     ql_ref, kl_ref, vl_ref, ol_gate_ref, sl_ref, wa_ref, ba_ref, g_ref) = refs[:13]
    if with_ctx_out:
        o_lat_ref, o_ctx_ref, bc_ref, u_ref, dec_ref = refs[13:]
    else:
        o_lat_ref, bc_ref, u_ref, dec_ref = refs[13:]
        o_ctx_ref = None
    ncc, ncl = qc_ref.shape[1] // CHUNK, ql_ref.shape[1] // CHUNK
    lo = 2 * N_MGATES
    mask_f, mask_b = _tri(CHUNK, False), _tri(CHUNK, True)
    first = lax.broadcasted_iota(jnp.int32, (HEAD_DIM, LANES), 1) < HEAD_DIM
    halves = (slice(0, HEAD_DIM), slice(HEAD_DIM, 2 * HEAD_DIM))
    silu = lambda t: t * jax.nn.sigmoid(t)

    def slab_of(t, grp):
        return pl.multiple_of(t * (grp * CHUNK), grp * CHUNK)

    def summarize(k_ref, v_ref, s_ref, base):
        nch = k_ref.shape[1] // CHUNK
        grp = _group(nch, GLA_GROUP)
        assert grp % 2 == 0

        def body(t, carry):
            r0 = slab_of(t, grp)
            slab = pl.ds(r0, grp * CHUNK)
            k3 = k_ref[0, slab, :].reshape(grp, CHUNK, LANES)
            v2 = v_ref[0, slab, :]
            kws, ends = [], []
            for d in range(2):
                lr = s_ref[0, slab, lo + d * GLA_RANK:lo + (d + 1) * GLA_RANK]
                glog = _log_sigmoid(_dot_hi(lr, wa_ref[d]) + ba_ref[d]) * (1.0 / GLA_TAU)
                bc = _seg_scan_sum(glog, d == 1)
                bc_ref[d, pl.ds(pl.multiple_of(base * CHUNK + r0, CHUNK), grp * CHUNK), :] = bc
                bc3 = bc.reshape(grp, CHUNK, LANES)
                end = bc3[:, 0:1, :] if d == 1 else bc3[:, CHUNK - 1:CHUNK, :]
                kws.append((k3 * jnp.exp(end - bc3)).astype(BF16))
                ends.append(end)
            vts = [v2[p * LANES:(p + 1) * LANES, :].T.astype(BF16) for p in range(grp // 2)]
            us = {}
            for gi in range(grp):
                for hh in range(2):
                    vt = vts[gi // 2][halves[hh], halves[gi % 2]]
                    for d in range(2):
                        us[gi, hh, d] = _dot(vt, kws[d][gi][:, halves[hh]])
            for gi in range(grp):
                c = base + t * grp + gi
                for hh in range(2):
                    u_ref[hh, c] = jnp.concatenate([us[gi, hh, 0], us[gi, hh, 1]], axis=1)
                    end2 = jnp.concatenate([ends[0][gi][:, halves[hh]], ends[1][gi][:, halves[hh]]], axis=1)
                    dec_ref[hh, c] = jnp.broadcast_to(jnp.exp(end2), (8, LANES))
            return carry

        lax.fori_loop(0, nch // grp, body, 0)

    def scan_states(base, nch, state):
        def body(t, st):
            cf, cb = base + t, base + nch - 1 - t
            new = []
            for hh in range(2):
                u = jnp.where(first, u_ref[hh, cf], u_ref[hh, cb])
                dec = jnp.where(first[0:1], dec_ref[hh, cf][0:1], dec_ref[hh, cb][0:1])
                u_ref[hh, cf, :, 0:HEAD_DIM] = st[hh][:, 0:HEAD_DIM]
                u_ref[hh, cb, :, HEAD_DIM:] = st[hh][:, HEAD_DIM:]
                new.append(st[hh] * dec + u)
            return tuple(new)

        return lax.fori_loop(0, nch, body, state)

    def emit(q_ref, k_ref, v_ref, gate_ref, o_ref, base):
        nch = q_ref.shape[1] // CHUNK
        grp = _group(nch, GLA_GROUP)

        def body(t, carry):
            r0 = slab_of(t, grp)
            slab = pl.ds(r0, grp * CHUNK)
            q2 = q_ref[0, slab, :] * (HEAD_DIM ** -0.5)
            k2 = k_ref[0, slab, :]
            vb = v_ref[0, slab, :].astype(BF16)
            qd, kd = [], []
            for d in range(2):
                bc = bc_ref[d, pl.ds(pl.multiple_of(base * CHUNK + r0, CHUNK), grp * CHUNK), :]
                qd.append((q2 * jnp.exp(bc)).astype(BF16))
                kd.append((k2 * jnp.exp(-bc)).astype(BF16))
            units = [(gi, hh) for gi in range(grp) for hh in range(2)]
            rs = lambda gi: slice(gi * CHUNK, (gi + 1) * CHUNK)
            att = {}
            for gi, hh in units:
                a_f = jnp.where(mask_f, _dot_nt(qd[0][rs(gi), halves[hh]], kd[0][rs(gi), halves[hh]]), 0.0)
                a_b = jnp.where(mask_b, _dot_nt(qd[1][rs(gi), halves[hh]], kd[1][rs(gi), halves[hh]]), 0.0)
                att[gi, hh] = (a_f + a_b).astype(BF16)
            outs = []
            for gi in range(grp):
                c = base + t * grp + gi
                parts = []
                for hh in range(2):
                    qcat = jnp.concatenate([qd[0][rs(gi), halves[hh]], qd[1][rs(gi), halves[hh]]], axis=1)
                    parts.append(_dot(att[gi, hh], vb[rs(gi), halves[hh]])
                                 + _dot_nt(qcat, u_ref[hh, c].astype(BF16)))
                outs.append(jnp.concatenate(parts, axis=1))
            o2 = jnp.concatenate(outs, axis=0)
            o_ref[0, slab, :] = _head_norm_gate(o2, g_ref[...], silu(gate_ref[0, slab, :]))
            return carry

        lax.fori_loop(0, nch // grp, body, 0)

    summarize(kc_ref, vc_ref, sc_ref, 0)
    summarize(kl_ref, vl_ref, sl_ref, ncc)
    state = scan_states(0, ncc, (jnp.zeros((HEAD_DIM, LANES), F32),) * 2)
    scan_states(ncc, ncl, state)
    if with_ctx_out:
        emit(qc_ref, kc_ref, vc_ref, oc_gate_ref, o_ctx_ref, 0)
    emit(ql_ref, kl_ref, vl_ref, ol_gate_ref, o_lat_ref, ncc)


def _gla_call(gc, sc, gl, sl, wa2, ba, g, with_ctx_out):
    b, n, _ = gl.shape
    nc = gc.shape[1]
    tot = (nc + n) // CHUNK
    small = lambda nn: pl.BlockSpec((1, nn, SMALL_COLS), lambda bi, hp: (bi, 0, 0))
    in_specs = (_seq_specs(nc) + [small(nc)] + _seq_specs(n) + [small(n)]
                + [pl.BlockSpec((2, GLA_RANK, LANES), lambda bi, hp: (0, 0, hp)),
                   pl.BlockSpec((2, 1, LANES), lambda bi, hp: (0, 0, hp)),
                   pl.BlockSpec((1, LANES), lambda bi, hp: (0, hp))])
    out_specs = [pl.BlockSpec((1, n, LANES), lambda bi, hp: (bi, 0, hp))]
    out_shape = [jax.ShapeDtypeStruct((b, n, G_HEADS * HEAD_DIM), F32)]
    if with_ctx_out:
        out_specs.append(pl.BlockSpec((1, nc, LANES), lambda bi, hp: (bi, 0, hp)))
        out_shape.append(jax.ShapeDtypeStruct((b, nc, G_HEADS * HEAD_DIM), F32))
    scratch = [pltpu.VMEM((2, nc + n, LANES), F32),
               pltpu.VMEM((2, tot, HEAD_DIM, LANES), F32),
               pltpu.VMEM((2, tot, 8, LANES), F32)]
    outs = pl.pallas_call(
        functools.partial(_gla_kernel, with_ctx_out=with_ctx_out),
        grid=(b, G_HEADS // 2),
        in_specs=in_specs,
        out_specs=out_specs,
        out_shape=out_shape,
        scratch_shapes=scratch,
        compiler_params=_cparams(("parallel", "arbitrary")),
        name="gla",
    )(gc, gc, gc, gc, sc, gl, gl, gl, gl, sl, wa2, ba.reshape(2, 1, -1), g.reshape(1, -1))
    return (outs[0], outs[1]) if with_ctx_out else (outs[0], None)


def _outmlp_kernel(*refs, final, ff_tile):
    if final:
        x_ref, oa_ref, om_ref, og_ref, mod_ref, g_ref, wo_ref, w1_ref, w2_ref, fg_ref, y_ref = refs
    else:
        x_ref, oa_ref, om_ref, og_ref, mod_ref, g_ref, wo_ref, w1_ref, w2_ref, y_ref = refs
    mod = mod_ref[0]
    a_w, m_w = oa_ref.shape[2], om_ref.shape[2]
    o = (_dot(oa_ref[0].astype(BF16), wo_ref[0:a_w, :])
         + _dot(om_ref[0].astype(BF16), wo_ref[a_w:a_w + m_w, :])
         + _dot(og_ref[0].astype(BF16), wo_ref[a_w + m_w:, :]))
    x1 = x_ref[0] + mod[2:3] * o
    hb = _norm_mod(x1, g_ref[...], mod[3:4], mod[4:5]).astype(BF16)
    d_ff = w1_ref.shape[1]
    acc = jnp.zeros(x1.shape, F32)
    for j in range(d_ff // ff_tile):
        t = jnp.maximum(_dot(hb, w1_ref[:, j * ff_tile:(j + 1) * ff_tile]), 0.0)
        acc = acc + _dot((t * t).astype(BF16), w2_ref[j * ff_tile:(j + 1) * ff_tile, :])
    x2 = x1 + mod[5:6] * acc
    if final:
        ms = jnp.mean(x2 * x2, axis=-1, keepdims=True)
        x2 = x2 * lax.rsqrt(ms + EPS) * fg_ref[...]
    y_ref[0] = x2


def _outmlp_call(x, oa, om, og, mod, g, wo, w1, w2, final_g):
    bx, n, d = x.shape
    tm = _pick_tile(n, 256)
    d_ff = w1.shape[1]
    row = lambda wd: pl.BlockSpec((1, tm, wd), lambda b, i: (b, i, 0))
    whole = lambda arr: pl.BlockSpec(arr.shape, lambda b, i: (0,) * arr.ndim)
    g2 = g.reshape(1, d)
    in_specs = [row(d), row(oa.shape[2]), row(om.shape[2]), row(og.shape[2]),
                pl.BlockSpec((1, 6, d), lambda b, i: (b, 0, 0)), whole(g2), whole(wo), whole(w1), whole(w2)]
    args = [x, oa, om, og, mod, g2, wo, w1, w2]
    if final_g is not None:
        fg = final_g.reshape(1, d)
        in_specs.append(whole(fg))
        args.append(fg)
    return pl.pallas_call(
        functools.partial(_outmlp_kernel, final=final_g is not None, ff_tile=_pick_tile(d_ff, 1024)),
        grid=(bx, n // tm),
        in_specs=in_specs,
        out_specs=row(d),
        out_shape=jax.ShapeDtypeStruct((bx, n, d), F32),
        compiler_params=_cparams(("parallel", "parallel")),
        name="outproj_mlp",
    )(*args)


def _rope_tables(n):
    t = jnp.arange(n)
    inv = ROPE_BASE ** (-jnp.arange(0, ROPE_AXIS_DIM, 2, dtype=F32) / ROPE_AXIS_DIM)
    ang_r = (t // GRID_W).astype(F32)[:, None] * inv[None, :]
    ang_c = (t % GRID_W).astype(F32)[:, None] * inv[None, :]
    half = ROPE_AXIS_DIM // 2
    cos = jnp.concatenate([jnp.cos(ang_r)] * 2 + [jnp.cos(ang_c)] * 2, axis=1)
    zero = jnp.zeros((n, half), F32)
    sin_a = jnp.concatenate([zero, jnp.sin(ang_r), zero, jnp.sin(ang_c)], axis=1)
    sin_b = jnp.concatenate([-jnp.sin(ang_r), zero, -jnp.sin(ang_c), zero], axis=1)
    return tuple(jnp.tile(a, (1, LANES // HEAD_DIM)).astype(F32) for a in (cos, sin_a, sin_b))


def _pair_heads(a, axis):
    shape = a.shape
    a = a.reshape(shape[:axis] + (ATTN_KV_HEADS, ATTN_GROUP, HEAD_DIM) + shape[axis + 1:])
    return jnp.swapaxes(a, axis, axis + 1).reshape(shape)


def _permute_w_in(w_in):
    aq, ak, av, mq, mk, mv, mo, mi, mf, gq, gk, gv, gg, glr = jnp.split(
        w_in, np.cumsum([Q_COLS, 128, 128, 256, 256, 256, 256, 8, 8, 256, 256, 256, 256])[:13].tolist(), axis=-1)
    pad = jnp.zeros(w_in.shape[:-1] + (SMALL_COLS - 16 - 2 * GLA_RANK,), w_in.dtype)
    return jnp.concatenate([_pair_heads(aq, aq.ndim - 1), ak, av, mq, mk, mv, mo, gq, gk, gv, gg, mi, mf, glr, pad],
                           axis=-1).astype(BF16)


def _permute_w_out(w_out):
    return jnp.concatenate([_pair_heads(w_out[:, :Q_COLS], 1), w_out[:, Q_COLS:]], axis=1).astype(BF16)


def kernel(x, c, ctx, c_ctx, w_ada, b_ada, norm1_g, norm2_g, w_in, attn_sink, m_i_bias, m_f_bias, m_norm_g,
           g_wa2, g_ba, g_norm_g, w_out, w_mlp1, w_mlp2, final_g):
    B, N, D = x.shape
    Nc = ctx.shape[1]
    depth = w_ada.shape[0]
    rope = _rope_tables(N)
    rows = -(-(B + 1) // 8) * 8
    cc = jnp.zeros((rows, D), F32).at[:B].set(c).at[B].set(c_ctx)
    mods = _ada_call(cc, w_ada, b_ada)
    mods_x = mods[:, :B].reshape(depth, B, 6, D)
    mods_c = mods[:, B:B + 1].reshape(depth, 1, 6, D)
    w_in_p = _permute_w_in(w_in)
    wo, w1, w2 = _permute_w_out(w_out), w_mlp1.astype(BF16), w_mlp2.astype(BF16)

    xc = ctx
    for l in range(depth):
        last = l == depth - 1
        q, kv, ml, gl, sl = _inproj_call(x, mods_x[l], norm1_g[l], w_in_p[l], rope)
        ctx_parts = _inproj_call(xc.reshape(1, B * Nc, D), mods_c[l], norm1_g[l], w_in_p[l], None)
        qc, kvc, mc, gc, sc = (t.reshape(B, Nc, -1) for t in ctx_parts)
        oa = _attn_call(attn_sink[l], q, kv, kvc)
        om, omc = _mlstm_call(mc, sc, ml, sl, m_i_bias[l], m_f_bias[l], m_norm_g[l], not last)
        og, ogc = _gla_call(gc, sc, gl, sl, g_wa2[l], g_ba[l], g_norm_g[l], not last)
        x = _outmlp_call(x, oa, om, og, mods_x[l], norm2_g[l], wo[l], w1[l], w2[l], final_g if last else None)
        if not last:
            oac = _attn_call(attn_sink[l], qc, None, kvc)
            flat = lambda t: t.reshape(1, B * Nc, -1)
            xc = _outmlp_call(flat(xc), flat(oac), flat(omc), flat(ogc), mods_c[l], norm2_g[l],
                              wo[l], w1[l], w2[l], None).reshape(B, Nc, D)
    return x
```

```python
import functools

import numpy as np
import jax
import jax.numpy as jnp
from jax import lax
from jax.experimental import pallas as pl
from jax.experimental.pallas import tpu as pltpu

F32 = jnp.float32
BF16 = jnp.bfloat16

HEAD_DIM = 64
GRID_W = 64
ATTN_HEADS = 8
ATTN_KV_HEADS = 2
ATTN_GROUP = ATTN_HEADS // ATTN_KV_HEADS
WINDOW = 128
ATTN_BLOCK = 128
ROPE_BASE = 10000.0
ROPE_AXIS_DIM = HEAD_DIM // 2
M_HEADS = 4
G_HEADS = 4
CHUNK = 64
GLA_GROUP = 16
M_CHUNK = 128
M_GROUP = 8
M_AUG = HEAD_DIM + 8
KT_CHUNKS = 128 // 8
GLA_RANK = 16
GLA_TAU = 16.0
EPS = 1e-6
NEG_BIG = -1e30

LANES = 128
Q_COLS = ATTN_HEADS * HEAD_DIM
KV_COLS = 2 * ATTN_KV_HEADS * HEAD_DIM
MIX_COLS = 4 * M_HEADS * HEAD_DIM
SMALL_COLS = LANES
N_MGATES = 2 * 2 * 2
VMEM_LIMIT = 56 * 1024 * 1024


def _cparams(sem):
    return pltpu.CompilerParams(dimension_semantics=sem, vmem_limit_bytes=VMEM_LIMIT)


def _pick_tile(n, pref):
    t = pref
    while n % t:
        t //= 2
    return t


def _split3(a):
    a1 = a.astype(BF16)
    r1 = a - a1.astype(F32)
    a2 = r1.astype(BF16)
    a3 = (r1 - a2.astype(F32)).astype(BF16)
    return a1, a2, a3


def _dot(a, b):
    return jnp.dot(a, b, preferred_element_type=F32)


def _dot_nt(a, b):
    return lax.dot_general(a, b, (((1,), (1,)), ((), ())), preferred_element_type=F32)


def _dot_hi(a, b):
    a1, a2, _ = _split3(a)
    b1, b2, _ = _split3(b)
    return _dot(a1, b1) + (_dot(a1, b2) + _dot(a2, b1))


def _log_sigmoid(x):
    return jnp.minimum(x, 0.0) - jnp.log1p(jnp.exp(-jnp.abs(x)))


def _ada_kernel(cc_ref, w_ref, b_ref, o_ref):
    cc = cc_ref[...]
    s = cc * jax.nn.sigmoid(cc)
    o_ref[0] = _dot_hi(s, w_ref[0]) + b_ref[0]


def _ada_call(cc, w_ada, b_ada):
    depth, d, six_d = w_ada.shape
    rows = cc.shape[0]
    tn = _pick_tile(six_d, 1024)
    return pl.pallas_call(
        _ada_kernel,
        grid=(depth, six_d // tn),
        in_specs=[pl.BlockSpec((rows, d), lambda l, j: (0, 0)),
                  pl.BlockSpec((1, d, tn), lambda l, j: (l, 0, j)),
                  pl.BlockSpec((1, 1, tn), lambda l, j: (l, 0, j))],
        out_specs=pl.BlockSpec((1, rows, tn), lambda l, j: (l, 0, j)),
        out_shape=jax.ShapeDtypeStruct((depth, rows, six_d), F32),
        compiler_params=_cparams(("arbitrary", "arbitrary")),
        name="ada_mod",
    )(cc, w_ada, b_ada.reshape(depth, 1, six_d))


def _norm_mod(x, g, shift, scale):
    ms = jnp.mean(x * x, axis=-1, keepdims=True)
    h = x * lax.rsqrt(ms + EPS) * g
    return h * (1.0 + scale) + shift


def _inproj_kernel(*refs, use_rope):
    if use_rope:
        x_ref, mod_ref, g_ref, w_ref, cos_ref, sa_ref, sb_ref, q_ref, kv_ref, m_ref, gl_ref, s_ref = refs
    else:
        x_ref, mod_ref, g_ref, w_ref, q_ref, kv_ref, m_ref, gl_ref, s_ref = refs
    mod = mod_ref[0]
    hb = _norm_mod(x_ref[0], g_ref[...], mod[0:1], mod[1:2]).astype(BF16)

    def proj(lo, width):
        return _dot(hb, w_ref[:, lo:lo + width])

    pa = proj(0, Q_COLS + KV_COLS)
    if use_rope:
        cos, sa, sb = cos_ref[...], sa_ref[...], sb_ref[...]
        segs = []
        for j in range((Q_COLS + KV_COLS // 2) // LANES):
            seg = pa[:, j * LANES:(j + 1) * LANES]
            segs.append(seg * cos + pltpu.roll(seg, ROPE_AXIS_DIM // 2, 1) * sa
                        + pltpu.roll(seg, LANES - ROPE_AXIS_DIM // 2, 1) * sb)
        qk = jnp.concatenate(segs, axis=1)
    else:
        qk = pa[:, :Q_COLS + KV_COLS // 2]
    q_ref[0] = qk[:, :Q_COLS] * (HEAD_DIM ** -0.5)
    kv_ref[0, :, :KV_COLS // 2] = qk[:, Q_COLS:]
    kv_ref[0, :, KV_COLS // 2:] = pa[:, Q_COLS + KV_COLS // 2:]
    lo = Q_COLS + KV_COLS
    m_ref[0] = proj(lo, MIX_COLS)
    gl_ref[0] = proj(lo + MIX_COLS, MIX_COLS)
    s_ref[0] = proj(lo + 2 * MIX_COLS, SMALL_COLS)


def _inproj_call(x, mod, g, w, rope):
    bx, n, d = x.shape
    tm = _pick_tile(n, 512)
    cols = w.shape[1]
    in_specs = [pl.BlockSpec((1, tm, d), lambda b, i: (b, i, 0)),
                pl.BlockSpec((1, 6, d), lambda b, i: (b, 0, 0)),
                pl.BlockSpec((1, d), lambda b, i: (0, 0)),
                pl.BlockSpec((d, cols), lambda b, i: (0, 0))]
    args = [x, mod, g.reshape(1, d), w]
    if rope is not None:
        in_specs += [pl.BlockSpec((tm, LANES), lambda b, i: (i, 0))] * 3
        args += list(rope)
    widths = (Q_COLS, KV_COLS, MIX_COLS, MIX_COLS, SMALL_COLS)
    return pl.pallas_call(
        functools.partial(_inproj_kernel, use_rope=rope is not None),
        grid=(bx, n // tm),
        in_specs=in_specs,
        out_specs=[pl.BlockSpec((1, tm, wd), lambda b, i: (b, i, 0)) for wd in widths],
        out_shape=[jax.ShapeDtypeStruct((bx, n, wd), F32) for wd in widths],
        compiler_params=_cparams(("parallel", "parallel")),
        name="inproj_rope" if rope is not None else "inproj",
    )(*args)


def _attn_kernel(*refs, local, tq):
    if local:
        sink_ref, q_ref, kv_ref, kvc_ref, o_ref, km_ref, vt_ref, kmc_ref, vtc_ref = refs
    else:
        sink_ref, q_ref, kvc_ref, o_ref, kmc_ref, vtc_ref = refs
    npairs = ATTN_GROUP
    cols = npairs * tq
    lane = lax.broadcasted_iota(jnp.int32, (1, LANES), 1)
    head_lanes = (lane < HEAD_DIM, lane >= HEAD_DIM)
    row8 = lax.broadcasted_iota(jnp.int32, (8, LANES), 0)
    ones8 = jnp.where(row8 == 0, 1.0, 0.0).astype(F32)

    def prepare(src_ref, km, vt):
        def body(i, carry):
            rows = pl.ds(pl.multiple_of(i * LANES, LANES), LANES)
            k128 = src_ref[0, rows, 0:LANES]
            v_t = src_ref[0, rows, LANES:2 * LANES].T
            for kvh in range(ATTN_KV_HEADS):
                km[kvh, rows, :] = jnp.where(head_lanes[kvh], k128, 0.0).astype(BF16)
                vt[kvh, 0:HEAD_DIM, rows] = v_t[kvh * HEAD_DIM:(kvh + 1) * HEAD_DIM]
                vt[kvh, HEAD_DIM:M_AUG, rows] = ones8
            return carry

        lax.fori_loop(0, src_ref.shape[1] // LANES, body, 0)

    @pl.when(pl.program_id(1) == 0)
    def _():
        prepare(kvc_ref, kmc_ref, vtc_ref)
        if local:
            prepare(kv_ref, km_ref, vt_ref)

    q = q_ref[0].astype(BF16)
    qall = jnp.concatenate([q[:, p * LANES:(p + 1) * LANES] for p in range(npairs)], axis=0)
    if local:
        n = kv_ref.shape[1]
        span = 3 * ATTN_BLOCK
        j = pl.program_id(1)
        start = pl.multiple_of(jnp.clip((j - 1) * ATTN_BLOCK, 0, n - span), ATTN_BLOCK)
        kpos = start + lax.broadcasted_iota(jnp.int32, (span, cols), 0)
        qpos = j * ATTN_BLOCK + (lax.broadcasted_iota(jnp.int32, (span, cols), 1) & (tq - 1))
        valid = jnp.abs(qpos - kpos) <= WINDOW
    s_ctx, s_loc, sink, m = {}, {}, {}, {}
    for kvh in range(ATTN_KV_HEADS):
        s_ctx[kvh] = _dot_nt(kmc_ref[kvh], qall)
        if local:
            s_loc[kvh] = jnp.where(valid, _dot_nt(km_ref[kvh, pl.ds(start, span), :], qall), NEG_BIG)
        sink[kvh] = jnp.concatenate([jnp.full((1, tq), sink_ref[kvh * ATTN_GROUP + p], F32)
                                     for p in range(npairs)], axis=1)
    for kvh in range(ATTN_KV_HEADS):
        m[kvh] = jnp.maximum(jnp.max(s_ctx[kvh], axis=0, keepdims=True), sink[kvh])
        if local:
            m[kvh] = jnp.maximum(m[kvh], jnp.max(s_loc[kvh], axis=0, keepdims=True))
    acc = {}
    for kvh in range(ATTN_KV_HEADS):
        acc[kvh] = _dot(vtc_ref[kvh].astype(BF16), jnp.exp(s_ctx[kvh] - m[kvh]).astype(BF16))
        if local:
            acc[kvh] = acc[kvh] + _dot(vt_ref[kvh, :, pl.ds(start, span)].astype(BF16),
                                       jnp.exp(s_loc[kvh] - m[kvh]).astype(BF16))
    o_t = [acc[kvh][0:HEAD_DIM] / (acc[kvh][HEAD_DIM:HEAD_DIM + 1] + jnp.exp(sink[kvh] - m[kvh]))
           for kvh in range(ATTN_KV_HEADS)]
    for p in range(npairs):
        for c in range(tq // LANES):
            sl = slice(p * tq + c * LANES, p * tq + (c + 1) * LANES)
            tile = jnp.concatenate([o_t[kvh][:, sl] for kvh in range(ATTN_KV_HEADS)], axis=0)
            o_ref[0, c * LANES:(c + 1) * LANES, p * LANES:(p + 1) * LANES] = tile.T


def _attn_call(sink, q, kv, kvc):
    b, nq, _ = q.shape
    nc = kvc.shape[1]
    local = kv is not None
    tq = ATTN_BLOCK if local else nq
    assert tq % LANES == 0 and nc % LANES == 0 and tq & (tq - 1) == 0
    in_specs = [pl.BlockSpec(memory_space=pltpu.SMEM),
                pl.BlockSpec((1, tq, Q_COLS), lambda bi, j: (bi, j, 0))]
    args = [sink, q]
    scratch = []
    if local:
        in_specs.append(pl.BlockSpec((1, nq, KV_COLS), lambda bi, j: (bi, 0, 0)))
        args.append(kv)
        scratch += [pltpu.VMEM((ATTN_KV_HEADS, nq, LANES), BF16), pltpu.VMEM((ATTN_KV_HEADS, M_AUG, nq), F32)]
    in_specs.append(pl.BlockSpec((1, nc, KV_COLS), lambda bi, j: (bi, 0, 0)))
    args.append(kvc)
    scratch += [pltpu.VMEM((ATTN_KV_HEADS, nc, LANES), BF16), pltpu.VMEM((ATTN_KV_HEADS, M_AUG, nc), F32)]
    return pl.pallas_call(
        functools.partial(_attn_kernel, local=local, tq=tq),
        grid=(b, nq // tq),
        in_specs=in_specs,
        out_specs=pl.BlockSpec((1, tq, Q_COLS), lambda bi, j: (bi, j, 0)),
        out_shape=jax.ShapeDtypeStruct((b, nq, Q_COLS), F32),
        scratch_shapes=scratch,
        compiler_params=_cparams(("arbitrary", "arbitrary")),
        name="attn_window" if local else "attn_ctx",
    )(*args)


def _tri(n, rev):
    ri = lax.broadcasted_iota(jnp.int32, (n, n), 0)
    ci = lax.broadcasted_iota(jnp.int32, (n, n), 1)
    return ci >= ri if rev else ci <= ri


def _seq_specs(n):
    return [pl.BlockSpec((1, n, LANES), lambda b, hp, off=off: (b, 0, off + hp)) for off in (0, 2, 4, 6)]


def _group(nchunks, pref):
    g = min(pref, nchunks)
    assert nchunks % g == 0
    return g


def _mlstm_kernel(*refs, with_ctx_out):
    (qc_ref, kc_ref, vc_ref, oc_gate_ref, grc_ref,
     ql_ref, kl_ref, vl_ref, ol_gate_ref, grl_ref, bias_ref, g_ref) = refs[:12]
    if with_ctx_out:
        o_lat_ref, o_ctx_ref = refs[12:14]
        scratch = refs[14:]
    else:
        o_lat_ref, o_ctx_ref = refs[12], None
        scratch = refs[13:]
    w_ref, b_ref, ck_ref, tot_ref, mc_ref, mp_ref, kt_ref, cc_ref, vt_ref = scratch
    L = M_CHUNK
    ncc, ncl = qc_ref.shape[1] // L, ql_ref.shape[1] // L
    kt_c = -(-ncc // KT_CHUNKS)

    mask_t = (_tri(L, True), _tri(L, False))
    tri_f, tri_b = (mk.astype(BF16) for mk in mask_t)
    ones_m = jnp.ones((L, L), BF16)
    row8 = lax.broadcasted_iota(jnp.int32, (8, L), 0)
    rev8 = (row8 & 2) != 0
    ones8 = jnp.where(row8 == 0, 1.0, 0.0).astype(F32)

    def prep(gr_ref, v_ref, base, kt_base, v_base):
        nch = gr_ref.shape[3]
        li = (gr_ref[0, 0, 0] + bias_ref[0, 0]).reshape(nch * 8, L)
        lf = _log_sigmoid(gr_ref[0, 0, 1] + bias_ref[0, 1]).reshape(nch * 8, L)
        rev = (lax.broadcasted_iota(jnp.int32, li.shape, 0) & 2) != 0
        lane = lax.broadcasted_iota(jnp.int32, li.shape, 1)
        l1, l2, l3 = _split3(lf)
        scan = lambda m: _dot(l1, m) + (_dot(l2, m) + _dot(l3, m))
        b = jnp.where(rev, scan(tri_b), scan(tri_f))
        tot = scan(ones_m)
        g = tot - b + li
        mc = jnp.max(g, axis=1, keepdims=True)
        kap = li - b
        ckf, ckb = kap, kap
        s = 1
        while s < L:
            ckf = jnp.maximum(ckf, jnp.where(lane >= s, pltpu.roll(ckf, s, 1), NEG_BIG))
            ckb = jnp.maximum(ckb, jnp.where(lane < L - s, pltpu.roll(ckb, L - s, 1), NEG_BIG))
            s *= 2
        sl = pl.ds(base, nch)
        w_ref[sl] = jnp.exp(g - mc).reshape(nch, 8, L)
        b_ref[sl] = b.reshape(nch, 8, L)
        ck_ref[sl] = jnp.where(rev, ckb, ckf).reshape(nch, 8, L)
        tot_ref[sl] = tot.reshape(nch, 8, L)
        mc_ref[sl] = jnp.broadcast_to(mc, (nch * 8, L)).reshape(nch, 8, L)
        for gi in range(-(-nch // KT_CHUNKS)):
            blk = kap[gi * LANES:(gi + 1) * LANES]
            if blk.shape[0] < LANES:
                blk = jnp.concatenate([blk, jnp.zeros((LANES - blk.shape[0], L), F32)], axis=0)
            kt_ref[kt_base + gi] = blk.T

        tiles = _group(nch, 4)

        def vt_body(i, carry):
            blocks = [v_ref[0, pl.ds(pl.multiple_of((i * tiles + u) * L, L), L), :].T for u in range(tiles)]
            for u, blk in enumerate(blocks):
                vt_ref[:, pl.ds(pl.multiple_of(v_base + (i * tiles + u) * L, L), L)] = blk
            return carry

        lax.fori_loop(0, nch // tiles, vt_body, 0)

    def vaug(vt2, hh):
        return jnp.concatenate([vt2[hh * HEAD_DIM:(hh + 1) * HEAD_DIM], ones8], axis=0)

    def summarize(k_ref, base, v_base):
        nch = k_ref.shape[1] // L
        grp = _group(nch, M_GROUP)

        def body(t, carry):
            lhs, rhs = {}, {}
            for gi in range(grp):
                cl = t * grp + gi
                k2 = (k_ref[0, pl.ds(pl.multiple_of(cl * L, L), L), :] * (HEAD_DIM ** -0.5)).astype(BF16)
                vt2 = vt_ref[:, pl.ds(pl.multiple_of(v_base + cl * L, L), L)]
                w8 = w_ref[base + cl]
                for hh in range(2):
                    va = vaug(vt2, hh)
                    lhs[gi, hh] = jnp.concatenate([va * w8[d * 2 + hh:d * 2 + hh + 1] for d in range(2)],
                                                  axis=0).astype(BF16)
                    rhs[gi, hh] = k2[:, hh * HEAD_DIM:(hh + 1) * HEAD_DIM]
            cc = {key: _dot(lhs[key], rhs[key]) for key in lhs}
            for (gi, hh), val in cc.items():
                for d in range(2):
                    cc_ref[base + t * grp + gi, d * 2 + hh] = val[d * M_AUG:(d + 1) * M_AUG]
            return carry

        lax.fori_loop(0, nch // grp, body, 0)

    def scan_states(base, nch, state):
        def body(t, st):
            m8, cs = st[0], list(st[1:])
            cf, cb = base + t, base + nch - 1 - t
            tot8 = jnp.where(rev8, tot_ref[cb], tot_ref[cf])
            mc8 = jnp.where(rev8, mc_ref[cb], mc_ref[cf])
            m_new = jnp.maximum(tot8 + m8, mc8)
            s_old = jnp.exp(tot8 + m8 - m_new)
            s_new = jnp.exp(mc8 - m_new)
            mp_ref[cf, 0:2, :] = m8[0:2]
            mp_ref[cb, 2:4, :] = m8[2:4]
            for j in range(4):
                c = cf if j < 2 else cb
                summary = cc_ref[c, j]
                cc_ref[c, j] = cs[j]
                cs[j] = s_old[j:j + 1, 0:HEAD_DIM] * cs[j] + s_new[j:j + 1, 0:HEAD_DIM] * summary
            return (m_new, *cs)

        return lax.fori_loop(0, nch, body, state)

    def emit(q_ref, k_ref, gate_ref, o_ref, base, kt_base, v_base):
        nch = q_ref.shape[1] // L
        grp = _group(nch, M_GROUP)

        def body(t, carry):
            units = [(gi, hh) for gi in range(grp) for hh in range(2)]
            rows, vas, rho, alpha, floor, kcol, lhs, rhs = {}, {}, {}, {}, {}, {}, {}, {}
            for gi in range(grp):
                cl = t * grp + gi
                c = base + cl
                rows[gi] = pl.ds(pl.multiple_of(cl * L, L), L)
                q2 = q_ref[0, rows[gi], :].astype(BF16)
                k2 = k_ref[0, rows[gi], :] * (HEAD_DIM ** -0.5)
                vt2 = vt_ref[:, pl.ds(pl.multiple_of(v_base + cl * L, L), L)]
                mp8 = mp_ref[c]
                rho[gi] = -jnp.maximum(mp8, ck_ref[c])
                alpha[gi] = jnp.exp(mp8 + rho[gi])
                floor[gi] = jnp.exp(rho[gi] - b_ref[c])
                kt = kt_ref[kt_base + (cl >> 4)]
                kcol[gi] = pltpu.roll(kt, (LANES - (cl & (KT_CHUNKS - 1)) * 8) & (LANES - 1), 1)
                for hh in range(2):
                    ln = slice(hh * HEAD_DIM, (hh + 1) * HEAD_DIM)
                    vas[gi, hh] = vaug(vt2, hh).astype(BF16)
                    lhs[gi, hh] = jnp.concatenate([k2[:, ln], cc_ref[c, hh], cc_ref[c, 2 + hh]], axis=0).astype(BF16)
                    rhs[gi, hh] = q2[:, ln]
            prod = {u: _dot_nt(lhs[u], rhs[u]) for u in units}
            pt = {}
            for gi, hh in units:
                st = prod[gi, hh][0:L]
                for d in range(2):
                    j = d * 2 + hh
                    e = kcol[gi][:, j:j + 1] + rho[gi][j:j + 1, :]
                    pt[gi, hh, d] = (jnp.where(mask_t[d], jnp.exp(e), 0.0) * st).astype(BF16)
            pv = {key: _dot(vas[key[0], key[1]], val) for key, val in pt.items()}
            for gi in range(grp):
                halves = []
                for hh in range(2):
                    hsum = None
                    for d in range(2):
                        j = d * 2 + hh
                        lo = L + d * M_AUG
                        num = alpha[gi][j:j + 1, :] * prod[gi, hh][lo:lo + M_AUG] + pv[gi, hh, d]
                        h = num[0:HEAD_DIM] / jnp.maximum(jnp.abs(num[HEAD_DIM:HEAD_DIM + 1]), floor[gi][j:j + 1, :])
                        hsum = h if hsum is None else hsum + h
                    ms = jnp.sum(hsum * hsum, axis=0, keepdims=True) * (1.0 / HEAD_DIM)
                    halves.append(hsum * lax.rsqrt(ms + EPS))
                y = jnp.concatenate(halves, axis=0).T
                o_ref[0, rows[gi], :] = y * g_ref[...] * jax.nn.sigmoid(gate_ref[0, rows[gi], :])
            return carry

        lax.fori_loop(0, nch // grp, body, 0)

    n_ctx = qc_ref.shape[1]
    prep(grc_ref, vc_ref, 0, 0, 0)
    prep(grl_ref, vl_ref, ncc, kt_c, n_ctx)
    summarize(kc_ref, 0, 0)
    summarize(kl_ref, ncc, n_ctx)
    zero = (jnp.zeros((8, L), F32),) + (jnp.zeros((M_AUG, HEAD_DIM), F32),) * 4
    state = scan_states(0, ncc, zero)
    scan_states(ncc, ncl, state)
    if with_ctx_out:
        emit(qc_ref, kc_ref, oc_gate_ref, o_ctx_ref, 0, 0, 0)
    emit(ql_ref, kl_ref, ol_gate_ref, o_lat_ref, ncc, kt_c, n_ctx)


def _mlstm_gates(s):
    b, n, _ = s.shape
    g = s[:, :, :2 * N_MGATES].reshape(b, n, 2, 2, 2, 2)
    g = g.transpose(0, 4, 2, 3, 5, 1).reshape(b, 2, 2, 4, n // M_CHUNK, M_CHUNK).transpose(0, 1, 2, 4, 3, 5)
    return jnp.concatenate([g, g], axis=4)


def _mlstm_call(mc, sc, ml, sl, i_bias, f_bias, g, with_ctx_out):
    b, n, _ = ml.shape
    nc = mc.shape[1]
    assert n % M_CHUNK == 0 and nc % M_CHUNK == 0
    ncc, ncl = nc // M_CHUNK, n // M_CHUNK
    bias = jnp.stack([i_bias, f_bias]).astype(F32).reshape(2, 2, 2, 2).transpose(2, 0, 1, 3).reshape(2, 2, 4)
    bias = jnp.concatenate([bias, bias], axis=2)[..., None]
    gr_spec = lambda nch: pl.BlockSpec((1, 1, 2, nch, 8, M_CHUNK), lambda bi, hp: (bi, hp, 0, 0, 0, 0))
    in_specs = (_seq_specs(nc) + [gr_spec(ncc)] + _seq_specs(n) + [gr_spec(ncl)]
                + [pl.BlockSpec((1, 2, 8, 1), lambda bi, hp: (hp, 0, 0, 0)),
                   pl.BlockSpec((1, LANES), lambda bi, hp: (0, hp))])
    out_specs = [pl.BlockSpec((1, n, LANES), lambda bi, hp: (bi, 0, hp))]
    out_shape = [jax.ShapeDtypeStruct((b, n, M_HEADS * HEAD_DIM), F32)]
    if with_ctx_out:
        out_specs.append(pl.BlockSpec((1, nc, LANES), lambda bi, hp: (bi, 0, hp)))
        out_shape.append(jax.ShapeDtypeStruct((b, nc, M_HEADS * HEAD_DIM), F32))
    tot = ncc + ncl
    per_chunk = pltpu.VMEM((tot, 8, M_CHUNK), F32)
    scratch = [per_chunk] * 6 + [
        pltpu.VMEM((-(-ncc // KT_CHUNKS) + -(-ncl // KT_CHUNKS), M_CHUNK, LANES), F32),
        pltpu.VMEM((tot, 4, M_AUG, HEAD_DIM), F32),
        pltpu.VMEM((LANES, nc + n), F32)]
    outs = pl.pallas_call(
        functools.partial(_mlstm_kernel, with_ctx_out=with_ctx_out),
        grid=(b, M_HEADS // 2),
        in_specs=in_specs,
        out_specs=out_specs,
        out_shape=out_shape,
        scratch_shapes=scratch,
        compiler_params=_cparams(("parallel", "arbitrary")),
        name="mlstm",
    )(mc, mc, mc, mc, _mlstm_gates(sc), ml, ml, ml, ml, _mlstm_gates(sl), bias, g.reshape(1, -1))
    return (outs[0], outs[1]) if with_ctx_out else (outs[0], None)


def _head_norm_gate(hsum, g, gate):
    sq = hsum * hsum
    lane = lax.broadcasted_iota(jnp.int32, hsum.shape, 1)
    first = lane < HEAD_DIM
    s0 = jnp.sum(jnp.where(first, sq, 0.0), axis=-1, keepdims=True)
    s1 = jnp.sum(jnp.where(first, 0.0, sq), axis=-1, keepdims=True)
    ms = jnp.where(first, s0, s1) * (1.0 / HEAD_DIM)
    return hsum * lax.rsqrt(ms + EPS) * g * gate


def _seg_scan_sum(x, rev):
    rows = x.shape[0]
    pos = lax.broadcasted_iota(jnp.int32, x.shape, 0) & (CHUNK - 1)
    s = 1
    while s < CHUNK:
        if rev:
            x = x + jnp.where(pos < CHUNK - s, pltpu.roll(x, rows - s, 0), 0.0)
        else:
            x = x + jnp.where(pos >= s, pltpu.roll(x, s, 0), 0.0)
        s *= 2
    return x


def _gla_kernel(*refs, with_ctx_out):
    (qc_ref, kc_ref, vc_ref, oc_gate_ref, sc_ref,
     ql_ref, kl_ref, vl_ref, ol_gate_ref, sl_ref, wa_ref, ba_ref, g_ref) = refs[:13]
    if with_ctx_out:
        o_lat_ref, o_ctx_ref, bc_ref, u_ref, dec_ref = refs[13:]
    else:
        o_lat_ref, bc_ref, u_ref, dec_ref = refs[13:]
        o_ctx_ref = None
    ncc, ncl = qc_ref.shape[1] // CHUNK, ql_ref.shape[1] // CHUNK
    lo = 2 * N_MGATES
    mask_f, mask_b = _tri(CHUNK, False), _tri(CHUNK, True)
    first = lax.broadcasted_iota(jnp.int32, (HEAD_DIM, LANES), 1) < HEAD_DIM
    halves = (slice(0, HEAD_DIM), slice(HEAD_DIM, 2 * HEAD_DIM))
    silu = lambda t: t * jax.nn.sigmoid(t)

    def slab_of(t, grp):
        return pl.multiple_of(t * (grp * CHUNK), grp * CHUNK)

    def summarize(k_ref, v_ref, s_ref, base):
        nch = k_ref.shape[1] // CHUNK
        grp = _group(nch, GLA_GROUP)
        assert grp % 2 == 0

        def body(t, carry):
            r0 = slab_of(t, grp)
            slab = pl.ds(r0, grp * CHUNK)
            k3 = k_ref[0, slab, :].reshape(grp, CHUNK, LANES)
            v2 = v_ref[0, slab, :]
            kws, ends = [], []
            for d in range(2):
                lr = s_ref[0, slab, lo + d * GLA_RANK:lo + (d + 1) * GLA_RANK]
                glog = _log_sigmoid(_dot_hi(lr, wa_ref[d]) + ba_ref[d]) * (1.0 / GLA_TAU)
                bc = _seg_scan_sum(glog, d == 1)
                bc_ref[d, pl.ds(pl.multiple_of(base * CHUNK + r0, CHUNK), grp * CHUNK), :] = bc
                bc3 = bc.reshape(grp, CHUNK, LANES)
                end = bc3[:, 0:1, :] if d == 1 else bc3[:, CHUNK - 1:CHUNK, :]
                kws.append((k3 * jnp.exp(end - bc3)).astype(BF16))
                ends.append(end)
            vts = [v2[p * LANES:(p + 1) * LANES, :].T.astype(BF16) for p in range(grp // 2)]
            us = {}
            for gi in range(grp):
                for hh in range(2):
                    vt = vts[gi // 2][halves[hh], halves[gi % 2]]
                    for d in range(2):
                        us[gi, hh, d] = _dot(vt, kws[d][gi][:, halves[hh]])
            for gi in range(grp):
                c = base + t * grp + gi
                for hh in range(2):
                    u_ref[hh, c] = jnp.concatenate([us[gi, hh, 0], us[gi, hh, 1]], axis=1)
                    end2 = jnp.concatenate([ends[0][gi][:, halves[hh]], ends[1][gi][:, halves[hh]]], axis=1)
                    dec_ref[hh, c] = jnp.broadcast_to(jnp.exp(end2), (8, LANES))
            return carry

        lax.fori_loop(0, nch // grp, body, 0)

    def scan_states(base, nch, state):
        def body(t, st):
            cf, cb = base + t, base + nch - 1 - t
            new = []
            for hh in range(2):
                u = jnp.where(first, u_ref[hh, cf], u_ref[hh, cb])
                dec = jnp.where(first[0:1], dec_ref[hh, cf][0:1], dec_ref[hh, cb][0:1])
                u_ref[hh, cf, :, 0:HEAD_DIM] = st[hh][:, 0:HEAD_DIM]
                u_ref[hh, cb, :, HEAD_DIM:] = st[hh][:, HEAD_DIM:]
                new.append(st[hh] * dec + u)
            return tuple(new)

        return lax.fori_loop(0, nch, body, state)

    def emit(q_ref, k_ref, v_ref, gate_ref, o_ref, base):
        nch = q_ref.shape[1] // CHUNK
        grp = _group(nch, GLA_GROUP)

        def body(t, carry):
            r0 = slab_of(t, grp)
            slab = pl.ds(r0, grp * CHUNK)
            q2 = q_ref[0, slab, :] * (HEAD_DIM ** -0.5)
            k2 = k_ref[0, slab, :]
            vb = v_ref[0, slab, :].astype(BF16)
            qd, kd = [], []
            for d in range(2):
                bc = bc_ref[d, pl.ds(pl.multiple_of(base * CHUNK + r0, CHUNK), grp * CHUNK), :]
                qd.append((q2 * jnp.exp(bc)).astype(BF16))
                kd.append((k2 * jnp.exp(-bc)).astype(BF16))
            units = [(gi, hh) for gi in range(grp) for hh in range(2)]
            rs = lambda gi: slice(gi * CHUNK, (gi + 1) * CHUNK)
            att = {}
            for gi, hh in units:
                a_f = jnp.where(mask_f, _dot_nt(qd[0][rs(gi), halves[hh]], kd[0][rs(gi), halves[hh]]), 0.0)
                a_b = jnp.where(mask_b, _dot_nt(qd[1][rs(gi), halves[hh]], kd[1][rs(gi), halves[hh]]), 0.0)
                att[gi, hh] = (a_f + a_b).astype(BF16)
            outs = []
            for gi in range(grp):
                c = base + t * grp + gi
                parts = []
                for hh in range(2):
                    qcat = jnp.concatenate([qd[0][rs(gi), halves[hh]], qd[1][rs(gi), halves[hh]]], axis=1)
                    parts.append(_dot(att[gi, hh], vb[rs(gi), halves[hh]])
                                 + _dot_nt(qcat, u_ref[hh, c].astype(BF16)))
                outs.append(jnp.concatenate(parts, axis=1))
            o2 = jnp.concatenate(outs, axis=0)
            o_ref[0, slab, :] = _head_norm_gate(o2, g_ref[...], silu(gate_ref[0, slab, :]))
            return carry

        lax.fori_loop(0, nch // grp, body, 0)

    summarize(kc_ref, vc_ref, sc_ref, 0)
    summarize(kl_ref, vl_ref, sl_ref, ncc)
    state = scan_states(0, ncc, (jnp.zeros((HEAD_DIM, LANES), F32),) * 2)
    scan_states(ncc, ncl, state)
    if with_ctx_out:
        emit(qc_ref, kc_ref, vc_ref, oc_gate_ref, o_ctx_ref, 0)
    emit(ql_ref, kl_ref, vl_ref, ol_gate_ref, o_lat_ref, ncc)


def _gla_call(gc, sc, gl, sl, wa2, ba, g, with_ctx_out):
    b, n, _ = gl.shape
    nc = gc.shape[1]
    tot = (nc + n) // CHUNK
    small = lambda nn: pl.BlockSpec((1, nn, SMALL_COLS), lambda bi, hp: (bi, 0, 0))
    in_specs = (_seq_specs(nc) + [small(nc)] + _seq_specs(n) + [small(n)]
                + [pl.BlockSpec((2, GLA_RANK, LANES), lambda bi, hp: (0, 0, hp)),
                   pl.BlockSpec((2, 1, LANES), lambda bi, hp: (0, 0, hp)),
                   pl.BlockSpec((1, LANES), lambda bi, hp: (0, hp))])
    out_specs = [pl.BlockSpec((1, n, LANES), lambda bi, hp: (bi, 0, hp))]
    out_shape = [jax.ShapeDtypeStruct((b, n, G_HEADS * HEAD_DIM), F32)]
    if with_ctx_out:
        out_specs.append(pl.BlockSpec((1, nc, LANES), lambda bi, hp: (bi, 0, hp)))
        out_shape.append(jax.ShapeDtypeStruct((b, nc, G_HEADS * HEAD_DIM), F32))
    scratch = [pltpu.VMEM((2, nc + n, LANES), F32),
               pltpu.VMEM((2, tot, HEAD_DIM, LANES), F32),
               pltpu.VMEM((2, tot, 8, LANES), F32)]
    outs = pl.pallas_call(
        functools.partial(_gla_kernel, with_ctx_out=with_ctx_out),
        grid=(b, G_HEADS // 2),
        in_specs=in_specs,
        out_specs=out_specs,
        out_shape=out_shape,
        scratch_shapes=scratch,
        compiler_params=_cparams(("parallel", "arbitrary")),
        name="gla",
    )(gc, gc, gc, gc, sc, gl, gl, gl, gl, sl, wa2, ba.reshape(2, 1, -1), g.reshape(1, -1))
    return (outs[0], outs[1]) if with_ctx_out else (outs[0], None)


def _outmlp_kernel(*refs, final, ff_tile):
    if final:
        x_ref, oa_ref, om_ref, og_ref, mod_ref, g_ref, wo_ref, w1_ref, w2_ref, fg_ref, y_ref = refs
    else:
        x_ref, oa_ref, om_ref, og_ref, mod_ref, g_ref, wo_ref, w1_ref, w2_ref, y_ref = refs
    mod = mod_ref[0]
    a_w, m_w = oa_ref.shape[2], om_ref.shape[2]
    o = (_dot(oa_ref[0].astype(BF16), wo_ref[0:a_w, :])
         + _dot(om_ref[0].astype(BF16), wo_ref[a_w:a_w + m_w, :])
         + _dot(og_ref[0].astype(BF16), wo_ref[a_w + m_w:, :]))
    x1 = x_ref[0] + mod[2:3] * o
    hb = _norm_mod(x1, g_ref[...], mod[3:4], mod[4:5]).astype(BF16)
    d_ff = w1_ref.shape[1]
    acc = jnp.zeros(x1.shape, F32)
    for j in range(d_ff // ff_tile):
        t = jnp.maximum(_dot(hb, w1_ref[:, j * ff_tile:(j + 1) * ff_tile]), 0.0)
        acc = acc + _dot((t * t).astype(BF16), w2_ref[j * ff_tile:(j + 1) * ff_tile, :])
    x2 = x1 + mod[5:6] * acc
    if final:
        ms = jnp.mean(x2 * x2, axis=-1, keepdims=True)
        x2 = x2 * lax.rsqrt(ms + EPS) * fg_ref[...]
    y_ref[0] = x2


def _outmlp_call(x, oa, om, og, mod, g, wo, w1, w2, final_g):
    bx, n, d = x.shape
    tm = _pick_tile(n, 256)
    d_ff = w1.shape[1]
    row = lambda wd: pl.BlockSpec((1, tm, wd), lambda b, i: (b, i, 0))
    whole = lambda arr: pl.BlockSpec(arr.shape, lambda b, i: (0,) * arr.ndim)
    g2 = g.reshape(1, d)
    in_specs = [row(d), row(oa.shape[2]), row(om.shape[2]), row(og.shape[2]),
                pl.BlockSpec((1, 6, d), lambda b, i: (b, 0, 0)), whole(g2), whole(wo), whole(w1), whole(w2)]
    args = [x, oa, om, og, mod, g2, wo, w1, w2]
    if final_g is not None:
        fg = final_g.reshape(1, d)
        in_specs.append(whole(fg))
        args.append(fg)
    return pl.pallas_call(
        functools.partial(_outmlp_kernel, final=final_g is not None, ff_tile=_pick_tile(d_ff, 1024)),
        grid=(bx, n // tm),
        in_specs=in_specs,
        out_specs=row(d),
        out_shape=jax.ShapeDtypeStruct((bx, n, d), F32),
        compiler_params=_cparams(("parallel", "parallel")),
        name="outproj_mlp",
    )(*args)


def _rope_tables(n):
    t = jnp.arange(n)
    inv = ROPE_BASE ** (-jnp.arange(0, ROPE_AXIS_DIM, 2, dtype=F32) / ROPE_AXIS_DIM)
    ang_r = (t // GRID_W).astype(F32)[:, None] * inv[None, :]
    ang_c = (t % GRID_W).astype(F32)[:, None] * inv[None, :]
    half = ROPE_AXIS_DIM // 2
    cos = jnp.concatenate([jnp.cos(ang_r)] * 2 + [jnp.cos(ang_c)] * 2, axis=1)
    zero = jnp.zeros((n, half), F32)
    sin_a = jnp.concatenate([zero, jnp.sin(ang_r), zero, jnp.sin(ang_c)], axis=1)
    sin_b = jnp.concatenate([-jnp.sin(ang_r), zero, -jnp.sin(ang_c), zero], axis=1)
    return tuple(jnp.tile(a, (1, LANES // HEAD_DIM)).astype(F32) for a in (cos, sin_a, sin_b))


def _pair_heads(a, axis):
    shape = a.shape
    a = a.reshape(shape[:axis] + (ATTN_KV_HEADS, ATTN_GROUP, HEAD_DIM) + shape[axis + 1:])
    return jnp.swapaxes(a, axis, axis + 1).reshape(shape)


def _permute_w_in(w_in):
    aq, ak, av, mq, mk, mv, mo, mi, mf, gq, gk, gv, gg, glr = jnp.split(
        w_in, np.cumsum([Q_COLS, 128, 128, 256, 256, 256, 256, 8, 8, 256, 256, 256, 256])[:13].tolist(), axis=-1)
    pad = jnp.zeros(w_in.shape[:-1] + (SMALL_COLS - 16 - 2 * GLA_RANK,), w_in.dtype)
    return jnp.concatenate([_pair_heads(aq, aq.ndim - 1), ak, av, mq, mk, mv, mo, gq, gk, gv, gg, mi, mf, glr, pad],
                           axis=-1).astype(BF16)


def _permute_w_out(w_out):
    return jnp.concatenate([_pair_heads(w_out[:, :Q_COLS], 1), w_out[:, Q_COLS:]], axis=1).astype(BF16)


def kernel(x, c, ctx, c_ctx, w_ada, b_ada, norm1_g, norm2_g, w_in, attn_sink, m_i_bias, m_f_bias, m_norm_g,
           g_wa2, g_ba, g_norm_g, w_out, w_mlp1, w_mlp2, final_g):
    B, N, D = x.shape
    Nc = ctx.shape[1]
    depth = w_ada.shape[0]
    rope = _rope_tables(N)
    rows = -(-(B + 1) // 8) * 8
    cc = jnp.zeros((rows, D), F32).at[:B].set(c).at[B].set(c_ctx)
    mods = _ada_call(cc, w_ada, b_ada)
    mods_x = mods[:, :B].reshape(depth, B, 6, D)
    mods_c = mods[:, B:B + 1].reshape(depth, 1, 6, D)
    w_in_p = _permute_w_in(w_in)
    wo, w1, w2 = _permute_w_out(w_out), w_mlp1.astype(BF16), w_mlp2.astype(BF16)

    xc = ctx
    for l in range(depth):
        last = l == depth - 1
        q, kv, ml, gl, sl = _inproj_call(x, mods_x[l], norm1_g[l], w_in_p[l], rope)
        ctx_parts = _inproj_call(xc.reshape(1, B * Nc, D), mods_c[l], norm1_g[l], w_in_p[l], None)
        qc, kvc, mc, gc, sc = (t.reshape(B, Nc, -1) for t in ctx_parts)
        oa = _attn_call(attn_sink[l], q, kv, kvc)
        om, omc = _mlstm_call(mc, sc, ml, sl, m_i_bias[l], m_f_bias[l], m_norm_g[l], not last)
        og, ogc = _gla_call(gc, sc, gl, sl, g_wa2[l], g_ba[l], g_norm_g[l], not last)
        x = _outmlp_call(x, oa, om, og, mods_x[l], norm2_g[l], wo[l], w1[l], w2[l], final_g if last else None)
        if not last:
            oac = _attn_call(attn_sink[l], qc, None, kvc)
            flat = lambda t: t.reshape(1, B * Nc, -1)
            xc = _outmlp_call(flat(xc), flat(oac), flat(omc), flat(ogc), mods_c[l], norm2_g[l],
                              wo[l], w1[l], w2[l], None).reshape(B, Nc, D)
    return x
```

```python
import functools

import numpy as np
import jax
import jax.numpy as jnp
from jax import lax
from jax.experimental import pallas as pl
from jax.experimental.pallas import tpu as pltpu

F32 = jnp.float32
BF16 = jnp.bfloat16

HEAD_DIM = 64
GRID_W = 64
ATTN_HEADS = 8
ATTN_KV_HEADS = 2
ATTN_GROUP = ATTN_HEADS // ATTN_KV_HEADS
WINDOW = 128
ATTN_BLOCK = 128
ATTN_SUBBLOCKS = 2
ROPE_BASE = 10000.0
ROPE_AXIS_DIM = HEAD_DIM // 2
M_HEADS = 4
G_HEADS = 4
CHUNK = 64
GLA_GROUP = 16
M_CHUNK = 128
M_GROUP = 8
M_AUG = HEAD_DIM + 8
KT_CHUNKS = 128 // 8
GLA_RANK = 16
GLA_TAU = 16.0
EPS = 1e-6
NEG_BIG = -1e30
LOG2E = 1.4426950408889634

LANES = 128
Q_COLS = ATTN_HEADS * HEAD_DIM
KV_COLS = 2 * ATTN_KV_HEADS * HEAD_DIM
MIX_COLS = 4 * M_HEADS * HEAD_DIM
SMALL_COLS = LANES
N_MGATES = 2 * 2 * 2
VMEM_LIMIT = 56 * 1024 * 1024


def _cparams(sem):
    return pltpu.CompilerParams(dimension_semantics=sem, vmem_limit_bytes=VMEM_LIMIT)


def _pick_tile(n, pref):
    t = pref
    while n % t:
        t //= 2
    return t


def _split3(a):
    a1 = a.astype(BF16)
    r1 = a - a1.astype(F32)
    a2 = r1.astype(BF16)
    a3 = (r1 - a2.astype(F32)).astype(BF16)
    return a1, a2, a3


def _dot(a, b):
    return jnp.dot(a, b, preferred_element_type=F32)


def _dot_nt(a, b):
    return lax.dot_general(a, b, (((1,), (1,)), ((), ())), preferred_element_type=F32)


def _dot_hi(a, b):
    a1, a2, _ = _split3(a)
    b1, b2, _ = _split3(b)
    return _dot(a1, b1) + (_dot(a1, b2) + _dot(a2, b1))


def _log_sigmoid(x):
    return jnp.minimum(x, 0.0) - jnp.log1p(jnp.exp(-jnp.abs(x)))


def _ada_kernel(cc_ref, w_ref, b_ref, o_ref):
    cc = cc_ref[...]
    s = cc * jax.nn.sigmoid(cc)
    o_ref[0] = _dot_hi(s, w_ref[0]) + b_ref[0]


def _ada_call(cc, w_ada, b_ada):
    depth, d, six_d = w_ada.shape
    rows = cc.shape[0]
    tn = _pick_tile(six_d, 1024)
    return pl.pallas_call(
        _ada_kernel,
        grid=(depth, six_d // tn),
        in_specs=[pl.BlockSpec((rows, d), lambda l, j: (0, 0)),
                  pl.BlockSpec((1, d, tn), lambda l, j: (l, 0, j)),
                  pl.BlockSpec((1, 1, tn), lambda l, j: (l, 0, j))],
        out_specs=pl.BlockSpec((1, rows, tn), lambda l, j: (l, 0, j)),
        out_shape=jax.ShapeDtypeStruct((depth, rows, six_d), F32),
        compiler_params=_cparams(("arbitrary", "arbitrary")),
        name="ada_mod",
    )(cc, w_ada, b_ada.reshape(depth, 1, six_d))


def _norm_mod(x, g, shift, scale):
    ms = jnp.mean(x * x, axis=-1, keepdims=True)
    h = x * lax.rsqrt(ms + EPS) * g
    return h * (1.0 + scale) + shift


def _inproj_kernel(*refs, use_rope):
    if use_rope:
        x_ref, mod_ref, g_ref, w_ref, cos_ref, sa_ref, sb_ref, q_ref, kv_ref, m_ref, gl_ref, s_ref = refs
    else:
        x_ref, mod_ref, g_ref, w_ref, q_ref, kv_ref, m_ref, gl_ref, s_ref = refs
    mod = mod_ref[0]
    hb = _norm_mod(x_ref[0], g_ref[...], mod[0:1], mod[1:2]).astype(BF16)

    def proj(lo, width):
        return _dot(hb, w_ref[:, lo:lo + width])

    pa = proj(0, Q_COLS + KV_COLS)
    if use_rope:
        cos, sa, sb = cos_ref[...], sa_ref[...], sb_ref[...]
        segs = []
        for j in range((Q_COLS + KV_COLS // 2) // LANES):
            seg = pa[:, j * LANES:(j + 1) * LANES]
            segs.append(seg * cos + pltpu.roll(seg, ROPE_AXIS_DIM // 2, 1) * sa
                        + pltpu.roll(seg, LANES - ROPE_AXIS_DIM // 2, 1) * sb)
        qk = jnp.concatenate(segs, axis=1)
    else:
        qk = pa[:, :Q_COLS + KV_COLS // 2]
    q_ref[0] = qk[:, :Q_COLS] * (HEAD_DIM ** -0.5 * LOG2E)
    kv_ref[0, :, :KV_COLS // 2] = qk[:, Q_COLS:]
    kv_ref[0, :, KV_COLS // 2:] = pa[:, Q_COLS + KV_COLS // 2:]
    lo = Q_COLS + KV_COLS
    m_ref[0] = proj(lo, MIX_COLS)
    gl_ref[0] = proj(lo + MIX_COLS, MIX_COLS)
    s_ref[0] = proj(lo + 2 * MIX_COLS, SMALL_COLS)


def _inproj_call(x, mod, g, w, rope):
    bx, n, d = x.shape
    tm = _pick_tile(n, 512)
    cols = w.shape[1]
    in_specs = [pl.BlockSpec((1, tm, d), lambda b, i: (b, i, 0)),
                pl.BlockSpec((1, 6, d), lambda b, i: (b, 0, 0)),
                pl.BlockSpec((1, d), lambda b, i: (0, 0)),
                pl.BlockSpec((d, cols), lambda b, i: (0, 0))]
    args = [x, mod, g.reshape(1, d), w]
    if rope is not None:
        in_specs += [pl.BlockSpec((tm, LANES), lambda b, i: (i, 0))] * 3
        args += list(rope)
    widths = (Q_COLS, KV_COLS, MIX_COLS, MIX_COLS, SMALL_COLS)
    return pl.pallas_call(
        functools.partial(_inproj_kernel, use_rope=rope is not None),
        grid=(bx, n // tm),
        in_specs=in_specs,
        out_specs=[pl.BlockSpec((1, tm, wd), lambda b, i: (b, i, 0)) for wd in widths],
        out_shape=[jax.ShapeDtypeStruct((bx, n, wd), F32) for wd in widths],
        compiler_params=_cparams(("parallel", "parallel")),
        name="inproj_rope" if rope is not None else "inproj",
    )(*args)


def _attn_kernel(*refs, local, tq, nsub):
    if local:
        sink_ref, q_ref, kv_ref, kvc_ref, o_ref, km_ref, vt_ref, bias_ref, kmc_ref, vtc_ref = refs
    else:
        sink_ref, q_ref, kvc_ref, o_ref, kmc_ref, vtc_ref = refs
    npairs = ATTN_GROUP
    cols = npairs * tq
    span = 3 * ATTN_BLOCK
    lane = lax.broadcasted_iota(jnp.int32, (1, LANES), 1)
    head_lanes = (lane < HEAD_DIM, lane >= HEAD_DIM)
    row8 = lax.broadcasted_iota(jnp.int32, (8, LANES), 0)
    ones8 = jnp.where(row8 == 0, 1.0, 0.0).astype(F32)

    def prepare(src_ref, km, vt):
        def body(i, carry):
            rows = pl.ds(pl.multiple_of(i * LANES, LANES), LANES)
            k128 = src_ref[0, rows, 0:LANES]
            v_t = src_ref[0, rows, LANES:2 * LANES].T
            for kvh in range(ATTN_KV_HEADS):
                km[kvh, rows, :] = jnp.where(head_lanes[kvh], k128, 0.0).astype(BF16)
                vt[kvh, 0:HEAD_DIM, rows] = v_t[kvh * HEAD_DIM:(kvh + 1) * HEAD_DIM]
                vt[kvh, HEAD_DIM:M_AUG, rows] = ones8
            return carry

        lax.fori_loop(0, src_ref.shape[1] // LANES, body, 0)

    @pl.when(pl.program_id(1) == 0)
    def _():
        prepare(kvc_ref, kmc_ref, vtc_ref)
        if local:
            prepare(kv_ref, km_ref, vt_ref)
            rel = lax.broadcasted_iota(jnp.int32, bias_ref.shape, 0) - 2 * ATTN_BLOCK
            qoff = lax.broadcasted_iota(jnp.int32, bias_ref.shape, 1) & (tq - 1)
            bias_ref[...] = jnp.where(jnp.abs(rel - qoff) <= WINDOW, 0.0, NEG_BIG)

    units = [(sb, kvh) for sb in range(nsub) for kvh in range(ATTN_KV_HEADS)]
    qall, start, band = {}, {}, {}
    for sb in range(nsub):
        q = q_ref[0, sb * tq:(sb + 1) * tq, :].astype(BF16)
        qall[sb] = jnp.concatenate([q[:, p * LANES:(p + 1) * LANES] for p in range(npairs)], axis=0)
        if local:
            j = pl.program_id(1) * nsub + sb
            start[sb] = pl.multiple_of(jnp.clip((j - 1) * ATTN_BLOCK, 0, kv_ref.shape[1] - span), ATTN_BLOCK)
            band[sb] = bias_ref[pl.ds(pl.multiple_of(start[sb] - j * ATTN_BLOCK + 2 * ATTN_BLOCK, ATTN_BLOCK),
                                      span), :]
    sink = [jnp.concatenate([jnp.full((1, tq), sink_ref[kvh * ATTN_GROUP + p] * LOG2E, F32)
                             for p in range(npairs)], axis=1) for kvh in range(ATTN_KV_HEADS)]
    s_ctx, s_loc, m, acc = {}, {}, {}, {}
    for sb, kvh in units:
        s_ctx[sb, kvh] = _dot_nt(kmc_ref[kvh], qall[sb])
        if local:
            s_loc[sb, kvh] = _dot_nt(km_ref[kvh, pl.ds(start[sb], span), :], qall[sb]) + band[sb]
    for u in units:
        m[u] = jnp.maximum(jnp.max(s_ctx[u], axis=0, keepdims=True), sink[u[1]])
        if local:
            m[u] = jnp.maximum(m[u], jnp.max(s_loc[u], axis=0, keepdims=True))
    for sb, kvh in units:
        u = (sb, kvh)
        acc[u] = _dot(vtc_ref[kvh].astype(BF16), jnp.exp2(s_ctx[u] - m[u]).astype(BF16))
        if local:
            acc[u] = acc[u] + _dot(vt_ref[kvh, :, pl.ds(start[sb], span)].astype(BF16),
                                   jnp.exp2(s_loc[u] - m[u]).astype(BF16))
    o_t = {u: acc[u][0:HEAD_DIM] / (acc[u][HEAD_DIM:HEAD_DIM + 1] + jnp.exp2(sink[u[1]] - m[u])) for u in units}
    for sb in range(nsub):
        for p in range(npairs):
            for c in range(tq // LANES):
                sl = slice(p * tq + c * LANES, p * tq + (c + 1) * LANES)
                tile = jnp.concatenate([o_t[sb, kvh][:, sl] for kvh in range(ATTN_KV_HEADS)], axis=0)
                r0 = sb * tq + c * LANES
                o_ref[0, r0:r0 + LANES, p * LANES:(p + 1) * LANES] = tile.T


def _attn_call(sink, q, kv, kvc):
    b, nq, _ = q.shape
    nc = kvc.shape[1]
    local = kv is not None
    tq = ATTN_BLOCK if local else nq
    nsub = ATTN_SUBBLOCKS if local else 1
    assert tq % LANES == 0 and nc % LANES == 0 and tq & (tq - 1) == 0 and nq % (nsub * tq) == 0
    in_specs = [pl.BlockSpec(memory_space=pltpu.SMEM),
                pl.BlockSpec((1, nsub * tq, Q_COLS), lambda bi, j: (bi, j, 0))]
    args = [sink, q]
    scratch = []
    if local:
        in_specs.append(pl.BlockSpec((1, nq, KV_COLS), lambda bi, j: (bi, 0, 0)))
        args.append(kv)
        scratch += [pltpu.VMEM((ATTN_KV_HEADS, nq, LANES), BF16), pltpu.VMEM((ATTN_KV_HEADS, M_AUG, nq), F32),
                    pltpu.VMEM((5 * ATTN_BLOCK, ATTN_GROUP * tq), F32)]
    in_specs.append(pl.BlockSpec((1, nc, KV_COLS), lambda bi, j: (bi, 0, 0)))
    args.append(kvc)
    scratch += [pltpu.VMEM((ATTN_KV_HEADS, nc, LANES), BF16), pltpu.VMEM((ATTN_KV_HEADS, M_AUG, nc), F32)]
    return pl.pallas_call(
        functools.partial(_attn_kernel, local=local, tq=tq, nsub=nsub),
        grid=(b, nq // (nsub * tq)),
        in_specs=in_specs,
        out_specs=pl.BlockSpec((1, nsub * tq, Q_COLS), lambda bi, j: (bi, j, 0)),
        out_shape=jax.ShapeDtypeStruct((b, nq, Q_COLS), F32),
        scratch_shapes=scratch,
        compiler_params=_cparams(("arbitrary", "arbitrary")),
        name="attn_window" if local else "attn_ctx",
    )(*args)


def _tri(n, rev):
    ri = lax.broadcasted_iota(jnp.int32, (n, n), 0)
    ci = lax.broadcasted_iota(jnp.int32, (n, n), 1)
    return ci >= ri if rev else ci <= ri


def _seq_specs(n):
    return [pl.BlockSpec((1, n, LANES), lambda b, hp, off=off: (b, 0, off + hp)) for off in (0, 2, 4, 6)]


def _group(nchunks, pref):
    g = min(pref, nchunks)
    assert nchunks % g == 0
    return g


def _mlstm_kernel(*refs, with_ctx_out):
    (qc_ref, kc_ref, vc_ref, oc_gate_ref, grc_ref,
     ql_ref, kl_ref, vl_ref, ol_gate_ref, grl_ref, bias_ref, g_ref) = refs[:12]
    if with_ctx_out:
        o_lat_ref, o_ctx_ref = refs[12:14]
        scratch = refs[14:]
    else:
        o_lat_ref, o_ctx_ref = refs[12], None
        scratch = refs[13:]
    w_ref, b_ref, ck_ref, tot_ref, mc_ref, mp_ref, kt_ref, cc_ref, vt_ref = scratch
    L = M_CHUNK
    ncc, ncl = qc_ref.shape[1] // L, ql_ref.shape[1] // L
    kt_c = -(-ncc // KT_CHUNKS)

    mask_t = (_tri(L, True), _tri(L, False))
    tri_f, tri_b = (mk.astype(BF16) for mk in mask_t)
    ones_m = jnp.ones((L, L), BF16)
    row8 = lax.broadcasted_iota(jnp.int32, (8, L), 0)
    rev8 = (row8 & 2) != 0
    ones8 = jnp.where(row8 == 0, 1.0, 0.0).astype(F32)

    def prep(gr_ref, v_ref, base, kt_base, v_base):
        nch = gr_ref.shape[3]
        li = (gr_ref[0, 0, 0] + bias_ref[0, 0]).reshape(nch * 8, L)
        lf = _log_sigmoid(gr_ref[0, 0, 1] + bias_ref[0, 1]).reshape(nch * 8, L)
        rev = (lax.broadcasted_iota(jnp.int32, li.shape, 0) & 2) != 0
        lane = lax.broadcasted_iota(jnp.int32, li.shape, 1)
        l1, l2, l3 = _split3(lf)
        scan = lambda m: _dot(l1, m) + (_dot(l2, m) + _dot(l3, m))
        b = jnp.where(rev, scan(tri_b), scan(tri_f))
        tot = scan(ones_m)
        g = tot - b + li
        mc = jnp.max(g, axis=1, keepdims=True)
        kap = li - b
        ckf, ckb = kap, kap
        s = 1
        while s < L:
            ckf = jnp.maximum(ckf, jnp.where(lane >= s, pltpu.roll(ckf, s, 1), NEG_BIG))
            ckb = jnp.maximum(ckb, jnp.where(lane < L - s, pltpu.roll(ckb, L - s, 1), NEG_BIG))
            s *= 2
        sl = pl.ds(base, nch)
        w_ref[sl] = jnp.exp(g - mc).reshape(nch, 8, L)
        b_ref[sl] = b.reshape(nch, 8, L)
        ck_ref[sl] = jnp.where(rev, ckb, ckf).reshape(nch, 8, L)
        tot_ref[sl] = tot.reshape(nch, 8, L)
        mc_ref[sl] = jnp.broadcast_to(mc, (nch * 8, L)).reshape(nch, 8, L)
        for gi in range(-(-nch // KT_CHUNKS)):
            blk = kap[gi * LANES:(gi + 1) * LANES]
            if blk.shape[0] < LANES:
                blk = jnp.concatenate([blk, jnp.zeros((LANES - blk.shape[0], L), F32)], axis=0)
            kt_ref[kt_base + gi] = blk.T

        tiles = _group(nch, 4)

        def vt_body(i, carry):
            blocks = [v_ref[0, pl.ds(pl.multiple_of((i * tiles + u) * L, L), L), :].T for u in range(tiles)]
            for u, blk in enumerate(blocks):
                vt_ref[:, pl.ds(pl.multiple_of(v_base + (i * tiles + u) * L, L), L)] = blk
            return carry

        lax.fori_loop(0, nch // tiles, vt_body, 0)

    def vaug(vt2, hh):
        return jnp.concatenate([vt2[hh * HEAD_DIM:(hh + 1) * HEAD_DIM], ones8], axis=0)

    def summarize(k_ref, base, v_base):
        nch = k_ref.shape[1] // L
        grp = _group(nch, M_GROUP)

        def body(t, carry):
            lhs, rhs = {}, {}
            for gi in range(grp):
                cl = t * grp + gi
                k2 = (k_ref[0, pl.ds(pl.multiple_of(cl * L, L), L), :] * (HEAD_DIM ** -0.5)).astype(BF16)
                vt2 = vt_ref[:, pl.ds(pl.multiple_of(v_base + cl * L, L), L)]
                w8 = w_ref[base + cl]
                for hh in range(2):
                    va = vaug(vt2, hh)
                    lhs[gi, hh] = jnp.concatenate([va * w8[d * 2 + hh:d * 2 + hh + 1] for d in range(2)],
                                                  axis=0).astype(BF16)
                    rhs[gi, hh] = k2[:, hh * HEAD_DIM:(hh + 1) * HEAD_DIM]
            cc = {key: _dot(lhs[key], rhs[key]) for key in lhs}
            for (gi, hh), val in cc.items():
                for d in range(2):
                    cc_ref[base + t * grp + gi, d * 2 + hh] = val[d * M_AUG:(d + 1) * M_AUG]
            return carry

        lax.fori_loop(0, nch // grp, body, 0)

    def scan_states(base, nch, state):
        def body(t, st):
            m8, cs = st[0], list(st[1:])
            cf, cb = base + t, base + nch - 1 - t
            tot8 = jnp.where(rev8, tot_ref[cb], tot_ref[cf])
            mc8 = jnp.where(rev8, mc_ref[cb], mc_ref[cf])
            m_new = jnp.maximum(tot8 + m8, mc8)
            s_old = jnp.exp(tot8 + m8 - m_new)
            s_new = jnp.exp(mc8 - m_new)
            mp_ref[cf, 0:2, :] = m8[0:2]
            mp_ref[cb, 2:4, :] = m8[2:4]
            for j in range(4):
                c = cf if j < 2 else cb
                summary = cc_ref[c, j]
                cc_ref[c, j] = cs[j]
                cs[j] = s_old[j:j + 1, 0:HEAD_DIM] * cs[j] + s_new[j:j + 1, 0:HEAD_DIM] * summary
            return (m_new, *cs)

        return lax.fori_loop(0, nch, body, state)

    def emit(q_ref, k_ref, gate_ref, o_ref, base, kt_base, v_base):
        nch = q_ref.shape[1] // L
        grp = _group(nch, M_GROUP)

        def body(t, carry):
            units = [(gi, hh) for gi in range(grp) for hh in range(2)]
            rows, vas, rho, alpha, floor, kcol, lhs, rhs = {}, {}, {}, {}, {}, {}, {}, {}
            for gi in range(grp):
                cl = t * grp + gi
                c = base + cl
                rows[gi] = pl.ds(pl.multiple_of(cl * L, L), L)
                q2 = q_ref[0, rows[gi], :].astype(BF16)
                k2 = k_ref[0, rows[gi], :] * (HEAD_DIM ** -0.5)
                vt2 = vt_ref[:, pl.ds(pl.multiple_of(v_base + cl * L, L), L)]
                mp8 = mp_ref[c]
                rho[gi] = -jnp.maximum(mp8, ck_ref[c])
                alpha[gi] = jnp.exp(mp8 + rho[gi])
                floor[gi] = jnp.exp(rho[gi] - b_ref[c])
                kt = kt_ref[kt_base + (cl >> 4)]
                kcol[gi] = pltpu.roll(kt, (LANES - (cl & (KT_CHUNKS - 1)) * 8) & (LANES - 1), 1)
                for hh in range(2):
                    ln = slice(hh * HEAD_DIM, (hh + 1) * HEAD_DIM)
                    vas[gi, hh] = vaug(vt2, hh).astype(BF16)
                    lhs[gi, hh] = jnp.concatenate([k2[:, ln], cc_ref[c, hh], cc_ref[c, 2 + hh]], axis=0).astype(BF16)
                    rhs[gi, hh] = q2[:, ln]
            prod = {u: _dot_nt(lhs[u], rhs[u]) for u in units}
            pt = {}
            for gi, hh in units:
                st = prod[gi, hh][0:L]
                for d in range(2):
                    j = d * 2 + hh
                    e = kcol[gi][:, j:j + 1] + rho[gi][j:j + 1, :]
                    pt[gi, hh, d] = (jnp.where(mask_t[d], jnp.exp(e), 0.0) * st).astype(BF16)
            pv = {key: _dot(vas[key[0], key[1]], val) for key, val in pt.items()}
            for gi in range(grp):
                halves = []
                for hh in range(2):
                    hsum = None
                    for d in range(2):
                        j = d * 2 + hh
                        lo = L + d * M_AUG
                        num = alpha[gi][j:j + 1, :] * prod[gi, hh][lo:lo + M_AUG] + pv[gi, hh, d]
                        h = num[0:HEAD_DIM] / jnp.maximum(jnp.abs(num[HEAD_DIM:HEAD_DIM + 1]), floor[gi][j:j + 1, :])
                        hsum = h if hsum is None else hsum + h
                    ms = jnp.sum(hsum * hsum, axis=0, keepdims=True) * (1.0 / HEAD_DIM)
                    halves.append(hsum * lax.rsqrt(ms + EPS))
                y = jnp.concatenate(halves, axis=0).T
                o_ref[0, rows[gi], :] = y * g_ref[...] * jax.nn.sigmoid(gate_ref[0, rows[gi], :])
            return carry

        lax.fori_loop(0, nch // grp, body, 0)

    n_ctx = qc_ref.shape[1]
    prep(grc_ref, vc_ref, 0, 0, 0)
    prep(grl_ref, vl_ref, ncc, kt_c, n_ctx)
    summarize(kc_ref, 0, 0)
    summarize(kl_ref, ncc, n_ctx)
    zero = (jnp.zeros((8, L), F32),) + (jnp.zeros((M_AUG, HEAD_DIM), F32),) * 4
    state = scan_states(0, ncc, zero)
    scan_states(ncc, ncl, state)
    if with_ctx_out:
        emit(qc_ref, kc_ref, oc_gate_ref, o_ctx_ref, 0, 0, 0)
    emit(ql_ref, kl_ref, ol_gate_ref, o_lat_ref, ncc, kt_c, n_ctx)


def _mlstm_gates(s):
    b, n, _ = s.shape
    g = s[:, :, :2 * N_MGATES].reshape(b, n, 2, 2, 2, 2)
    g = g.transpose(0, 4, 2, 3, 5, 1).reshape(b, 2, 2, 4, n // M_CHUNK, M_CHUNK).transpose(0, 1, 2, 4, 3, 5)
    return jnp.concatenate([g, g], axis=4)


def _mlstm_call(mc, sc, ml, sl, i_bias, f_bias, g, with_ctx_out):
    b, n, _ = ml.shape
    nc = mc.shape[1]
    assert n % M_CHUNK == 0 and nc % M_CHUNK == 0
    ncc, ncl = nc // M_CHUNK, n // M_CHUNK
    bias = jnp.stack([i_bias, f_bias]).astype(F32).reshape(2, 2, 2, 2).transpose(2, 0, 1, 3).reshape(2, 2, 4)
    bias = jnp.concatenate([bias, bias], axis=2)[..., None]
    gr_spec = lambda nch: pl.BlockSpec((1, 1, 2, nch, 8, M_CHUNK), lambda bi, hp: (bi, hp, 0, 0, 0, 0))
    in_specs = (_seq_specs(nc) + [gr_spec(ncc)] + _seq_specs(n) + [gr_spec(ncl)]
                + [pl.BlockSpec((1, 2, 8, 1), lambda bi, hp: (hp, 0, 0, 0)),
                   pl.BlockSpec((1, LANES), lambda bi, hp: (0, hp))])
    out_specs = [pl.BlockSpec((1, n, LANES), lambda bi, hp: (bi, 0, hp))]
    out_shape = [jax.ShapeDtypeStruct((b, n, M_HEADS * HEAD_DIM), F32)]
    if with_ctx_out:
        out_specs.append(pl.BlockSpec((1, nc, LANES), lambda bi, hp: (bi, 0, hp)))
        out_shape.append(jax.ShapeDtypeStruct((b, nc, M_HEADS * HEAD_DIM), F32))
    tot = ncc + ncl
    per_chunk = pltpu.VMEM((tot, 8, M_CHUNK), F32)
    scratch = [per_chunk] * 6 + [
        pltpu.VMEM((-(-ncc // KT_CHUNKS) + -(-ncl // KT_CHUNKS), M_CHUNK, LANES), F32),
        pltpu.VMEM((tot, 4, M_AUG, HEAD_DIM), F32),
        pltpu.VMEM((LANES, nc + n), F32)]
    outs = pl.pallas_call(
        functools.partial(_mlstm_kernel, with_ctx_out=with_ctx_out),
        grid=(b, M_HEADS // 2),
        in_specs=in_specs,
        out_specs=out_specs,
        out_shape=out_shape,
        scratch_shapes=scratch,
        compiler_params=_cparams(("parallel", "arbitrary")),
        name="mlstm",
    )(mc, mc, mc, mc, _mlstm_gates(sc), ml, ml, ml, ml, _mlstm_gates(sl), bias, g.reshape(1, -1))
    return (outs[0], outs[1]) if with_ctx_out else (outs[0], None)


def _head_norm_gate(hsum, g, gate):
    sq = hsum * hsum
    lane = lax.broadcasted_iota(jnp.int32, hsum.shape, 1)
    first = lane < HEAD_DIM
    s0 = jnp.sum(jnp.where(first, sq, 0.0), axis=-1, keepdims=True)
    s1 = jnp.sum(jnp.where(first, 0.0, sq), axis=-1, keepdims=True)
    ms = jnp.where(first, s0, s1) * (1.0 / HEAD_DIM)
    return hsum * lax.rsqrt(ms + EPS) * g * gate


def _seg_scan_sum(x, rev):
    rows = x.shape[0]
    pos = lax.broadcasted_iota(jnp.int32, x.shape, 0) & (CHUNK - 1)
    s = 1
    while s < CHUNK:
        if rev:
            x = x + jnp.where(pos < CHUNK - s, pltpu.roll(x, rows - s, 0), 0.0)
        else:
            x = x + jnp.where(pos >= s, pltpu.roll(x, s, 0), 0.0)
        s *= 2
    return x


def _gla_kernel(*refs, with_ctx_out):
    (qc_ref, kc_ref, vc_ref, oc_gate_ref, sc_ref,
     ql_ref, kl_ref, vl_ref, ol_gate_ref, sl_ref, wa_ref, ba_ref, g_ref) = refs[:13]
    if with_ctx_out:
        o_lat_ref, o_ctx_ref, bc_ref, u_ref, dec_ref = refs[13:]
    else:
        o_lat_ref, bc_ref, u_ref, dec_ref = refs[13:]
        o_ctx_ref = None
    ncc, ncl = qc_ref.shape[1] // CHUNK, ql_ref.shape[1] // CHUNK
    lo = 2 * N_MGATES
    mask_f, mask_b = _tri(CHUNK, False), _tri(CHUNK, True)
    first = lax.broadcasted_iota(jnp.int32, (HEAD_DIM, LANES), 1) < HEAD_DIM
    halves = (slice(0, HEAD_DIM), slice(HEAD_DIM, 2 * HEAD_DIM))
    silu = lambda t: t * jax.nn.sigmoid(t)

    def slab_of(t, grp):
        return pl.multiple_of(t * (grp * CHUNK), grp * CHUNK)

    def summarize(k_ref, v_ref, s_ref, base):
        nch = k_ref.shape[1] // CHUNK
        grp = _group(nch, GLA_GROUP)
        assert grp % 2 == 0

        def body(t, carry):
            r0 = slab_of(t, grp)
            slab = pl.ds(r0, grp * CHUNK)
            k3 = k_ref[0, slab, :].reshape(grp, CHUNK, LANES)
            v2 = v_ref[0, slab, :]
            kws, ends = [], []
            for d in range(2):
                lr = s_ref[0, slab, lo + d * GLA_RANK:lo + (d + 1) * GLA_RANK]
                glog = _log_sigmoid(_dot_hi(lr, wa_ref[d]) + ba_ref[d]) * (1.0 / GLA_TAU)
                bc = _seg_scan_sum(glog, d == 1)
                bc_ref[d, pl.ds(pl.multiple_of(base * CHUNK + r0, CHUNK), grp * CHUNK), :] = bc
                bc3 = bc.reshape(grp, CHUNK, LANES)
                end = bc3[:, 0:1, :] if d == 1 else bc3[:, CHUNK - 1:CHUNK, :]
                kws.append((k3 * jnp.exp(end - bc3)).astype(BF16))
                ends.append(end)
            vts = [v2[p * LANES:(p + 1) * LANES, :].T.astype(BF16) for p in range(grp // 2)]
            us = {}
            for gi in range(grp):
                for hh in range(2):
                    vt = vts[gi // 2][halves[hh], halves[gi % 2]]
                    for d in range(2):
                        us[gi, hh, d] = _dot(vt, kws[d][gi][:, halves[hh]])
            for gi in range(grp):
                c = base + t * grp + gi
                for hh in range(2):
                    u_ref[hh, c] = jnp.concatenate([us[gi, hh, 0], us[gi, hh, 1]], axis=1)
                    end2 = jnp.concatenate([ends[0][gi][:, halves[hh]], ends[1][gi][:, halves[hh]]], axis=1)
                    dec_ref[hh, c] = jnp.broadcast_to(jnp.exp(end2), (8, LANES))
            return carry

        lax.fori_loop(0, nch // grp, body, 0)

    def scan_states(base, nch, state):
        def body(t, st):
            cf, cb = base + t, base + nch - 1 - t
            new = []
            for hh in range(2):
                u = jnp.where(first, u_ref[hh, cf], u_ref[hh, cb])
                dec = jnp.where(first[0:1], dec_ref[hh, cf][0:1], dec_ref[hh, cb][0:1])
                u_ref[hh, cf, :, 0:HEAD_DIM] = st[hh][:, 0:HEAD_DIM]
                u_ref[hh, cb, :, HEAD_DIM:] = st[hh][:, HEAD_DIM:]
                new.append(st[hh] * dec + u)
            return tuple(new)

        return lax.fori_loop(0, nch, body, state)

    def emit(q_ref, k_ref, v_ref, gate_ref, o_ref, base):
        nch = q_ref.shape[1] // CHUNK
        grp = _group(nch, GLA_GROUP)

        def body(t, carry):
            r0 = slab_of(t, grp)
            slab = pl.ds(r0, grp * CHUNK)
            q2 = q_ref[0, slab, :] * (HEAD_DIM ** -0.5)
            k2 = k_ref[0, slab, :]
            vb = v_ref[0, slab, :].astype(BF16)
            qd, kd = [], []
            for d in range(2):
                bc = bc_ref[d, pl.ds(pl.multiple_of(base * CHUNK + r0, CHUNK), grp * CHUNK), :]
                qd.append((q2 * jnp.exp(bc)).astype(BF16))
                kd.append((k2 * jnp.exp(-bc)).astype(BF16))
            units = [(gi, hh) for gi in range(grp) for hh in range(2)]
            rs = lambda gi: slice(gi * CHUNK, (gi + 1) * CHUNK)
            att = {}
            for gi, hh in units:
                a_f = jnp.where(mask_f, _dot_nt(qd[0][rs(gi), halves[hh]], kd[0][rs(gi), halves[hh]]), 0.0)
                a_b = jnp.where(mask_b, _dot_nt(qd[1][rs(gi), halves[hh]], kd[1][rs(gi), halves[hh]]), 0.0)
                att[gi, hh] = (a_f + a_b).astype(BF16)
            outs = []
            for gi in range(grp):
                c = base + t * grp + gi
                parts = []
                for hh in range(2):
                    qcat = jnp.concatenate([qd[0][rs(gi), halves[hh]], qd[1][rs(gi), halves[hh]]], axis=1)
                    parts.append(_dot(att[gi, hh], vb[rs(gi), halves[hh]])
                                 + _dot_nt(qcat, u_ref[hh, c].astype(BF16)))
                outs.append(jnp.concatenate(parts, axis=1))
            o2 = jnp.concatenate(outs, axis=0)
            o_ref[0, slab, :] = _head_norm_gate(o2, g_ref[...], silu(gate_ref[0, slab, :]))
            return carry

        lax.fori_loop(0, nch // grp, body, 0)

    summarize(kc_ref, vc_ref, sc_ref, 0)
    summarize(kl_ref, vl_ref, sl_ref, ncc)
    state = scan_states(0, ncc, (jnp.zeros((HEAD_DIM, LANES), F32),) * 2)
    scan_states(ncc, ncl, state)
    if with_ctx_out:
        emit(qc_ref, kc_ref, vc_ref, oc_gate_ref, o_ctx_ref, 0)
    emit(ql_ref, kl_ref, vl_ref, ol_gate_ref, o_lat_ref, ncc)


def _gla_call(gc, sc, gl, sl, wa2, ba, g, with_ctx_out):
    b, n, _ = gl.shape
    nc = gc.shape[1]
    tot = (nc + n) // CHUNK
    small = lambda nn: pl.BlockSpec((1, nn, SMALL_COLS), lambda bi, hp: (bi, 0, 0))
    in_specs = (_seq_specs(nc) + [small(nc)] + _seq_specs(n) + [small(n)]
                + [pl.BlockSpec((2, GLA_RANK, LANES), lambda bi, hp: (0, 0, hp)),
                   pl.BlockSpec((2, 1, LANES), lambda bi, hp: (0, 0, hp)),
                   pl.BlockSpec((1, LANES), lambda bi, hp: (0, hp))])
    out_specs = [pl.BlockSpec((1, n, LANES), lambda bi, hp: (bi, 0, hp))]
    out_shape = [jax.ShapeDtypeStruct((b, n, G_HEADS * HEAD_DIM), F32)]
    if with_ctx_out:
        out_specs.append(pl.BlockSpec((1, nc, LANES), lambda bi, hp: (bi, 0, hp)))
        out_shape.append(jax.ShapeDtypeStruct((b, nc, G_HEADS * HEAD_DIM), F32))
    scratch = [pltpu.VMEM((2, nc + n, LANES), F32),
               pltpu.VMEM((2, tot, HEAD_DIM, LANES), F32),
               pltpu.VMEM((2, tot, 8, LANES), F32)]
    outs = pl.pallas_call(
        functools.partial(_gla_kernel, with_ctx_out=with_ctx_out),
        grid=(b, G_HEADS // 2),
        in_specs=in_specs,
        out_specs=out_specs,
        out_shape=out_shape,
        scratch_shapes=scratch,
        compiler_params=_cparams(("parallel", "arbitrary")),
        name="gla",
    )(gc, gc, gc, gc, sc, gl, gl, gl, gl, sl, wa2, ba.reshape(2, 1, -1), g.reshape(1, -1))
    return (outs[0], outs[1]) if with_ctx_out else (outs[0], None)


def _outmlp_kernel(*refs, final, ff_tile):
    if final:
        x_ref, oa_ref, om_ref, og_ref, mod_ref, g_ref, wo_ref, w1_ref, w2_ref, fg_ref, y_ref = refs
    else:
        x_ref, oa_ref, om_ref, og_ref, mod_ref, g_ref, wo_ref, w1_ref, w2_ref, y_ref = refs
    mod = mod_ref[0]
    a_w, m_w = oa_ref.shape[2], om_ref.shape[2]
    o = (_dot(oa_ref[0].astype(BF16), wo_ref[0:a_w, :])
         + _dot(om_ref[0].astype(BF16), wo_ref[a_w:a_w + m_w, :])
         + _dot(og_ref[0].astype(BF16), wo_ref[a_w + m_w:, :]))
    x1 = x_ref[0] + mod[2:3] * o
    hb = _norm_mod(x1, g_ref[...], mod[3:4], mod[4:5]).astype(BF16)
    d_ff = w1_ref.shape[1]
    acc = jnp.zeros(x1.shape, F32)
    for j in range(d_ff // ff_tile):
        t = jnp.maximum(_dot(hb, w1_ref[:, j * ff_tile:(j + 1) * ff_tile]), 0.0)
        acc = acc + _dot((t * t).astype(BF16), w2_ref[j * ff_tile:(j + 1) * ff_tile, :])
    x2 = x1 + mod[5:6] * acc
    if final:
        ms = jnp.mean(x2 * x2, axis=-1, keepdims=True)
        x2 = x2 * lax.rsqrt(ms + EPS) * fg_ref[...]
    y_ref[0] = x2


def _outmlp_call(x, oa, om, og, mod, g, wo, w1, w2, final_g):
    bx, n, d = x.shape
    tm = _pick_tile(n, 256)
    d_ff = w1.shape[1]
    row = lambda wd: pl.BlockSpec((1, tm, wd), lambda b, i: (b, i, 0))
    whole = lambda arr: pl.BlockSpec(arr.shape, lambda b, i: (0,) * arr.ndim)
    g2 = g.reshape(1, d)
    in_specs = [row(d), row(oa.shape[2]), row(om.shape[2]), row(og.shape[2]),
                pl.BlockSpec((1, 6, d), lambda b, i: (b, 0, 0)), whole(g2), whole(wo), whole(w1), whole(w2)]
    args = [x, oa, om, og, mod, g2, wo, w1, w2]
    if final_g is not None:
        fg = final_g.reshape(1, d)
        in_specs.append(whole(fg))
        args.append(fg)
    return pl.pallas_call(
        functools.partial(_outmlp_kernel, final=final_g is not None, ff_tile=_pick_tile(d_ff, 1024)),
        grid=(bx, n // tm),
        in_specs=in_specs,
        out_specs=row(d),
        out_shape=jax.ShapeDtypeStruct((bx, n, d), F32),
        compiler_params=_cparams(("parallel", "parallel")),
        name="outproj_mlp",
    )(*args)


def _rope_tables(n):
    t = jnp.arange(n)
    inv = ROPE_BASE ** (-jnp.arange(0, ROPE_AXIS_DIM, 2, dtype=F32) / ROPE_AXIS_DIM)
    ang_r = (t // GRID_W).astype(F32)[:, None] * inv[None, :]
    ang_c = (t % GRID_W).astype(F32)[:, None] * inv[None, :]
    half = ROPE_AXIS_DIM // 2
    cos = jnp.concatenate([jnp.cos(ang_r)] * 2 + [jnp.cos(ang_c)] * 2, axis=1)
    zero = jnp.zeros((n, half), F32)
    sin_a = jnp.concatenate([zero, jnp.sin(ang_r), zero, jnp.sin(ang_c)], axis=1)
    sin_b = jnp.concatenate([-jnp.sin(ang_r), zero, -jnp.sin(ang_c), zero], axis=1)
    return tuple(jnp.tile(a, (1, LANES // HEAD_DIM)).astype(F32) for a in (cos, sin_a, sin_b))


def _pair_heads(a, axis):
    shape = a.shape
    a = a.reshape(shape[:axis] + (ATTN_KV_HEADS, ATTN_GROUP, HEAD_DIM) + shape[axis + 1:])
    return jnp.swapaxes(a, axis, axis + 1).reshape(shape)


def _permute_w_in(w_in):
    aq, ak, av, mq, mk, mv, mo, mi, mf, gq, gk, gv, gg, glr = jnp.split(
        w_in, np.cumsum([Q_COLS, 128, 128, 256, 256, 256, 256, 8, 8, 256, 256, 256, 256])[:13].tolist(), axis=-1)
    pad = jnp.zeros(w_in.shape[:-1] + (SMALL_COLS - 16 - 2 * GLA_RANK,), w_in.dtype)
    return jnp.concatenate([_pair_heads(aq, aq.ndim - 1), ak, av, mq, mk, mv, mo, gq, gk, gv, gg, mi, mf, glr, pad],
                           axis=-1).astype(BF16)


def _permute_w_out(w_out):
    return jnp.concatenate([_pair_heads(w_out[:, :Q_COLS], 1), w_out[:, Q_COLS:]], axis=1).astype(BF16)


def kernel(x, c, ctx, c_ctx, w_ada, b_ada, norm1_g, norm2_g, w_in, attn_sink, m_i_bias, m_f_bias, m_norm_g,
           g_wa2, g_ba, g_norm_g, w_out, w_mlp1, w_mlp2, final_g):
    B, N, D = x.shape
    Nc = ctx.shape[1]
    depth = w_ada.shape[0]
    rope = _rope_tables(N)
    rows = -(-(B + 1) // 8) * 8
    cc = jnp.zeros((rows, D), F32).at[:B].set(c).at[B].set(c_ctx)
    mods = _ada_call(cc, w_ada, b_ada)
    mods_x = mods[:, :B].reshape(depth, B, 6, D)
    mods_c = mods[:, B:B + 1].reshape(depth, 1, 6, D)
    w_in_p = _permute_w_in(w_in)
    wo, w1, w2 = _permute_w_out(w_out), w_mlp1.astype(BF16), w_mlp2.astype(BF16)

    xc = ctx
    for l in range(depth):
        last = l == depth - 1
        q, kv, ml, gl, sl = _inproj_call(x, mods_x[l], norm1_g[l], w_in_p[l], rope)
        ctx_parts = _inproj_call(xc.reshape(1, B * Nc, D), mods_c[l], norm1_g[l], w_in_p[l], None)
        qc, kvc, mc, gc, sc = (t.reshape(B, Nc, -1) for t in ctx_parts)
        oa = _attn_call(attn_sink[l], q, kv, kvc)
        om, omc = _mlstm_call(mc, sc, ml, sl, m_i_bias[l], m_f_bias[l], m_norm_g[l], not last)
        og, ogc = _gla_call(gc, sc, gl, sl, g_wa2[l], g_ba[l], g_norm_g[l], not last)
        x = _outmlp_call(x, oa, om, og, mods_x[l], norm2_g[l], wo[l], w1[l], w2[l], final_g if last else None)
        if not last:
            oac = _attn_call(attn_sink[l], qc, None, kvc)
            flat = lambda t: t.reshape(1, B * Nc, -1)
            xc = _outmlp_call(flat(xc), flat(oac), flat(omc), flat(ogc), mods_c[l], norm2_g[l],
                              wo[l], w1[l], w2[l], None).reshape(B, Nc, D)
    return x
```

```python
import functools

import numpy as np
import jax
import jax.numpy as jnp
from jax import lax
from jax.experimental import pallas as pl
from jax.experimental.pallas import tpu as pltpu

F32 = jnp.float32
BF16 = jnp.bfloat16

HEAD_DIM = 64
GRID_W = 64
ATTN_HEADS = 8
ATTN_KV_HEADS = 2
ATTN_GROUP = ATTN_HEADS // ATTN_KV_HEADS
WINDOW = 128
ATTN_BLOCK = 128
ATTN_SUBBLOCKS = 4
ROPE_BASE = 10000.0
ROPE_AXIS_DIM = HEAD_DIM // 2
M_HEADS = 4
G_HEADS = 4
CHUNK = 64
GLA_GROUP = 16
M_CHUNK = 128
M_GROUP = 8
M_AUG = HEAD_DIM + 8
KT_CHUNKS = 128 // 8
GLA_RANK = 16
GLA_TAU = 16.0
EPS = 1e-6
NEG_BIG = -1e30
LOG2E = 1.4426950408889634

LANES = 128
Q_COLS = ATTN_HEADS * HEAD_DIM
KV_COLS = 2 * ATTN_KV_HEADS * HEAD_DIM
MIX_COLS = 4 * M_HEADS * HEAD_DIM
SMALL_COLS = LANES
N_MGATES = 2 * 2 * 2
VMEM_LIMIT = 56 * 1024 * 1024


def _cparams(sem):
    return pltpu.CompilerParams(dimension_semantics=sem, vmem_limit_bytes=VMEM_LIMIT)


def _pick_tile(n, pref):
    t = pref
    while n % t:
        t //= 2
    return t


def _split3(a):
    a1 = a.astype(BF16)
    r1 = a - a1.astype(F32)
    a2 = r1.astype(BF16)
    a3 = (r1 - a2.astype(F32)).astype(BF16)
    return a1, a2, a3


def _dot(a, b):
    return jnp.dot(a, b, preferred_element_type=F32)


def _dot_nt(a, b):
    return lax.dot_general(a, b, (((1,), (1,)), ((), ())), preferred_element_type=F32)


def _dot_hi(a, b):
    a1, a2, _ = _split3(a)
    b1, b2, _ = _split3(b)
    return _dot(a1, b1) + (_dot(a1, b2) + _dot(a2, b1))


def _log_sigmoid(x):
    return jnp.minimum(x, 0.0) - jnp.log1p(jnp.exp(-jnp.abs(x)))


def _ada_kernel(cc_ref, w_ref, b_ref, o_ref):
    cc = cc_ref[...]
    s = cc * jax.nn.sigmoid(cc)
    o_ref[0] = _dot_hi(s, w_ref[0]) + b_ref[0]


def _ada_call(cc, w_ada, b_ada):
    depth, d, six_d = w_ada.shape
    rows = cc.shape[0]
    tn = _pick_tile(six_d, 1024)
    return pl.pallas_call(
        _ada_kernel,
        grid=(depth, six_d // tn),
        in_specs=[pl.BlockSpec((rows, d), lambda l, j: (0, 0)),
                  pl.BlockSpec((1, d, tn), lambda l, j: (l, 0, j)),
                  pl.BlockSpec((1, 1, tn), lambda l, j: (l, 0, j))],
        out_specs=pl.BlockSpec((1, rows, tn), lambda l, j: (l, 0, j)),
        out_shape=jax.ShapeDtypeStruct((depth, rows, six_d), F32),
        compiler_params=_cparams(("arbitrary", "arbitrary")),
        name="ada_mod",
    )(cc, w_ada, b_ada.reshape(depth, 1, six_d))


def _norm_mod(x, g, shift, scale):
    ms = jnp.mean(x * x, axis=-1, keepdims=True)
    h = x * lax.rsqrt(ms + EPS) * g
    return h * (1.0 + scale) + shift


def _inproj_kernel(*refs, use_rope):
    if use_rope:
        x_ref, mod_ref, g_ref, w_ref, cos_ref, sa_ref, sb_ref, q_ref, kv_ref, m_ref, gl_ref, s_ref = refs
    else:
        x_ref, mod_ref, g_ref, w_ref, q_ref, kv_ref, m_ref, gl_ref, s_ref = refs
    mod = mod_ref[0]
    hb = _norm_mod(x_ref[0], g_ref[...], mod[0:1], mod[1:2]).astype(BF16)

    def proj(lo, width):
        return _dot(hb, w_ref[:, lo:lo + width])

    pa = proj(0, Q_COLS + KV_COLS)
    if use_rope:
        cos, sa, sb = cos_ref[...], sa_ref[...], sb_ref[...]
        segs = []
        for j in range((Q_COLS + KV_COLS // 2) // LANES):
            seg = pa[:, j * LANES:(j + 1) * LANES]
            segs.append(seg * cos + pltpu.roll(seg, ROPE_AXIS_DIM // 2, 1) * sa
                        + pltpu.roll(seg, LANES - ROPE_AXIS_DIM // 2, 1) * sb)
        qk = jnp.concatenate(segs, axis=1)
    else:
        qk = pa[:, :Q_COLS + KV_COLS // 2]
    q_ref[0] = qk[:, :Q_COLS] * (HEAD_DIM ** -0.5 * LOG2E)
    kv_ref[0, :, :KV_COLS // 2] = qk[:, Q_COLS:]
    kv_ref[0, :, KV_COLS // 2:] = pa[:, Q_COLS + KV_COLS // 2:]
    lo = Q_COLS + KV_COLS
    m_ref[0] = proj(lo, MIX_COLS)
    gl_ref[0] = proj(lo + MIX_COLS, MIX_COLS)
    s_ref[0] = proj(lo + 2 * MIX_COLS, SMALL_COLS)


def _inproj_call(x, mod, g, w, rope):
    bx, n, d = x.shape
    tm = _pick_tile(n, 512)
    cols = w.shape[1]
    in_specs = [pl.BlockSpec((1, tm, d), lambda b, i: (b, i, 0)),
                pl.BlockSpec((1, 6, d), lambda b, i: (b, 0, 0)),
                pl.BlockSpec((1, d), lambda b, i: (0, 0)),
                pl.BlockSpec((d, cols), lambda b, i: (0, 0))]
    args = [x, mod, g.reshape(1, d), w]
    if rope is not None:
        in_specs += [pl.BlockSpec((tm, LANES), lambda b, i: (i, 0))] * 3
        args += list(rope)
    widths = (Q_COLS, KV_COLS, MIX_COLS, MIX_COLS, SMALL_COLS)
    return pl.pallas_call(
        functools.partial(_inproj_kernel, use_rope=rope is not None),
        grid=(bx, n // tm),
        in_specs=in_specs,
        out_specs=[pl.BlockSpec((1, tm, wd), lambda b, i: (b, i, 0)) for wd in widths],
        out_shape=[jax.ShapeDtypeStruct((bx, n, wd), F32) for wd in widths],
        compiler_params=_cparams(("parallel", "parallel")),
        name="inproj_rope" if rope is not None else "inproj",
    )(*args)


def _attn_kernel(*refs, local, tq, nsub):
    if local:
        sink_ref, q_ref, kv_ref, kvc_ref, o_ref, km_ref, vt_ref, bias_ref, kmc_ref, vtc_ref = refs
    else:
        sink_ref, q_ref, kvc_ref, o_ref, kmc_ref, vtc_ref = refs
    npairs = ATTN_GROUP
    cols = npairs * tq
    span = 3 * ATTN_BLOCK
    lane = lax.broadcasted_iota(jnp.int32, (1, LANES), 1)
    head_lanes = (lane < HEAD_DIM, lane >= HEAD_DIM)
    row8 = lax.broadcasted_iota(jnp.int32, (8, LANES), 0)
    ones8 = jnp.where(row8 == 0, 1.0, 0.0).astype(F32)

    def prepare(src_ref, km, vt):
        ntiles = src_ref.shape[1] // LANES
        tiles = _group(ntiles, 4)

        def body(i, carry):
            rows = [pl.ds(pl.multiple_of((i * tiles + u) * LANES, LANES), LANES) for u in range(tiles)]
            k128 = [src_ref[0, r, 0:LANES] for r in rows]
            v_t = [src_ref[0, r, LANES:2 * LANES].T for r in rows]
            for u, r in enumerate(rows):
                for kvh in range(ATTN_KV_HEADS):
                    km[kvh, r, :] = jnp.where(head_lanes[kvh], k128[u], 0.0).astype(BF16)
                    vt[kvh, 0:HEAD_DIM, r] = v_t[u][kvh * HEAD_DIM:(kvh + 1) * HEAD_DIM]
                    vt[kvh, HEAD_DIM:M_AUG, r] = ones8
            return carry

        lax.fori_loop(0, ntiles // tiles, body, 0)

    @pl.when(pl.program_id(1) == 0)
    def _():
        prepare(kvc_ref, kmc_ref, vtc_ref)
        if local:
            prepare(kv_ref, km_ref, vt_ref)
            rel = lax.broadcasted_iota(jnp.int32, bias_ref.shape, 0) - 2 * ATTN_BLOCK
            qoff = lax.broadcasted_iota(jnp.int32, bias_ref.shape, 1) & (tq - 1)
            bias_ref[...] = jnp.where(jnp.abs(rel - qoff) <= WINDOW, 0.0, NEG_BIG)

    units = [(sb, kvh) for sb in range(nsub) for kvh in range(ATTN_KV_HEADS)]
    qall, start, band = {}, {}, {}
    for sb in range(nsub):
        q = q_ref[0, sb * tq:(sb + 1) * tq, :].astype(BF16)
        qall[sb] = jnp.concatenate([q[:, p * LANES:(p + 1) * LANES] for p in range(npairs)], axis=0)
        if local:
            j = pl.program_id(1) * nsub + sb
            start[sb] = pl.multiple_of(jnp.clip((j - 1) * ATTN_BLOCK, 0, kv_ref.shape[1] - span), ATTN_BLOCK)
            band[sb] = bias_ref[pl.ds(pl.multiple_of(start[sb] - j * ATTN_BLOCK + 2 * ATTN_BLOCK, ATTN_BLOCK),
                                      span), :]
    sink = [jnp.concatenate([jnp.full((1, tq), sink_ref[kvh * ATTN_GROUP + p] * LOG2E, F32)
                             for p in range(npairs)], axis=1) for kvh in range(ATTN_KV_HEADS)]
    s_ctx, s_loc, m, acc = {}, {}, {}, {}
    for sb, kvh in units:
        s_ctx[sb, kvh] = _dot_nt(kmc_ref[kvh], qall[sb])
        if local:
            s_loc[sb, kvh] = _dot_nt(km_ref[kvh, pl.ds(start[sb], span), :], qall[sb]) + band[sb]
    for u in units:
        m[u] = jnp.maximum(jnp.max(s_ctx[u], axis=0, keepdims=True), sink[u[1]])
        if local:
            m[u] = jnp.maximum(m[u], jnp.max(s_loc[u], axis=0, keepdims=True))
    for sb, kvh in units:
        u = (sb, kvh)
        acc[u] = _dot(vtc_ref[kvh].astype(BF16), jnp.exp2(s_ctx[u] - m[u]).astype(BF16))
        if local:
            acc[u] = acc[u] + _dot(vt_ref[kvh, :, pl.ds(start[sb], span)].astype(BF16),
                                   jnp.exp2(s_loc[u] - m[u]).astype(BF16))
    o_t = {u: acc[u][0:HEAD_DIM] / (acc[u][HEAD_DIM:HEAD_DIM + 1] + jnp.exp2(sink[u[1]] - m[u])) for u in units}
    for sb in range(nsub):
        for p in range(npairs):
            for c in range(tq // LANES):
                sl = slice(p * tq + c * LANES, p * tq + (c + 1) * LANES)
                tile = jnp.concatenate([o_t[sb, kvh][:, sl] for kvh in range(ATTN_KV_HEADS)], axis=0)
                r0 = sb * tq + c * LANES
                o_ref[0, r0:r0 + LANES, p * LANES:(p + 1) * LANES] = tile.T


def _attn_call(sink, q, kv, kvc):
    b, nq, _ = q.shape
    nc = kvc.shape[1]
    local = kv is not None
    tq = ATTN_BLOCK if local else nq
    nsub = ATTN_SUBBLOCKS if local else 1
    assert tq % LANES == 0 and nc % LANES == 0 and tq & (tq - 1) == 0 and nq % (nsub * tq) == 0
    in_specs = [pl.BlockSpec(memory_space=pltpu.SMEM),
                pl.BlockSpec((1, nsub * tq, Q_COLS), lambda bi, j: (bi, j, 0))]
    args = [sink, q]
    scratch = []
    if local:
        in_specs.append(pl.BlockSpec((1, nq, KV_COLS), lambda bi, j: (bi, 0, 0)))
        args.append(kv)
        scratch += [pltpu.VMEM((ATTN_KV_HEADS, nq, LANES), BF16), pltpu.VMEM((ATTN_KV_HEADS, M_AUG, nq), F32),
                    pltpu.VMEM((5 * ATTN_BLOCK, ATTN_GROUP * tq), F32)]
    in_specs.append(pl.BlockSpec((1, nc, KV_COLS), lambda bi, j: (bi, 0, 0)))
    args.append(kvc)
    scratch += [pltpu.VMEM((ATTN_KV_HEADS, nc, LANES), BF16), pltpu.VMEM((ATTN_KV_HEADS, M_AUG, nc), F32)]
    return pl.pallas_call(
        functools.partial(_attn_kernel, local=local, tq=tq, nsub=nsub),
        grid=(b, nq // (nsub * tq)),
        in_specs=in_specs,
        out_specs=pl.BlockSpec((1, nsub * tq, Q_COLS), lambda bi, j: (bi, j, 0)),
        out_shape=jax.ShapeDtypeStruct((b, nq, Q_COLS), F32),
        scratch_shapes=scratch,
        compiler_params=_cparams(("arbitrary", "arbitrary")),
        name="attn_window" if local else "attn_ctx",
    )(*args)


def _tri(n, rev):
    ri = lax.broadcasted_iota(jnp.int32, (n, n), 0)
    ci = lax.broadcasted_iota(jnp.int32, (n, n), 1)
    return ci >= ri if rev else ci <= ri


def _seq_specs(n):
    return [pl.BlockSpec((1, n, LANES), lambda b, hp, off=off: (b, 0, off + hp)) for off in (0, 2, 4, 6)]


def _group(nchunks, pref):
    g = min(pref, nchunks)
    assert nchunks % g == 0
    return g


def _mlstm_kernel(*refs, with_ctx_out):
    (qc_ref, kc_ref, vc_ref, oc_gate_ref, grc_ref,
     ql_ref, kl_ref, vl_ref, ol_gate_ref, grl_ref, bias_ref, g_ref) = refs[:12]
    if with_ctx_out:
        o_lat_ref, o_ctx_ref = refs[12:14]
        scratch = refs[14:]
    else:
        o_lat_ref, o_ctx_ref = refs[12], None
        scratch = refs[13:]
    w_ref, b_ref, ck_ref, tot_ref, mc_ref, mp_ref, kt_ref, cc_ref, vt_ref = scratch
    L = M_CHUNK
    ncc, ncl = qc_ref.shape[1] // L, ql_ref.shape[1] // L
    kt_c = -(-ncc // KT_CHUNKS)

    mask_t = (_tri(L, True), _tri(L, False))
    tri_f, tri_b = (mk.astype(BF16) for mk in mask_t)
    ones_m = jnp.ones((L, L), BF16)
    row8 = lax.broadcasted_iota(jnp.int32, (8, L), 0)
    rev8 = (row8 & 2) != 0
    ones8 = jnp.where(row8 == 0, 1.0, 0.0).astype(F32)

    def prep(gr_ref, v_ref, base, kt_base, v_base):
        nch = gr_ref.shape[3]
        li = (gr_ref[0, 0, 0] + bias_ref[0, 0]).reshape(nch * 8, L)
        lf = _log_sigmoid(gr_ref[0, 0, 1] + bias_ref[0, 1]).reshape(nch * 8, L)
        rev = (lax.broadcasted_iota(jnp.int32, li.shape, 0) & 2) != 0
        lane = lax.broadcasted_iota(jnp.int32, li.shape, 1)
        l1, l2, l3 = _split3(lf)
        scan = lambda m: _dot(l1, m) + (_dot(l2, m) + _dot(l3, m))
        b = jnp.where(rev, scan(tri_b), scan(tri_f))
        tot = scan(ones_m)
        g = tot - b + li
        mc = jnp.max(g, axis=1, keepdims=True)
        kap = li - b
        ckf, ckb = kap, kap
        s = 1
        while s < L:
            ckf = jnp.maximum(ckf, jnp.where(lane >= s, pltpu.roll(ckf, s, 1), NEG_BIG))
            ckb = jnp.maximum(ckb, jnp.where(lane < L - s, pltpu.roll(ckb, L - s, 1), NEG_BIG))
            s *= 2
        sl = pl.ds(base, nch)
        w_ref[sl] = jnp.exp(g - mc).reshape(nch, 8, L)
        b_ref[sl] = b.reshape(nch, 8, L)
        ck_ref[sl] = jnp.where(rev, ckb, ckf).reshape(nch, 8, L)
        tot_ref[sl] = tot.reshape(nch, 8, L)
        mc_ref[sl] = jnp.broadcast_to(mc, (nch * 8, L)).reshape(nch, 8, L)
        for gi in range(-(-nch // KT_CHUNKS)):
            blk = kap[gi * LANES:(gi + 1) * LANES]
            if blk.shape[0] < LANES:
                blk = jnp.concatenate([blk, jnp.zeros((LANES - blk.shape[0], L), F32)], axis=0)
            kt_ref[kt_base + gi] = blk.T

        tiles = _group(nch, 4)

        def vt_body(i, carry):
            blocks = [v_ref[0, pl.ds(pl.multiple_of((i * tiles + u) * L, L), L), :].T for u in range(tiles)]
            for u, blk in enumerate(blocks):
                vt_ref[:, pl.ds(pl.multiple_of(v_base + (i * tiles + u) * L, L), L)] = blk
            return carry

        lax.fori_loop(0, nch // tiles, vt_body, 0)

    def vaug(vt2, hh):
        return jnp.concatenate([vt2[hh * HEAD_DIM:(hh + 1) * HEAD_DIM], ones8], axis=0)

    def summarize(k_ref, base, v_base):
        nch = k_ref.shape[1] // L
        grp = _group(nch, M_GROUP)

        def body(t, carry):
            lhs, rhs = {}, {}
            for gi in range(grp):
                cl = t * grp + gi
                k2 = (k_ref[0, pl.ds(pl.multiple_of(cl * L, L), L), :] * (HEAD_DIM ** -0.5)).astype(BF16)
                vt2 = vt_ref[:, pl.ds(pl.multiple_of(v_base + cl * L, L), L)]
                w8 = w_ref[base + cl]
                for hh in range(2):
                    va = vaug(vt2, hh)
                    lhs[gi, hh] = jnp.concatenate([va * w8[d * 2 + hh:d * 2 + hh + 1] for d in range(2)],
                                                  axis=0).astype(BF16)
                    rhs[gi, hh] = k2[:, hh * HEAD_DIM:(hh + 1) * HEAD_DIM]
            cc = {key: _dot(lhs[key], rhs[key]) for key in lhs}
            for (gi, hh), val in cc.items():
                for d in range(2):
                    cc_ref[base + t * grp + gi, d * 2 + hh] = val[d * M_AUG:(d + 1) * M_AUG]
            return carry

        lax.fori_loop(0, nch // grp, body, 0)

    def scan_states(base, nch, state):
        def body(t, st):
            m8, cs = st[0], list(st[1:])
            cf, cb = base + t, base + nch - 1 - t
            tot8 = jnp.where(rev8, tot_ref[cb], tot_ref[cf])
            mc8 = jnp.where(rev8, mc_ref[cb], mc_ref[cf])
            m_new = jnp.maximum(tot8 + m8, mc8)
            s_old = jnp.exp(tot8 + m8 - m_new)
            s_new = jnp.exp(mc8 - m_new)
            mp_ref[cf, 0:2, :] = m8[0:2]
            mp_ref[cb, 2:4, :] = m8[2:4]
            for j in range(4):
                c = cf if j < 2 else cb
                summary = cc_ref[c, j]
                cc_ref[c, j] = cs[j]
                cs[j] = s_old[j:j + 1, 0:HEAD_DIM] * cs[j] + s_new[j:j + 1, 0:HEAD_DIM] * summary
            return (m_new, *cs)

        return lax.fori_loop(0, nch, body, state)

    def emit(q_ref, k_ref, gate_ref, o_ref, base, kt_base, v_base):
        nch = q_ref.shape[1] // L
        grp = _group(nch, M_GROUP)

        def body(t, carry):
            units = [(gi, hh) for gi in range(grp) for hh in range(2)]
            rows, vas, rho, alpha, floor, kcol, lhs, rhs = {}, {}, {}, {}, {}, {}, {}, {}
            for gi in range(grp):
                cl = t * grp + gi
                c = base + cl
                rows[gi] = pl.ds(pl.multiple_of(cl * L, L), L)
                q2 = q_ref[0, rows[gi], :].astype(BF16)
                k2 = k_ref[0, rows[gi], :] * (HEAD_DIM ** -0.5)
                vt2 = vt_ref[:, pl.ds(pl.multiple_of(v_base + cl * L, L), L)]
                mp8 = mp_ref[c]
                rho[gi] = -jnp.maximum(mp8, ck_ref[c])
                alpha[gi] = jnp.exp(mp8 + rho[gi])
                floor[gi] = jnp.exp(rho[gi] - b_ref[c])
                kt = kt_ref[kt_base + (cl >> 4)]
                kcol[gi] = pltpu.roll(kt, (LANES - (cl & (KT_CHUNKS - 1)) * 8) & (LANES - 1), 1)
                for hh in range(2):
                    ln = slice(hh * HEAD_DIM, (hh + 1) * HEAD_DIM)
                    vas[gi, hh] = vaug(vt2, hh).astype(BF16)
                    lhs[gi, hh] = jnp.concatenate([k2[:, ln], cc_ref[c, hh], cc_ref[c, 2 + hh]], axis=0).astype(BF16)
                    rhs[gi, hh] = q2[:, ln]
            prod = {u: _dot_nt(lhs[u], rhs[u]) for u in units}
            pt = {}
            for gi, hh in units:
                st = prod[gi, hh][0:L]
                for d in range(2):
                    j = d * 2 + hh
                    e = kcol[gi][:, j:j + 1] + rho[gi][j:j + 1, :]
                    pt[gi, hh, d] = (jnp.where(mask_t[d], jnp.exp(e), 0.0) * st).astype(BF16)
            pv = {key: _dot(vas[key[0], key[1]], val) for key, val in pt.items()}
            for gi in range(grp):
                halves = []
                for hh in range(2):
                    hsum = None
                    for d in range(2):
                        j = d * 2 + hh
                        lo = L + d * M_AUG
                        num = alpha[gi][j:j + 1, :] * prod[gi, hh][lo:lo + M_AUG] + pv[gi, hh, d]
                        h = num[0:HEAD_DIM] / jnp.maximum(jnp.abs(num[HEAD_DIM:HEAD_DIM + 1]), floor[gi][j:j + 1, :])
                        hsum = h if hsum is None else hsum + h
                    ms = jnp.sum(hsum * hsum, axis=0, keepdims=True) * (1.0 / HEAD_DIM)
                    halves.append(hsum * lax.rsqrt(ms + EPS))
                y = jnp.concatenate(halves, axis=0).T
                o_ref[0, rows[gi], :] = y * g_ref[...] * jax.nn.sigmoid(gate_ref[0, rows[gi], :])
            return carry

        lax.fori_loop(0, nch // grp, body, 0)

    n_ctx = qc_ref.shape[1]
    prep(grc_ref, vc_ref, 0, 0, 0)
    prep(grl_ref, vl_ref, ncc, kt_c, n_ctx)
    summarize(kc_ref, 0, 0)
    summarize(kl_ref, ncc, n_ctx)
    zero = (jnp.zeros((8, L), F32),) + (jnp.zeros((M_AUG, HEAD_DIM), F32),) * 4
    state = scan_states(0, ncc, zero)
    scan_states(ncc, ncl, state)
    if with_ctx_out:
        emit(qc_ref, kc_ref, oc_gate_ref, o_ctx_ref, 0, 0, 0)
    emit(ql_ref, kl_ref, ol_gate_ref, o_lat_ref, ncc, kt_c, n_ctx)


def _mlstm_gates(s):
    b, n, _ = s.shape
    g = s[:, :, :2 * N_MGATES].reshape(b, n, 2, 2, 2, 2)
    g = g.transpose(0, 4, 2, 3, 5, 1).reshape(b, 2, 2, 4, n // M_CHUNK, M_CHUNK).transpose(0, 1, 2, 4, 3, 5)
    return jnp.concatenate([g, g], axis=4)


def _mlstm_call(mc, sc, ml, sl, i_bias, f_bias, g, with_ctx_out):
    b, n, _ = ml.shape
    nc = mc.shape[1]
    assert n % M_CHUNK == 0 and nc % M_CHUNK == 0
    ncc, ncl = nc // M_CHUNK, n // M_CHUNK
    bias = jnp.stack([i_bias, f_bias]).astype(F32).reshape(2, 2, 2, 2).transpose(2, 0, 1, 3).reshape(2, 2, 4)
    bias = jnp.concatenate([bias, bias], axis=2)[..., None]
    gr_spec = lambda nch: pl.BlockSpec((1, 1, 2, nch, 8, M_CHUNK), lambda bi, hp: (bi, hp, 0, 0, 0, 0))
    in_specs = (_seq_specs(nc) + [gr_spec(ncc)] + _seq_specs(n) + [gr_spec(ncl)]
                + [pl.BlockSpec((1, 2, 8, 1), lambda bi, hp: (hp, 0, 0, 0)),
                   pl.BlockSpec((1, LANES), lambda bi, hp: (0, hp))])
    out_specs = [pl.BlockSpec((1, n, LANES), lambda bi, hp: (bi, 0, hp))]
    out_shape = [jax.ShapeDtypeStruct((b, n, M_HEADS * HEAD_DIM), F32)]
    if with_ctx_out:
        out_specs.append(pl.BlockSpec((1, nc, LANES), lambda bi, hp: (bi, 0, hp)))
        out_shape.append(jax.ShapeDtypeStruct((b, nc, M_HEADS * HEAD_DIM), F32))
    tot = ncc + ncl
    per_chunk = pltpu.VMEM((tot, 8, M_CHUNK), F32)
    scratch = [per_chunk] * 6 + [
        pltpu.VMEM((-(-ncc // KT_CHUNKS) + -(-ncl // KT_CHUNKS), M_CHUNK, LANES), F32),
        pltpu.VMEM((tot, 4, M_AUG, HEAD_DIM), F32),
        pltpu.VMEM((LANES, nc + n), F32)]
    outs = pl.pallas_call(
        functools.partial(_mlstm_kernel, with_ctx_out=with_ctx_out),
        grid=(b, M_HEADS // 2),
        in_specs=in_specs,
        out_specs=out_specs,
        out_shape=out_shape,
        scratch_shapes=scratch,
        compiler_params=_cparams(("parallel", "arbitrary")),
        name="mlstm",
    )(mc, mc, mc, mc, _mlstm_gates(sc), ml, ml, ml, ml, _mlstm_gates(sl), bias, g.reshape(1, -1))
    return (outs[0], outs[1]) if with_ctx_out else (outs[0], None)


def _head_norm_gate(hsum, g, gate):
    sq = hsum * hsum
    lane = lax.broadcasted_iota(jnp.int32, hsum.shape, 1)
    first = lane < HEAD_DIM
    s0 = jnp.sum(jnp.where(first, sq, 0.0), axis=-1, keepdims=True)
    s1 = jnp.sum(jnp.where(first, 0.0, sq), axis=-1, keepdims=True)
    ms = jnp.where(first, s0, s1) * (1.0 / HEAD_DIM)
    return hsum * lax.rsqrt(ms + EPS) * g * gate


def _seg_scan_sum(x, rev):
    rows = x.shape[0]
    pos = lax.broadcasted_iota(jnp.int32, x.shape, 0) & (CHUNK - 1)
    s = 1
    while s < CHUNK:
        if rev:
            x = x + jnp.where(pos < CHUNK - s, pltpu.roll(x, rows - s, 0), 0.0)
        else:
            x = x + jnp.where(pos >= s, pltpu.roll(x, s, 0), 0.0)
        s *= 2
    return x


def _gla_kernel(*refs, with_ctx_out):
    (qc_ref, kc_ref, vc_ref, oc_gate_ref, sc_ref,
     ql_ref, kl_ref, vl_ref, ol_gate_ref, sl_ref, wa_ref, ba_ref, g_ref) = refs[:13]
    if with_ctx_out:
        o_lat_ref, o_ctx_ref, bc_ref, u_ref, dec_ref = refs[13:]
    else:
        o_lat_ref, bc_ref, u_ref, dec_ref = refs[13:]
        o_ctx_ref = None
    ncc, ncl = qc_ref.shape[1] // CHUNK, ql_ref.shape[1] // CHUNK
    lo = 2 * N_MGATES
    mask_f, mask_b = _tri(CHUNK, False), _tri(CHUNK, True)
    first = lax.broadcasted_iota(jnp.int32, (HEAD_DIM, LANES), 1) < HEAD_DIM
    halves = (slice(0, HEAD_DIM), slice(HEAD_DIM, 2 * HEAD_DIM))
    silu = lambda t: t * jax.nn.sigmoid(t)

    def slab_of(t, grp):
        return pl.multiple_of(t * (grp * CHUNK), grp * CHUNK)

    def summarize(k_ref, v_ref, s_ref, base):
        nch = k_ref.shape[1] // CHUNK
        grp = _group(nch, GLA_GROUP)
        assert grp % 2 == 0

        def body(t, carry):
            r0 = slab_of(t, grp)
            slab = pl.ds(r0, grp * CHUNK)
            k3 = k_ref[0, slab, :].reshape(grp, CHUNK, LANES)
            v2 = v_ref[0, slab, :]
            kws, ends = [], []
            for d in range(2):
                lr = s_ref[0, slab, lo + d * GLA_RANK:lo + (d + 1) * GLA_RANK]
                glog = _log_sigmoid(_dot_hi(lr, wa_ref[d]) + ba_ref[d]) * (1.0 / GLA_TAU)
                bc = _seg_scan_sum(glog, d == 1)
                bc_ref[d, pl.ds(pl.multiple_of(base * CHUNK + r0, CHUNK), grp * CHUNK), :] = bc
                bc3 = bc.reshape(grp, CHUNK, LANES)
                end = bc3[:, 0:1, :] if d == 1 else bc3[:, CHUNK - 1:CHUNK, :]
                kws.append((k3 * jnp.exp(end - bc3)).astype(BF16))
                ends.append(end)
            vts = [v2[p * LANES:(p + 1) * LANES, :].T.astype(BF16) for p in range(grp // 2)]
            us = {}
            for gi in range(grp):
                for hh in range(2):
                    vt = vts[gi // 2][halves[hh], halves[gi % 2]]
                    for d in range(2):
                        us[gi, hh, d] = _dot(vt, kws[d][gi][:, halves[hh]])
            for gi in range(grp):
                c = base + t * grp + gi
                for hh in range(2):
                    u_ref[hh, c] = jnp.concatenate([us[gi, hh, 0], us[gi, hh, 1]], axis=1)
                    end2 = jnp.concatenate([ends[0][gi][:, halves[hh]], ends[1][gi][:, halves[hh]]], axis=1)
                    dec_ref[hh, c] = jnp.broadcast_to(jnp.exp(end2), (8, LANES))
            return carry

        lax.fori_loop(0, nch // grp, body, 0)

    def scan_states(base, nch, state):
        def body(t, st):
            cf, cb = base + t, base + nch - 1 - t
            new = []
            for hh in range(2):
                u = jnp.where(first, u_ref[hh, cf], u_ref[hh, cb])
                dec = jnp.where(first[0:1], dec_ref[hh, cf][0:1], dec_ref[hh, cb][0:1])
                u_ref[hh, cf, :, 0:HEAD_DIM] = st[hh][:, 0:HEAD_DIM]
                u_ref[hh, cb, :, HEAD_DIM:] = st[hh][:, HEAD_DIM:]
                new.append(st[hh] * dec + u)
            return tuple(new)

        return lax.fori_loop(0, nch, body, state)

    def emit(q_ref, k_ref, v_ref, gate_ref, o_ref, base):
        nch = q_ref.shape[1] // CHUNK
        grp = _group(nch, GLA_GROUP)

        def body(t, carry):
            r0 = slab_of(t, grp)
            slab = pl.ds(r0, grp * CHUNK)
            q2 = q_ref[0, slab, :] * (HEAD_DIM ** -0.5)
            k2 = k_ref[0, slab, :]
            vb = v_ref[0, slab, :].astype(BF16)
            qd, kd = [], []
            for d in range(2):
                bc = bc_ref[d, pl.ds(pl.multiple_of(base * CHUNK + r0, CHUNK), grp * CHUNK), :]
                qd.append((q2 * jnp.exp(bc)).astype(BF16))
                kd.append((k2 * jnp.exp(-bc)).astype(BF16))
            units = [(gi, hh) for gi in range(grp) for hh in range(2)]
            rs = lambda gi: slice(gi * CHUNK, (gi + 1) * CHUNK)
            att = {}
            for gi, hh in units:
                a_f = jnp.where(mask_f, _dot_nt(qd[0][rs(gi), halves[hh]], kd[0][rs(gi), halves[hh]]), 0.0)
                a_b = jnp.where(mask_b, _dot_nt(qd[1][rs(gi), halves[hh]], kd[1][rs(gi), halves[hh]]), 0.0)
                att[gi, hh] = (a_f + a_b).astype(BF16)
            outs = []
            for gi in range(grp):
                c = base + t * grp + gi
                parts = []
                for hh in range(2):
                    qcat = jnp.concatenate([qd[0][rs(gi), halves[hh]], qd[1][rs(gi), halves[hh]]], axis=1)
                    parts.append(_dot(att[gi, hh], vb[rs(gi), halves[hh]])
                                 + _dot_nt(qcat, u_ref[hh, c].astype(BF16)))
                outs.append(jnp.concatenate(parts, axis=1))
            o2 = jnp.concatenate(outs, axis=0)
            o_ref[0, slab, :] = _head_norm_gate(o2, g_ref[...], silu(gate_ref[0, slab, :]))
            return carry

        lax.fori_loop(0, nch // grp, body, 0)

    summarize(kc_ref, vc_ref, sc_ref, 0)
    summarize(kl_ref, vl_ref, sl_ref, ncc)
    state = scan_states(0, ncc, (jnp.zeros((HEAD_DIM, LANES), F32),) * 2)
    scan_states(ncc, ncl, state)
    if with_ctx_out:
        emit(qc_ref, kc_ref, vc_ref, oc_gate_ref, o_ctx_ref, 0)
    emit(ql_ref, kl_ref, vl_ref, ol_gate_ref, o_lat_ref, ncc)


def _gla_call(gc, sc, gl, sl, wa2, ba, g, with_ctx_out):
    b, n, _ = gl.shape
    nc = gc.shape[1]
    tot = (nc + n) // CHUNK
    small = lambda nn: pl.BlockSpec((1, nn, SMALL_COLS), lambda bi, hp: (bi, 0, 0))
    in_specs = (_seq_specs(nc) + [small(nc)] + _seq_specs(n) + [small(n)]
                + [pl.BlockSpec((2, GLA_RANK, LANES), lambda bi, hp: (0, 0, hp)),
                   pl.BlockSpec((2, 1, LANES), lambda bi, hp: (0, 0, hp)),
                   pl.BlockSpec((1, LANES), lambda bi, hp: (0, hp))])
    out_specs = [pl.BlockSpec((1, n, LANES), lambda bi, hp: (bi, 0, hp))]
    out_shape = [jax.ShapeDtypeStruct((b, n, G_HEADS * HEAD_DIM), F32)]
    if with_ctx_out:
        out_specs.append(pl.BlockSpec((1, nc, LANES), lambda bi, hp: (bi, 0, hp)))
        out_shape.append(jax.ShapeDtypeStruct((b, nc, G_HEADS * HEAD_DIM), F32))
    scratch = [pltpu.VMEM((2, nc + n, LANES), F32),
               pltpu.VMEM((2, tot, HEAD_DIM, LANES), F32),
               pltpu.VMEM((2, tot, 8, LANES), F32)]
    outs = pl.pallas_call(
        functools.partial(_gla_kernel, with_ctx_out=with_ctx_out),
        grid=(b, G_HEADS // 2),
        in_specs=in_specs,
        out_specs=out_specs,
        out_shape=out_shape,
        scratch_shapes=scratch,
        compiler_params=_cparams(("parallel", "arbitrary")),
        name="gla",
    )(gc, gc, gc, gc, sc, gl, gl, gl, gl, sl, wa2, ba.reshape(2, 1, -1), g.reshape(1, -1))
    return (outs[0], outs[1]) if with_ctx_out else (outs[0], None)


def _outmlp_kernel(*refs, final, ff_tile):
    if final:
        x_ref, oa_ref, om_ref, og_ref, mod_ref, g_ref, wo_ref, w1_ref, w2_ref, fg_ref, y_ref = refs
    else:
        x_ref, oa_ref, om_ref, og_ref, mod_ref, g_ref, wo_ref, w1_ref, w2_ref, y_ref = refs
    mod = mod_ref[0]
    a_w, m_w = oa_ref.shape[2], om_ref.shape[2]
    o = (_dot(oa_ref[0].astype(BF16), wo_ref[0:a_w, :])
         + _dot(om_ref[0].astype(BF16), wo_ref[a_w:a_w + m_w, :])
         + _dot(og_ref[0].astype(BF16), wo_ref[a_w + m_w:, :]))
    x1 = x_ref[0] + mod[2:3] * o
    hb = _norm_mod(x1, g_ref[...], mod[3:4], mod[4:5]).astype(BF16)
    d_ff = w1_ref.shape[1]
    acc = jnp.zeros(x1.shape, F32)
    for j in range(d_ff // ff_tile):
        t = jnp.maximum(_dot(hb, w1_ref[:, j * ff_tile:(j + 1) * ff_tile]), 0.0)
        acc = acc + _dot((t * t).astype(BF16), w2_ref[j * ff_tile:(j + 1) * ff_tile, :])
    x2 = x1 + mod[5:6] * acc
    if final:
        ms = jnp.mean(x2 * x2, axis=-1, keepdims=True)
        x2 = x2 * lax.rsqrt(ms + EPS) * fg_ref[...]
    y_ref[0] = x2


def _outmlp_call(x, oa, om, og, mod, g, wo, w1, w2, final_g):
    bx, n, d = x.shape
    tm = _pick_tile(n, 512)
    d_ff = w1.shape[1]
    row = lambda wd: pl.BlockSpec((1, tm, wd), lambda b, i: (b, i, 0))
    whole = lambda arr: pl.BlockSpec(arr.shape, lambda b, i: (0,) * arr.ndim, pipeline_mode=pl.Buffered(1))
    g2 = g.reshape(1, d)
    in_specs = [row(d), row(oa.shape[2]), row(om.shape[2]), row(og.shape[2]),
                pl.BlockSpec((1, 6, d), lambda b, i: (b, 0, 0)), whole(g2), whole(wo), whole(w1), whole(w2)]
    args = [x, oa, om, og, mod, g2, wo, w1, w2]
    if final_g is not None:
        fg = final_g.reshape(1, d)
        in_specs.append(whole(fg))
        args.append(fg)
    return pl.pallas_call(
        functools.partial(_outmlp_kernel, final=final_g is not None, ff_tile=_pick_tile(d_ff, 1024)),
        grid=(bx, n // tm),
        in_specs=in_specs,
        out_specs=row(d),
        out_shape=jax.ShapeDtypeStruct((bx, n, d), F32),
        compiler_params=_cparams(("parallel", "parallel")),
        name="outproj_mlp",
    )(*args)


def _rope_tables(n):
    t = jnp.arange(n)
    inv = ROPE_BASE ** (-jnp.arange(0, ROPE_AXIS_DIM, 2, dtype=F32) / ROPE_AXIS_DIM)
    ang_r = (t // GRID_W).astype(F32)[:, None] * inv[None, :]
    ang_c = (t % GRID_W).astype(F32)[:, None] * inv[None, :]
    half = ROPE_AXIS_DIM // 2
    cos = jnp.concatenate([jnp.cos(ang_r)] * 2 + [jnp.cos(ang_c)] * 2, axis=1)
    zero = jnp.zeros((n, half), F32)
    sin_a = jnp.concatenate([zero, jnp.sin(ang_r), zero, jnp.sin(ang_c)], axis=1)
    sin_b = jnp.concatenate([-jnp.sin(ang_r), zero, -jnp.sin(ang_c), zero], axis=1)
    return tuple(jnp.tile(a, (1, LANES // HEAD_DIM)).astype(F32) for a in (cos, sin_a, sin_b))


def _pair_heads(a, axis):
    shape = a.shape
    a = a.reshape(shape[:axis] + (ATTN_KV_HEADS, ATTN_GROUP, HEAD_DIM) + shape[axis + 1:])
    return jnp.swapaxes(a, axis, axis + 1).reshape(shape)


def _permute_w_in(w_in):
    aq, ak, av, mq, mk, mv, mo, mi, mf, gq, gk, gv, gg, glr = jnp.split(
        w_in, np.cumsum([Q_COLS, 128, 128, 256, 256, 256, 256, 8, 8, 256, 256, 256, 256])[:13].tolist(), axis=-1)
    pad = jnp.zeros(w_in.shape[:-1] + (SMALL_COLS - 16 - 2 * GLA_RANK,), w_in.dtype)
    return jnp.concatenate([_pair_heads(aq, aq.ndim - 1), ak, av, mq, mk, mv, mo, gq, gk, gv, gg, mi, mf, glr, pad],
                           axis=-1).astype(BF16)


def _permute_w_out(w_out):
    return jnp.concatenate([_pair_heads(w_out[:, :Q_COLS], 1), w_out[:, Q_COLS:]], axis=1).astype(BF16)


def kernel(x, c, ctx, c_ctx, w_ada, b_ada, norm1_g, norm2_g, w_in, attn_sink, m_i_bias, m_f_bias, m_norm_g,
           g_wa2, g_ba, g_norm_g, w_out, w_mlp1, w_mlp2, final_g):
    B, N, D = x.shape
    Nc = ctx.shape[1]
    depth = w_ada.shape[0]
    rope = _rope_tables(N)
    rows = -(-(B + 1) // 8) * 8
    cc = jnp.zeros((rows, D), F32).at[:B].set(c).at[B].set(c_ctx)
    mods = _ada_call(cc, w_ada, b_ada)
    mods_x = mods[:, :B].reshape(depth, B, 6, D)
    mods_c = mods[:, B:B + 1].reshape(depth, 1, 6, D)
    w_in_p = _permute_w_in(w_in)
    wo, w1, w2 = _permute_w_out(w_out), w_mlp1.astype(BF16), w_mlp2.astype(BF16)

    xc = ctx
    for l in range(depth):
        last = l == depth - 1
        q, kv, ml, gl, sl = _inproj_call(x, mods_x[l], norm1_g[l], w_in_p[l], rope)
        ctx_parts = _inproj_call(xc.reshape(1, B * Nc, D), mods_c[l], norm1_g[l], w_in_p[l], None)
        qc, kvc, mc, gc, sc = (t.reshape(B, Nc, -1) for t in ctx_parts)
        oa = _attn_call(attn_sink[l], q, kv, kvc)
        om, omc = _mlstm_call(mc, sc, ml, sl, m_i_bias[l], m_f_bias[l], m_norm_g[l], not last)
        og, ogc = _gla_call(gc, sc, gl, sl, g_wa2[l], g_ba[l], g_norm_g[l], not last)
        x = _outmlp_call(x, oa, om, og, mods_x[l], norm2_g[l], wo[l], w1[l], w2[l], final_g if last else None)
        if not last:
            oac = _attn_call(attn_sink[l], qc, None, kvc)
            flat = lambda t: t.reshape(1, B * Nc, -1)
            xc = _outmlp_call(flat(xc), flat(oac), flat(omc), flat(ogc), mods_c[l], norm2_g[l],
                              wo[l], w1[l], w2[l], None).reshape(B, Nc, D)
    return x
```

```python
import functools

import numpy as np
import jax
import jax.numpy as jnp
from jax import lax
from jax.experimental import pallas as pl
from jax.experimental.pallas import tpu as pltpu

F32 = jnp.float32
BF16 = jnp.bfloat16

HEAD_DIM = 64
GRID_W = 64
ATTN_HEADS = 8
ATTN_KV_HEADS = 2
ATTN_GROUP = ATTN_HEADS // ATTN_KV_HEADS
WINDOW = 128
ATTN_BLOCK = 128
ATTN_SUBBLOCKS = 4
ROPE_BASE = 10000.0
ROPE_AXIS_DIM = HEAD_DIM // 2
M_HEADS = 4
G_HEADS = 4
CHUNK = 64
GLA_GROUP = 16
M_CHUNK = 128
M_GROUP = 8
M_AUG = HEAD_DIM + 8
KT_CHUNKS = 128 // 8
GLA_RANK = 16
GLA_TAU = 16.0
EPS = 1e-6
NEG_BIG = -1e30
LOG2E = 1.4426950408889634

LANES = 128
Q_COLS = ATTN_HEADS * HEAD_DIM
KV_COLS = 2 * ATTN_KV_HEADS * HEAD_DIM
MIX_COLS = 4 * M_HEADS * HEAD_DIM
SMALL_COLS = LANES
N_MGATES = 2 * 2 * 2
VMEM_LIMIT = 56 * 1024 * 1024


def _cparams(sem):
    return pltpu.CompilerParams(dimension_semantics=sem, vmem_limit_bytes=VMEM_LIMIT)


def _pick_tile(n, pref):
    t = pref
    while n % t:
        t //= 2
    return t


def _split3(a):
    a1 = a.astype(BF16)
    r1 = a - a1.astype(F32)
    a2 = r1.astype(BF16)
    a3 = (r1 - a2.astype(F32)).astype(BF16)
    return a1, a2, a3


def _dot(a, b):
    return jnp.dot(a, b, preferred_element_type=F32)


def _dot_nt(a, b):
    return lax.dot_general(a, b, (((1,), (1,)), ((), ())), preferred_element_type=F32)


def _dot_hi(a, b):
    a1, a2, _ = _split3(a)
    b1, b2, _ = _split3(b)
    return _dot(a1, b1) + (_dot(a1, b2) + _dot(a2, b1))


def _log_sigmoid(x):
    return jnp.minimum(x, 0.0) - jnp.log(1.0 + jnp.exp(-jnp.abs(x)))


def _ada_kernel(cc_ref, w_ref, b_ref, o_ref):
    cc = cc_ref[...]
    s = cc * jax.nn.sigmoid(cc)
    o_ref[0] = _dot_hi(s, w_ref[0]) + b_ref[0]


def _ada_call(cc, w_ada, b_ada):
    depth, d, six_d = w_ada.shape
    rows = cc.shape[0]
    tn = _pick_tile(six_d, 1024)
    return pl.pallas_call(
        _ada_kernel,
        grid=(depth, six_d // tn),
        in_specs=[pl.BlockSpec((rows, d), lambda l, j: (0, 0)),
                  pl.BlockSpec((1, d, tn), lambda l, j: (l, 0, j)),
                  pl.BlockSpec((1, 1, tn), lambda l, j: (l, 0, j))],
        out_specs=pl.BlockSpec((1, rows, tn), lambda l, j: (l, 0, j)),
        out_shape=jax.ShapeDtypeStruct((depth, rows, six_d), F32),
        compiler_params=_cparams(("arbitrary", "arbitrary")),
        name="ada_mod",
    )(cc, w_ada, b_ada.reshape(depth, 1, six_d))


def _norm_mod(x, g, shift, scale):
    ms = jnp.mean(x * x, axis=-1, keepdims=True)
    h = x * lax.rsqrt(ms + EPS) * g
    return h * (1.0 + scale) + shift


def _inproj_kernel(*refs, use_rope):
    if use_rope:
        x_ref, mod_ref, g_ref, w_ref, cos_ref, sa_ref, sb_ref, q_ref, kv_ref, m_ref, gl_ref, s_ref = refs
    else:
        x_ref, mod_ref, g_ref, w_ref, q_ref, kv_ref, m_ref, gl_ref, s_ref = refs
    mod = mod_ref[0]
    hb = _norm_mod(x_ref[0], g_ref[...], mod[0:1], mod[1:2]).astype(BF16)

    def proj(lo, width):
        return _dot(hb, w_ref[:, lo:lo + width])

    pa = proj(0, Q_COLS + KV_COLS)
    if use_rope:
        cos, sa, sb = cos_ref[...], sa_ref[...], sb_ref[...]
        segs = []
        for j in range((Q_COLS + KV_COLS // 2) // LANES):
            seg = pa[:, j * LANES:(j + 1) * LANES]
            segs.append(seg * cos + pltpu.roll(seg, ROPE_AXIS_DIM // 2, 1) * sa
                        + pltpu.roll(seg, LANES - ROPE_AXIS_DIM // 2, 1) * sb)
        qk = jnp.concatenate(segs, axis=1)
    else:
        qk = pa[:, :Q_COLS + KV_COLS // 2]
    q_ref[0] = qk[:, :Q_COLS] * (HEAD_DIM ** -0.5 * LOG2E)
    kv_ref[0, :, :KV_COLS // 2] = qk[:, Q_COLS:]
    kv_ref[0, :, KV_COLS // 2:] = pa[:, Q_COLS + KV_COLS // 2:]
    lo = Q_COLS + KV_COLS
    m_ref[0] = proj(lo, MIX_COLS)
    gl_ref[0] = proj(lo + MIX_COLS, MIX_COLS)
    s_ref[0] = proj(lo + 2 * MIX_COLS, SMALL_COLS)


def _inproj_call(x, mod, g, w, rope):
    bx, n, d = x.shape
    tm = _pick_tile(n, 512)
    cols = w.shape[1]
    in_specs = [pl.BlockSpec((1, tm, d), lambda b, i: (b, i, 0)),
                pl.BlockSpec((1, 6, d), lambda b, i: (b, 0, 0)),
                pl.BlockSpec((1, d), lambda b, i: (0, 0)),
                pl.BlockSpec((d, cols), lambda b, i: (0, 0))]
    args = [x, mod, g.reshape(1, d), w]
    if rope is not None:
        in_specs += [pl.BlockSpec((tm, LANES), lambda b, i: (i, 0))] * 3
        args += list(rope)
    widths = (Q_COLS, KV_COLS, MIX_COLS, MIX_COLS, SMALL_COLS)
    return pl.pallas_call(
        functools.partial(_inproj_kernel, use_rope=rope is not None),
        grid=(bx, n // tm),
        in_specs=in_specs,
        out_specs=[pl.BlockSpec((1, tm, wd), lambda b, i: (b, i, 0)) for wd in widths],
        out_shape=[jax.ShapeDtypeStruct((bx, n, wd), F32) for wd in widths],
        compiler_params=_cparams(("parallel", "parallel")),
        name="inproj_rope" if rope is not None else "inproj",
    )(*args)


def _attn_kernel(*refs, local, tq, nsub):
    if local:
        sink_ref, q_ref, kv_ref, kvc_ref, o_ref, km_ref, vt_ref, bias_ref, kmc_ref, vtc_ref = refs
    else:
        sink_ref, q_ref, kvc_ref, o_ref, kmc_ref, vtc_ref = refs
    npairs = ATTN_GROUP
    cols = npairs * tq
    span = 3 * ATTN_BLOCK
    lane = lax.broadcasted_iota(jnp.int32, (1, LANES), 1)
    head_lanes = (lane < HEAD_DIM, lane >= HEAD_DIM)
    row8 = lax.broadcasted_iota(jnp.int32, (8, LANES), 0)
    ones8 = jnp.where(row8 == 0, 1.0, 0.0).astype(F32)

    def prepare(src_ref, km, vt):
        ntiles = src_ref.shape[1] // LANES
        tiles = _group(ntiles, 4)

        def body(i, carry):
            rows = [pl.ds(pl.multiple_of((i * tiles + u) * LANES, LANES), LANES) for u in range(tiles)]
            k128 = [src_ref[0, r, 0:LANES] for r in rows]
            v_t = [src_ref[0, r, LANES:2 * LANES].T for r in rows]
            for u, r in enumerate(rows):
                for kvh in range(ATTN_KV_HEADS):
                    km[kvh, r, :] = jnp.where(head_lanes[kvh], k128[u], 0.0).astype(BF16)
                    vt[kvh, 0:HEAD_DIM, r] = v_t[u][kvh * HEAD_DIM:(kvh + 1) * HEAD_DIM]
                    vt[kvh, HEAD_DIM:M_AUG, r] = ones8
            return carry

        lax.fori_loop(0, ntiles // tiles, body, 0)

    @pl.when(pl.program_id(1) == 0)
    def _():
        prepare(kvc_ref, kmc_ref, vtc_ref)
        if local:
            prepare(kv_ref, km_ref, vt_ref)
            rel = lax.broadcasted_iota(jnp.int32, bias_ref.shape, 0) - 2 * ATTN_BLOCK
            qoff = lax.broadcasted_iota(jnp.int32, bias_ref.shape, 1) & (tq - 1)
            bias_ref[...] = jnp.where(jnp.abs(rel - qoff) <= WINDOW, 0.0, NEG_BIG)

    units = [(sb, kvh) for sb in range(nsub) for kvh in range(ATTN_KV_HEADS)]
    qall, start, band = {}, {}, {}
    for sb in range(nsub):
        q = q_ref[0, sb * tq:(sb + 1) * tq, :].astype(BF16)
        qall[sb] = jnp.concatenate([q[:, p * LANES:(p + 1) * LANES] for p in range(npairs)], axis=0)
        if local:
            j = pl.program_id(1) * nsub + sb
            start[sb] = pl.multiple_of(jnp.clip((j - 1) * ATTN_BLOCK, 0, kv_ref.shape[1] - span), ATTN_BLOCK)
            band[sb] = bias_ref[pl.ds(pl.multiple_of(start[sb] - j * ATTN_BLOCK + 2 * ATTN_BLOCK, ATTN_BLOCK),
                                      span), :]
    sink = [jnp.concatenate([jnp.full((1, tq), sink_ref[kvh * ATTN_GROUP + p] * LOG2E, F32)
                             for p in range(npairs)], axis=1) for kvh in range(ATTN_KV_HEADS)]
    s_ctx, s_loc, m, acc = {}, {}, {}, {}
    for sb, kvh in units:
        s_ctx[sb, kvh] = _dot_nt(kmc_ref[kvh], qall[sb])
        if local:
            s_loc[sb, kvh] = _dot_nt(km_ref[kvh, pl.ds(start[sb], span), :], qall[sb]) + band[sb]
    for u in units:
        m[u] = jnp.maximum(jnp.max(s_ctx[u], axis=0, keepdims=True), sink[u[1]])
        if local:
            m[u] = jnp.maximum(m[u], jnp.max(s_loc[u], axis=0, keepdims=True))
    for sb, kvh in units:
        u = (sb, kvh)
        acc[u] = _dot(vtc_ref[kvh].astype(BF16), jnp.exp2(s_ctx[u] - m[u]).astype(BF16))
        if local:
            acc[u] = acc[u] + _dot(vt_ref[kvh, :, pl.ds(start[sb], span)].astype(BF16),
                                   jnp.exp2(s_loc[u] - m[u]).astype(BF16))
    o_t = {u: acc[u][0:HEAD_DIM] / (acc[u][HEAD_DIM:HEAD_DIM + 1] + jnp.exp2(sink[u[1]] - m[u])) for u in units}
    for sb in range(nsub):
        for p in range(npairs):
            for c in range(tq // LANES):
                sl = slice(p * tq + c * LANES, p * tq + (c + 1) * LANES)
                tile = jnp.concatenate([o_t[sb, kvh][:, sl] for kvh in range(ATTN_KV_HEADS)], axis=0)
                r0 = sb * tq + c * LANES
                o_ref[0, r0:r0 + LANES, p * LANES:(p + 1) * LANES] = tile.T


def _attn_call(sink, q, kv, kvc):
    b, nq, _ = q.shape
    nc = kvc.shape[1]
    local = kv is not None
    tq = ATTN_BLOCK if local else nq
    nsub = ATTN_SUBBLOCKS if local else 1
    assert tq % LANES == 0 and nc % LANES == 0 and tq & (tq - 1) == 0 and nq % (nsub * tq) == 0
    in_specs = [pl.BlockSpec(memory_space=pltpu.SMEM),
                pl.BlockSpec((1, nsub * tq, Q_COLS), lambda bi, j: (bi, j, 0))]
    args = [sink, q]
    scratch = []
    if local:
        in_specs.append(pl.BlockSpec((1, nq, KV_COLS), lambda bi, j: (bi, 0, 0)))
        args.append(kv)
        scratch += [pltpu.VMEM((ATTN_KV_HEADS, nq, LANES), BF16), pltpu.VMEM((ATTN_KV_HEADS, M_AUG, nq), F32),
                    pltpu.VMEM((5 * ATTN_BLOCK, ATTN_GROUP * tq), F32)]
    in_specs.append(pl.BlockSpec((1, nc, KV_COLS), lambda bi, j: (bi, 0, 0)))
    args.append(kvc)
    scratch += [pltpu.VMEM((ATTN_KV_HEADS, nc, LANES), BF16), pltpu.VMEM((ATTN_KV_HEADS, M_AUG, nc), F32)]
    return pl.pallas_call(
        functools.partial(_attn_kernel, local=local, tq=tq, nsub=nsub),
        grid=(b, nq // (nsub * tq)),
        in_specs=in_specs,
        out_specs=pl.BlockSpec((1, nsub * tq, Q_COLS), lambda bi, j: (bi, j, 0)),
        out_shape=jax.ShapeDtypeStruct((b, nq, Q_COLS), F32),
        scratch_shapes=scratch,
        compiler_params=_cparams(("arbitrary", "arbitrary")),
        name="attn_window" if local else "attn_ctx",
    )(*args)


def _tri(n, rev):
    ri = lax.broadcasted_iota(jnp.int32, (n, n), 0)
    ci = lax.broadcasted_iota(jnp.int32, (n, n), 1)
    return ci >= ri if rev else ci <= ri


def _seq_specs(n):
    return [pl.BlockSpec((1, n, LANES), lambda b, hp, off=off: (b, 0, off + hp)) for off in (0, 2, 4, 6)]


def _group(nchunks, pref):
    g = min(pref, nchunks)
    assert nchunks % g == 0
    return g


def _mlstm_kernel(*refs, with_ctx_out):
    (qc_ref, kc_ref, vc_ref, oc_gate_ref, grc_ref,
     ql_ref, kl_ref, vl_ref, ol_gate_ref, grl_ref, bias_ref, g_ref) = refs[:12]
    if with_ctx_out:
        o_lat_ref, o_ctx_ref = refs[12:14]
        scratch = refs[14:]
    else:
        o_lat_ref, o_ctx_ref = refs[12], None
        scratch = refs[13:]
    w_ref, b_ref, ck_ref, tot_ref, mc_ref, mp_ref, kt_ref, cc_ref, vt_ref = scratch
    L = M_CHUNK
    ncc, ncl = qc_ref.shape[1] // L, ql_ref.shape[1] // L
    kt_c = -(-ncc // KT_CHUNKS)

    mask_t = (_tri(L, True), _tri(L, False))
    tri_f, tri_b = (mk.astype(BF16) for mk in mask_t)
    ones_m = jnp.ones((L, L), BF16)
    row8 = lax.broadcasted_iota(jnp.int32, (8, L), 0)
    rev8 = (row8 & 2) != 0
    ones8 = jnp.where(row8 == 0, 1.0, 0.0).astype(F32)

    def prep(gr_ref, v_ref, base, kt_base, v_base):
        nch = gr_ref.shape[3]
        li = (gr_ref[0, 0, 0] + bias_ref[0, 0]).reshape(nch * 8, L)
        lf = _log_sigmoid(gr_ref[0, 0, 1] + bias_ref[0, 1]).reshape(nch * 8, L)
        rev = (lax.broadcasted_iota(jnp.int32, li.shape, 0) & 2) != 0
        lane = lax.broadcasted_iota(jnp.int32, li.shape, 1)
        l1, l2, l3 = _split3(lf)
        scan = lambda m: _dot(l1, m) + (_dot(l2, m) + _dot(l3, m))
        b = jnp.where(rev, scan(tri_b), scan(tri_f))
        tot = scan(ones_m)
        g = tot - b + li
        mc = jnp.max(g, axis=1, keepdims=True)
        kap = li - b
        ckf, ckb = kap, kap
        s = 1
        while s < L:
            ckf = jnp.maximum(ckf, jnp.where(lane >= s, pltpu.roll(ckf, s, 1), NEG_BIG))
            ckb = jnp.maximum(ckb, jnp.where(lane < L - s, pltpu.roll(ckb, L - s, 1), NEG_BIG))
            s *= 2
        sl = pl.ds(base, nch)
        w_ref[sl] = jnp.exp(g - mc).reshape(nch, 8, L)
        b_ref[sl] = b.reshape(nch, 8, L)
        ck_ref[sl] = jnp.where(rev, ckb, ckf).reshape(nch, 8, L)
        tot_ref[sl] = tot.reshape(nch, 8, L)
        mc_ref[sl] = jnp.broadcast_to(mc, (nch * 8, L)).reshape(nch, 8, L)
        for gi in range(-(-nch // KT_CHUNKS)):
            blk = kap[gi * LANES:(gi + 1) * LANES]
            if blk.shape[0] < LANES:
                blk = jnp.concatenate([blk, jnp.zeros((LANES - blk.shape[0], L), F32)], axis=0)
            kt_ref[kt_base + gi] = blk.T

        tiles = _group(nch, 4)

        def vt_body(i, carry):
            blocks = [v_ref[0, pl.ds(pl.multiple_of((i * tiles + u) * L, L), L), :].T for u in range(tiles)]
            for u, blk in enumerate(blocks):
                vt_ref[:, pl.ds(pl.multiple_of(v_base + (i * tiles + u) * L, L), L)] = blk
            return carry

        lax.fori_loop(0, nch // tiles, vt_body, 0)

    def vaug(vt2, hh):
        return jnp.concatenate([vt2[hh * HEAD_DIM:(hh + 1) * HEAD_DIM], ones8], axis=0)

    def summarize(k_ref, base, v_base):
        nch = k_ref.shape[1] // L
        grp = _group(nch, M_GROUP)

        def body(t, carry):
            lhs, rhs = {}, {}
            for gi in range(grp):
                cl = t * grp + gi
                k2 = (k_ref[0, pl.ds(pl.multiple_of(cl * L, L), L), :] * (HEAD_DIM ** -0.5)).astype(BF16)
                vt2 = vt_ref[:, pl.ds(pl.multiple_of(v_base + cl * L, L), L)]
                w8 = w_ref[base + cl]
                for hh in range(2):
                    va = vaug(vt2, hh)
                    lhs[gi, hh] = jnp.concatenate([va * w8[d * 2 + hh:d * 2 + hh + 1] for d in range(2)],
                                                  axis=0).astype(BF16)
                    rhs[gi, hh] = k2[:, hh * HEAD_DIM:(hh + 1) * HEAD_DIM]
            cc = {key: _dot(lhs[key], rhs[key]) for key in lhs}
            for (gi, hh), val in cc.items():
                for d in range(2):
                    cc_ref[base + t * grp + gi, d * 2 + hh] = val[d * M_AUG:(d + 1) * M_AUG]
            return carry

        lax.fori_loop(0, nch // grp, body, 0)

    def scan_states(base, nch, state):
        def body(t, st):
            m8, cs = st[0], list(st[1:])
            cf, cb = base + t, base + nch - 1 - t
            tot8 = jnp.where(rev8, tot_ref[cb], tot_ref[cf])
            mc8 = jnp.where(rev8, mc_ref[cb], mc_ref[cf])
            m_new = jnp.maximum(tot8 + m8, mc8)
            s_old = jnp.exp(tot8 + m8 - m_new)
            s_new = jnp.exp(mc8 - m_new)
            mp_ref[cf, 0:2, :] = m8[0:2]
            mp_ref[cb, 2:4, :] = m8[2:4]
            for j in range(4):
                c = cf if j < 2 else cb
                summary = cc_ref[c, j]
                cc_ref[c, j] = cs[j]
                cs[j] = s_old[j:j + 1, 0:HEAD_DIM] * cs[j] + s_new[j:j + 1, 0:HEAD_DIM] * summary
            return (m_new, *cs)

        return lax.fori_loop(0, nch, body, state)

    def emit(q_ref, k_ref, gate_ref, o_ref, base, kt_base, v_base):
        nch = q_ref.shape[1] // L
        grp = _group(nch, M_GROUP)

        def body(t, carry):
            units = [(gi, hh) for gi in range(grp) for hh in range(2)]
            rows, vas, rho, alpha, floor, kcol, lhs, rhs = {}, {}, {}, {}, {}, {}, {}, {}
            for gi in range(grp):
                cl = t * grp + gi
                c = base + cl
                rows[gi] = pl.ds(pl.multiple_of(cl * L, L), L)
                q2 = q_ref[0, rows[gi], :].astype(BF16)
                k2 = k_ref[0, rows[gi], :] * (HEAD_DIM ** -0.5)
                vt2 = vt_ref[:, pl.ds(pl.multiple_of(v_base + cl * L, L), L)]
                mp8 = mp_ref[c]
                rho[gi] = -jnp.maximum(mp8, ck_ref[c])
                alpha[gi] = jnp.exp(mp8 + rho[gi])
                floor[gi] = jnp.exp(rho[gi] - b_ref[c])
                kt = kt_ref[kt_base + (cl >> 4)]
                kcol[gi] = pltpu.roll(kt, (LANES - (cl & (KT_CHUNKS - 1)) * 8) & (LANES - 1), 1)
                for hh in range(2):
                    ln = slice(hh * HEAD_DIM, (hh + 1) * HEAD_DIM)
                    vas[gi, hh] = vaug(vt2, hh).astype(BF16)
                    lhs[gi, hh] = jnp.concatenate([k2[:, ln], cc_ref[c, hh], cc_ref[c, 2 + hh]], axis=0).astype(BF16)
                    rhs[gi, hh] = q2[:, ln]
            prod = {u: _dot_nt(lhs[u], rhs[u]) for u in units}
            pt = {}
            for gi, hh in units:
                st = prod[gi, hh][0:L]
                for d in range(2):
                    j = d * 2 + hh
                    e = kcol[gi][:, j:j + 1] + rho[gi][j:j + 1, :]
                    pt[gi, hh, d] = (jnp.where(mask_t[d], jnp.exp(e), 0.0) * st).astype(BF16)
            pv = {key: _dot(vas[key[0], key[1]], val) for key, val in pt.items()}
            for gi in range(grp):
                halves = []
                for hh in range(2):
                    hsum = None
                    for d in range(2):
                        j = d * 2 + hh
                        lo = L + d * M_AUG
                        num = alpha[gi][j:j + 1, :] * prod[gi, hh][lo:lo + M_AUG] + pv[gi, hh, d]
                        h = num[0:HEAD_DIM] / jnp.maximum(jnp.abs(num[HEAD_DIM:HEAD_DIM + 1]), floor[gi][j:j + 1, :])
                        hsum = h if hsum is None else hsum + h
                    ms = jnp.sum(hsum * hsum, axis=0, keepdims=True) * (1.0 / HEAD_DIM)
                    halves.append(hsum * lax.rsqrt(ms + EPS))
                y = jnp.concatenate(halves, axis=0).T
                o_ref[0, rows[gi], :] = y * g_ref[...] * jax.nn.sigmoid(gate_ref[0, rows[gi], :])
            return carry

        lax.fori_loop(0, nch // grp, body, 0)

    n_ctx = qc_ref.shape[1]
    prep(grc_ref, vc_ref, 0, 0, 0)
    prep(grl_ref, vl_ref, ncc, kt_c, n_ctx)
    summarize(kc_ref, 0, 0)
    summarize(kl_ref, ncc, n_ctx)
    zero = (jnp.zeros((8, L), F32),) + (jnp.zeros((M_AUG, HEAD_DIM), F32),) * 4
    state = scan_states(0, ncc, zero)
    scan_states(ncc, ncl, state)
    if with_ctx_out:
        emit(qc_ref, kc_ref, oc_gate_ref, o_ctx_ref, 0, 0, 0)
    emit(ql_ref, kl_ref, ol_gate_ref, o_lat_ref, ncc, kt_c, n_ctx)


def _mlstm_gates(s):
    b, n, _ = s.shape
    g = s[:, :, :2 * N_MGATES].reshape(b, n, 2, 2, 2, 2)
    g = g.transpose(0, 4, 2, 3, 5, 1).reshape(b, 2, 2, 4, n // M_CHUNK, M_CHUNK).transpose(0, 1, 2, 4, 3, 5)
    return jnp.concatenate([g, g], axis=4)


def _mlstm_call(mc, sc, ml, sl, i_bias, f_bias, g, with_ctx_out):
    b, n, _ = ml.shape
    nc = mc.shape[1]
    assert n % M_CHUNK == 0 and nc % M_CHUNK == 0
    ncc, ncl = nc // M_CHUNK, n // M_CHUNK
    bias = jnp.stack([i_bias, f_bias]).astype(F32).reshape(2, 2, 2, 2).transpose(2, 0, 1, 3).reshape(2, 2, 4)
    bias = jnp.concatenate([bias, bias], axis=2)[..., None]
    gr_spec = lambda nch: pl.BlockSpec((1, 1, 2, nch, 8, M_CHUNK), lambda bi, hp: (bi, hp, 0, 0, 0, 0))
    in_specs = (_seq_specs(nc) + [gr_spec(ncc)] + _seq_specs(n) + [gr_spec(ncl)]
                + [pl.BlockSpec((1, 2, 8, 1), lambda bi, hp: (hp, 0, 0, 0)),
                   pl.BlockSpec((1, LANES), lambda bi, hp: (0, hp))])
    out_specs = [pl.BlockSpec((1, n, LANES), lambda bi, hp: (bi, 0, hp))]
    out_shape = [jax.ShapeDtypeStruct((b, n, M_HEADS * HEAD_DIM), F32)]
    if with_ctx_out:
        out_specs.append(pl.BlockSpec((1, nc, LANES), lambda bi, hp: (bi, 0, hp)))
        out_shape.append(jax.ShapeDtypeStruct((b, nc, M_HEADS * HEAD_DIM), F32))
    tot = ncc + ncl
    per_chunk = pltpu.VMEM((tot, 8, M_CHUNK), F32)
    scratch = [per_chunk] * 6 + [
        pltpu.VMEM((-(-ncc // KT_CHUNKS) + -(-ncl // KT_CHUNKS), M_CHUNK, LANES), F32),
        pltpu.VMEM((tot, 4, M_AUG, HEAD_DIM), F32),
        pltpu.VMEM((LANES, nc + n), F32)]
    outs = pl.pallas_call(
        functools.partial(_mlstm_kernel, with_ctx_out=with_ctx_out),
        grid=(b, M_HEADS // 2),
        in_specs=in_specs,
        out_specs=out_specs,
        out_shape=out_shape,
        scratch_shapes=scratch,
        compiler_params=_cparams(("parallel", "arbitrary")),
        name="mlstm",
    )(mc, mc, mc, mc, _mlstm_gates(sc), ml, ml, ml, ml, _mlstm_gates(sl), bias, g.reshape(1, -1))
    return (outs[0], outs[1]) if with_ctx_out else (outs[0], None)


def _head_norm_gate(hsum, g, gate):
    sq = hsum * hsum
    lane = lax.broadcasted_iota(jnp.int32, hsum.shape, 1)
    first = lane < HEAD_DIM
    s0 = jnp.sum(jnp.where(first, sq, 0.0), axis=-1, keepdims=True)
    s1 = jnp.sum(jnp.where(first, 0.0, sq), axis=-1, keepdims=True)
    ms = jnp.where(first, s0, s1) * (1.0 / HEAD_DIM)
    return hsum * lax.rsqrt(ms + EPS) * g * gate


def _seg_scan_sum(x, rev):
    rows = x.shape[0]
    pos = lax.broadcasted_iota(jnp.int32, x.shape, 0) & (CHUNK - 1)
    s = 1
    while s < CHUNK:
        if rev:
            x = x + jnp.where(pos < CHUNK - s, pltpu.roll(x, rows - s, 0), 0.0)
        else:
            x = x + jnp.where(pos >= s, pltpu.roll(x, s, 0), 0.0)
        s *= 2
    return x


def _gla_kernel(*refs, with_ctx_out):
    (qc_ref, kc_ref, vc_ref, oc_gate_ref, sc_ref,
     ql_ref, kl_ref, vl_ref, ol_gate_ref, sl_ref, wa_ref, ba_ref, g_ref) = refs[:13]
    if with_ctx_out:
        o_lat_ref, o_ctx_ref, bc_ref, u_ref, dec_ref = refs[13:]
    else:
        o_lat_ref, bc_ref, u_ref, dec_ref = refs[13:]
        o_ctx_ref = None
    ncc, ncl = qc_ref.shape[1] // CHUNK, ql_ref.shape[1] // CHUNK
    lo = 2 * N_MGATES
    mask_f, mask_b = _tri(CHUNK, False), _tri(CHUNK, True)
    first = lax.broadcasted_iota(jnp.int32, (HEAD_DIM, LANES), 1) < HEAD_DIM
    halves = (slice(0, HEAD_DIM), slice(HEAD_DIM, 2 * HEAD_DIM))
    silu = lambda t: t * jax.nn.sigmoid(t)

    def slab_of(t, grp):
        return pl.multiple_of(t * (grp * CHUNK), grp * CHUNK)

    def summarize(k_ref, v_ref, s_ref, base):
        nch = k_ref.shape[1] // CHUNK
        grp = _group(nch, GLA_GROUP)
        assert grp % 2 == 0

        def body(t, carry):
            r0 = slab_of(t, grp)
            slab = pl.ds(r0, grp * CHUNK)
            k3 = k_ref[0, slab, :].reshape(grp, CHUNK, LANES)
            v2 = v_ref[0, slab, :]
            kws, ends = [], []
            for d in range(2):
                lr = s_ref[0, slab, lo + d * GLA_RANK:lo + (d + 1) * GLA_RANK]
                glog = _log_sigmoid(_dot(lr.astype(BF16), wa_ref[d].astype(BF16)) + ba_ref[d]) * (1.0 / GLA_TAU)
                bc = _seg_scan_sum(glog, d == 1)
                bc_ref[d, pl.ds(pl.multiple_of(base * CHUNK + r0, CHUNK), grp * CHUNK), :] = bc
                bc3 = bc.reshape(grp, CHUNK, LANES)
                end = bc3[:, 0:1, :] if d == 1 else bc3[:, CHUNK - 1:CHUNK, :]
                kws.append((k3 * jnp.exp(end - bc3)).astype(BF16))
                ends.append(end)
            vts = [v2[p * LANES:(p + 1) * LANES, :].T.astype(BF16) for p in range(grp // 2)]
            us = {}
            for gi in range(grp):
                for hh in range(2):
                    vt = vts[gi // 2][halves[hh], halves[gi % 2]]
                    for d in range(2):
                        us[gi, hh, d] = _dot(vt, kws[d][gi][:, halves[hh]])
            for gi in range(grp):
                c = base + t * grp + gi
                for hh in range(2):
                    u_ref[hh, c] = jnp.concatenate([us[gi, hh, 0], us[gi, hh, 1]], axis=1)
                    end2 = jnp.concatenate([ends[0][gi][:, halves[hh]], ends[1][gi][:, halves[hh]]], axis=1)
                    dec_ref[hh, c] = jnp.broadcast_to(jnp.exp(end2), (8, LANES))
            return carry

        lax.fori_loop(0, nch // grp, body, 0)

    def scan_states(base, nch, state):
        def body(t, st):
            cf, cb = base + t, base + nch - 1 - t
            new = []
            for hh in range(2):
                u = jnp.where(first, u_ref[hh, cf], u_ref[hh, cb])
                dec = jnp.where(first[0:1], dec_ref[hh, cf][0:1], dec_ref[hh, cb][0:1])
                u_ref[hh, cf, :, 0:HEAD_DIM] = st[hh][:, 0:HEAD_DIM]
                u_ref[hh, cb, :, HEAD_DIM:] = st[hh][:, HEAD_DIM:]
                new.append(st[hh] * dec + u)
            return tuple(new)

        return lax.fori_loop(0, nch, body, state)

    def emit(q_ref, k_ref, v_ref, gate_ref, o_ref, base):
        nch = q_ref.shape[1] // CHUNK
        grp = _group(nch, GLA_GROUP)

        def body(t, carry):
            r0 = slab_of(t, grp)
            slab = pl.ds(r0, grp * CHUNK)
            q2 = q_ref[0, slab, :] * (HEAD_DIM ** -0.5)
            k2 = k_ref[0, slab, :]
            vb = v_ref[0, slab, :].astype(BF16)
            qd, kd = [], []
            for d in range(2):
                bc = bc_ref[d, pl.ds(pl.multiple_of(base * CHUNK + r0, CHUNK), grp * CHUNK), :]
                qd.append((q2 * jnp.exp(bc)).astype(BF16))
                kd.append((k2 * jnp.exp(-bc)).astype(BF16))
            units = [(gi, hh) for gi in range(grp) for hh in range(2)]
            rs = lambda gi: slice(gi * CHUNK, (gi + 1) * CHUNK)
            att = {}
            for gi, hh in units:
                a_f = jnp.where(mask_f, _dot_nt(qd[0][rs(gi), halves[hh]], kd[0][rs(gi), halves[hh]]), 0.0)
                a_b = jnp.where(mask_b, _dot_nt(qd[1][rs(gi), halves[hh]], kd[1][rs(gi), halves[hh]]), 0.0)
                att[gi, hh] = (a_f + a_b).astype(BF16)
            outs = []
            for gi in range(grp):
                c = base + t * grp + gi
                parts = []
                for hh in range(2):
                    qcat = jnp.concatenate([qd[0][rs(gi), halves[hh]], qd[1][rs(gi), halves[hh]]], axis=1)
                    parts.append(_dot(att[gi, hh], vb[rs(gi), halves[hh]])
                                 + _dot_nt(qcat, u_ref[hh, c].astype(BF16)))
                outs.append(jnp.concatenate(parts, axis=1))
            o2 = jnp.concatenate(outs, axis=0)
            o_ref[0, slab, :] = _head_norm_gate(o2, g_ref[...], silu(gate_ref[0, slab, :]))
            return carry

        lax.fori_loop(0, nch // grp, body, 0)

    summarize(kc_ref, vc_ref, sc_ref, 0)
    summarize(kl_ref, vl_ref, sl_ref, ncc)
    state = scan_states(0, ncc, (jnp.zeros((HEAD_DIM, LANES), F32),) * 2)
    scan_states(ncc, ncl, state)
    if with_ctx_out:
        emit(qc_ref, kc_ref, vc_ref, oc_gate_ref, o_ctx_ref, 0)
    emit(ql_ref, kl_ref, vl_ref, ol_gate_ref, o_lat_ref, ncc)


def _gla_call(gc, sc, gl, sl, wa2, ba, g, with_ctx_out):
    b, n, _ = gl.shape
    nc = gc.shape[1]
    tot = (nc + n) // CHUNK
    small = lambda nn: pl.BlockSpec((1, nn, SMALL_COLS), lambda bi, hp: (bi, 0, 0))
    in_specs = (_seq_specs(nc) + [small(nc)] + _seq_specs(n) + [small(n)]
                + [pl.BlockSpec((2, GLA_RANK, LANES), lambda bi, hp: (0, 0, hp)),
                   pl.BlockSpec((2, 1, LANES), lambda bi, hp: (0, 0, hp)),
                   pl.BlockSpec((1, LANES), lambda bi, hp: (0, hp))])
    out_specs = [pl.BlockSpec((1, n, LANES), lambda bi, hp: (bi, 0, hp))]
    out_shape = [jax.ShapeDtypeStruct((b, n, G_HEADS * HEAD_DIM), F32)]
    if with_ctx_out:
        out_specs.append(pl.BlockSpec((1, nc, LANES), lambda bi, hp: (bi, 0, hp)))
        out_shape.append(jax.ShapeDtypeStruct((b, nc, G_HEADS * HEAD_DIM), F32))
    scratch = [pltpu.VMEM((2, nc + n, LANES), F32),
               pltpu.VMEM((2, tot, HEAD_DIM, LANES), F32),
               pltpu.VMEM((2, tot, 8, LANES), F32)]
    outs = pl.pallas_call(
        functools.partial(_gla_kernel, with_ctx_out=with_ctx_out),
        grid=(b, G_HEADS // 2),
        in_specs=in_specs,
        out_specs=out_specs,
        out_shape=out_shape,
        scratch_shapes=scratch,
        compiler_params=_cparams(("parallel", "arbitrary")),
        name="gla",
    )(gc, gc, gc, gc, sc, gl, gl, gl, gl, sl, wa2, ba.reshape(2, 1, -1), g.reshape(1, -1))
    return (outs[0], outs[1]) if with_ctx_out else (outs[0], None)


def _outmlp_kernel(*refs, final, ff_tile):
    if final:
        x_ref, oa_ref, om_ref, og_ref, mod_ref, g_ref, wo_ref, w1_ref, w2_ref, fg_ref, y_ref = refs
    else:
        x_ref, oa_ref, om_ref, og_ref, mod_ref, g_ref, wo_ref, w1_ref, w2_ref, y_ref = refs
    mod = mod_ref[0]
    a_w, m_w = oa_ref.shape[2], om_ref.shape[2]
    o = (_dot(oa_ref[0].astype(BF16), wo_ref[0:a_w, :])
         + _dot(om_ref[0].astype(BF16), wo_ref[a_w:a_w + m_w, :])
         + _dot(og_ref[0].astype(BF16), wo_ref[a_w + m_w:, :]))
    x1 = x_ref[0] + mod[2:3] * o
    hb = _norm_mod(x1, g_ref[...], mod[3:4], mod[4:5]).astype(BF16)
    d_ff = w1_ref.shape[1]
    acc = jnp.zeros(x1.shape, F32)
    for j in range(d_ff // ff_tile):
        t = jnp.maximum(_dot(hb, w1_ref[:, j * ff_tile:(j + 1) * ff_tile]), 0.0)
        acc = acc + _dot((t * t).astype(BF16), w2_ref[j * ff_tile:(j + 1) * ff_tile, :])
    x2 = x1 + mod[5:6] * acc
    if final:
        ms = jnp.mean(x2 * x2, axis=-1, keepdims=True)
        x2 = x2 * lax.rsqrt(ms + EPS) * fg_ref[...]
    y_ref[0] = x2


def _outmlp_call(x, oa, om, og, mod, g, wo, w1, w2, final_g):
    bx, n, d = x.shape
    tm = _pick_tile(n, 512)
    d_ff = w1.shape[1]
    row = lambda wd: pl.BlockSpec((1, tm, wd), lambda b, i: (b, i, 0))
    whole = lambda arr: pl.BlockSpec(arr.shape, lambda b, i: (0,) * arr.ndim, pipeline_mode=pl.Buffered(1))
    g2 = g.reshape(1, d)
    in_specs = [row(d), row(oa.shape[2]), row(om.shape[2]), row(og.shape[2]),
                pl.BlockSpec((1, 6, d), lambda b, i: (b, 0, 0)), whole(g2), whole(wo), whole(w1), whole(w2)]
    args = [x, oa, om, og, mod, g2, wo, w1, w2]
    if final_g is not None:
        fg = final_g.reshape(1, d)
        in_specs.append(whole(fg))
        args.append(fg)
    return pl.pallas_call(
        functools.partial(_outmlp_kernel, final=final_g is not None, ff_tile=_pick_tile(d_ff, 1024)),
        grid=(bx, n // tm),
        in_specs=in_specs,
        out_specs=row(d),
        out_shape=jax.ShapeDtypeStruct((bx, n, d), F32),
        compiler_params=_cparams(("parallel", "parallel")),
        name="outproj_mlp",
    )(*args)


def _rope_tables(n):
    t = jnp.arange(n)
    inv = ROPE_BASE ** (-jnp.arange(0, ROPE_AXIS_DIM, 2, dtype=F32) / ROPE_AXIS_DIM)
    ang_r = (t // GRID_W).astype(F32)[:, None] * inv[None, :]
    ang_c = (t % GRID_W).astype(F32)[:, None] * inv[None, :]
    half = ROPE_AXIS_DIM // 2
    cos = jnp.concatenate([jnp.cos(ang_r)] * 2 + [jnp.cos(ang_c)] * 2, axis=1)
    zero = jnp.zeros((n, half), F32)
    sin_a = jnp.concatenate([zero, jnp.sin(ang_r), zero, jnp.sin(ang_c)], axis=1)
    sin_b = jnp.concatenate([-jnp.sin(ang_r), zero, -jnp.sin(ang_c), zero], axis=1)
    return tuple(jnp.tile(a, (1, LANES // HEAD_DIM)).astype(F32) for a in (cos, sin_a, sin_b))


def _pair_heads(a, axis):
    shape = a.shape
    a = a.reshape(shape[:axis] + (ATTN_KV_HEADS, ATTN_GROUP, HEAD_DIM) + shape[axis + 1:])
    return jnp.swapaxes(a, axis, axis + 1).reshape(shape)


def _permute_w_in(w_in):
    aq, ak, av, mq, mk, mv, mo, mi, mf, gq, gk, gv, gg, glr = jnp.split(
        w_in, np.cumsum([Q_COLS, 128, 128, 256, 256, 256, 256, 8, 8, 256, 256, 256, 256])[:13].tolist(), axis=-1)
    pad = jnp.zeros(w_in.shape[:-1] + (SMALL_COLS - 16 - 2 * GLA_RANK,), w_in.dtype)
    return jnp.concatenate([_pair_heads(aq, aq.ndim - 1), ak, av, mq, mk, mv, mo, gq, gk, gv, gg, mi, mf, glr, pad],
                           axis=-1).astype(BF16)


def _permute_w_out(w_out):
    return jnp.concatenate([_pair_heads(w_out[:, :Q_COLS], 1), w_out[:, Q_COLS:]], axis=1).astype(BF16)


def kernel(x, c, ctx, c_ctx, w_ada, b_ada, norm1_g, norm2_g, w_in, attn_sink, m_i_bias, m_f_bias, m_norm_g,
           g_wa2, g_ba, g_norm_g, w_out, w_mlp1, w_mlp2, final_g):
    B, N, D = x.shape
    Nc = ctx.shape[1]
    depth = w_ada.shape[0]
    rope = _rope_tables(N)
    rows = -(-(B + 1) // 8) * 8
    cc = jnp.zeros((rows, D), F32).at[:B].set(c).at[B].set(c_ctx)
    mods = _ada_call(cc, w_ada, b_ada)
    mods_x = mods[:, :B].reshape(depth, B, 6, D)
    mods_c = mods[:, B:B + 1].reshape(depth, 1, 6, D)
    w_in_p = _permute_w_in(w_in)
    wo, w1, w2 = _permute_w_out(w_out), w_mlp1.astype(BF16), w_mlp2.astype(BF16)

    xc = ctx
    for l in range(depth):
        last = l == depth - 1
        q, kv, ml, gl, sl = _inproj_call(x, mods_x[l], norm1_g[l], w_in_p[l], rope)
        ctx_parts = _inproj_call(xc.reshape(1, B * Nc, D), mods_c[l], norm1_g[l], w_in_p[l], None)
        qc, kvc, mc, gc, sc = (t.reshape(B, Nc, -1) for t in ctx_parts)
        oa = _attn_call(attn_sink[l], q, kv, kvc)
        om, omc = _mlstm_call(mc, sc, ml, sl, m_i_bias[l], m_f_bias[l], m_norm_g[l], not last)
        og, ogc = _gla_call(gc, sc, gl, sl, g_wa2[l], g_ba[l], g_norm_g[l], not last)
        x = _outmlp_call(x, oa, om, og, mods_x[l], norm2_g[l], wo[l], w1[l], w2[l], final_g if last else None)
        if not last:
            oac = _attn_call(attn_sink[l], qc, None, kvc)
            flat = lambda t: t.reshape(1, B * Nc, -1)
            xc = _outmlp_call(flat(xc), flat(oac), flat(omc), flat(ogc), mods_c[l], norm2_g[l],
                              wo[l], w1[l], w2[l], None).reshape(B, Nc, D)
    return x
```

```python
import functools

import numpy as np
import jax
import jax.numpy as jnp
from jax import lax
from jax.experimental import pallas as pl
from jax.experimental.pallas import tpu as pltpu

F32 = jnp.float32
BF16 = jnp.bfloat16

HEAD_DIM = 64
GRID_W = 64
ATTN_HEADS = 8
ATTN_KV_HEADS = 2
ATTN_GROUP = ATTN_HEADS // ATTN_KV_HEADS
WINDOW = 128
ATTN_BLOCK = 128
ATTN_SUBBLOCKS = 4
ROPE_BASE = 10000.0
ROPE_AXIS_DIM = HEAD_DIM // 2
M_HEADS = 4
G_HEADS = 4
CHUNK = 64
GLA_GROUP = 16
M_CHUNK = 128
M_GROUP = 8
M_AUG = HEAD_DIM + 8
KT_CHUNKS = 128 // 8
GLA_RANK = 16
GLA_TAU = 16.0
EPS = 1e-6
NEG_BIG = -1e30
LOG2E = 1.4426950408889634

LANES = 128
Q_COLS = ATTN_HEADS * HEAD_DIM
KV_COLS = 2 * ATTN_KV_HEADS * HEAD_DIM
MIX_COLS = 4 * M_HEADS * HEAD_DIM
SMALL_COLS = LANES
N_MGATES = 2 * 2 * 2
VMEM_LIMIT = 56 * 1024 * 1024


def _cparams(sem):
    return pltpu.CompilerParams(dimension_semantics=sem, vmem_limit_bytes=VMEM_LIMIT)


def _pick_tile(n, pref):
    t = pref
    while n % t:
        t //= 2
    return t


def _split3(a):
    a1 = a.astype(BF16)
    r1 = a - a1.astype(F32)
    a2 = r1.astype(BF16)
    a3 = (r1 - a2.astype(F32)).astype(BF16)
    return a1, a2, a3


def _dot(a, b):
    return jnp.dot(a, b, preferred_element_type=F32)


def _dot_nt(a, b):
    return lax.dot_general(a, b, (((1,), (1,)), ((), ())), preferred_element_type=F32)


def _dot_hi(a, b):
    a1, a2, _ = _split3(a)
    b1, b2, _ = _split3(b)
    return _dot(a1, b1) + (_dot(a1, b2) + _dot(a2, b1))


def _log_sigmoid(x):
    return jnp.minimum(x, 0.0) - jnp.log(1.0 + jnp.exp(-jnp.abs(x)))


def _ada_kernel(cc_ref, w_ref, b_ref, o_ref):
    cc = cc_ref[...]
    s = cc * jax.nn.sigmoid(cc)
    o_ref[0] = _dot_hi(s, w_ref[0]) + b_ref[0]


def _ada_call(cc, w_ada, b_ada):
    depth, d, six_d = w_ada.shape
    rows = cc.shape[0]
    tn = _pick_tile(six_d, 1024)
    return pl.pallas_call(
        _ada_kernel,
        grid=(depth, six_d // tn),
        in_specs=[pl.BlockSpec((rows, d), lambda l, j: (0, 0)),
                  pl.BlockSpec((1, d, tn), lambda l, j: (l, 0, j)),
                  pl.BlockSpec((1, 1, tn), lambda l, j: (l, 0, j))],
        out_specs=pl.BlockSpec((1, rows, tn), lambda l, j: (l, 0, j)),
        out_shape=jax.ShapeDtypeStruct((depth, rows, six_d), F32),
        compiler_params=_cparams(("arbitrary", "arbitrary")),
        name="ada_mod",
    )(cc, w_ada, b_ada.reshape(depth, 1, six_d))


def _norm_mod(x, g, shift, scale):
    ms = jnp.mean(x * x, axis=-1, keepdims=True)
    h = x * lax.rsqrt(ms + EPS) * g
    return h * (1.0 + scale) + shift


def _inproj_kernel(*refs, use_rope):
    if use_rope:
        x_ref, mod_ref, g_ref, w_ref, cos_ref, sa_ref, sb_ref, q_ref, kv_ref, m_ref, gl_ref, s_ref = refs
    else:
        x_ref, mod_ref, g_ref, w_ref, q_ref, kv_ref, m_ref, gl_ref, s_ref = refs
    mod = mod_ref[0]
    hb = _norm_mod(x_ref[0], g_ref[...], mod[0:1], mod[1:2]).astype(BF16)

    def proj(lo, width):
        return _dot(hb, w_ref[:, lo:lo + width])

    pa = proj(0, Q_COLS + KV_COLS)
    if use_rope:
        cos, sa, sb = cos_ref[...], sa_ref[...], sb_ref[...]
        segs = []
        for j in range((Q_COLS + KV_COLS // 2) // LANES):
            seg = pa[:, j * LANES:(j + 1) * LANES]
            segs.append(seg * cos + pltpu.roll(seg, ROPE_AXIS_DIM // 2, 1) * sa
                        + pltpu.roll(seg, LANES - ROPE_AXIS_DIM // 2, 1) * sb)
        qk = jnp.concatenate(segs, axis=1)
    else:
        qk = pa[:, :Q_COLS + KV_COLS // 2]
    q_ref[0] = qk[:, :Q_COLS] * (HEAD_DIM ** -0.5 * LOG2E)
    kv_ref[0, :, :KV_COLS // 2] = qk[:, Q_COLS:]
    kv_ref[0, :, KV_COLS // 2:] = pa[:, Q_COLS + KV_COLS // 2:]
    lo = Q_COLS + KV_COLS
    m_ref[0] = proj(lo, MIX_COLS)
    gl_ref[0] = proj(lo + MIX_COLS, MIX_COLS)
    s_ref[0] = proj(lo + 2 * MIX_COLS, SMALL_COLS)


def _inproj_call(x, mod, g, w, rope):
    bx, n, d = x.shape
    tm = _pick_tile(n, 1024)
    cols = w.shape[1]
    in_specs = [pl.BlockSpec((1, tm, d), lambda b, i: (b, i, 0)),
                pl.BlockSpec((1, 6, d), lambda b, i: (b, 0, 0)),
                pl.BlockSpec((1, d), lambda b, i: (0, 0)),
                pl.BlockSpec((d, cols), lambda b, i: (0, 0), pipeline_mode=pl.Buffered(1))]
    args = [x, mod, g.reshape(1, d), w]
    if rope is not None:
        in_specs += [pl.BlockSpec((tm, LANES), lambda b, i: (i, 0))] * 3
        args += list(rope)
    widths = (Q_COLS, KV_COLS, MIX_COLS, MIX_COLS, SMALL_COLS)
    return pl.pallas_call(
        functools.partial(_inproj_kernel, use_rope=rope is not None),
        grid=(bx, n // tm),
        in_specs=in_specs,
        out_specs=[pl.BlockSpec((1, tm, wd), lambda b, i: (b, i, 0)) for wd in widths],
        out_shape=[jax.ShapeDtypeStruct((bx, n, wd), F32) for wd in widths],
        compiler_params=_cparams(("parallel", "parallel")),
        name="inproj_rope" if rope is not None else "inproj",
    )(*args)


def _attn_kernel(*refs, local, tq, nsub):
    if local:
        sink_ref, q_ref, kv_ref, kvc_ref, o_ref, km_ref, vt_ref, bias_ref, kmc_ref, vtc_ref = refs
    else:
        sink_ref, q_ref, kvc_ref, o_ref, kmc_ref, vtc_ref = refs
    npairs = ATTN_GROUP
    cols = npairs * tq
    span = 3 * ATTN_BLOCK
    lane = lax.broadcasted_iota(jnp.int32, (1, LANES), 1)
    head_lanes = (lane < HEAD_DIM, lane >= HEAD_DIM)
    row8 = lax.broadcasted_iota(jnp.int32, (8, LANES), 0)
    ones8 = jnp.where(row8 == 0, 1.0, 0.0).astype(F32)

    def prepare(src_ref, km, vt):
        ntiles = src_ref.shape[1] // LANES
        tiles = _group(ntiles, 4)

        def body(i, carry):
            rows = [pl.ds(pl.multiple_of((i * tiles + u) * LANES, LANES), LANES) for u in range(tiles)]
            k128 = [src_ref[0, r, 0:LANES] for r in rows]
            v_t = [src_ref[0, r, LANES:2 * LANES].T for r in rows]
            for u, r in enumerate(rows):
                for kvh in range(ATTN_KV_HEADS):
                    km[kvh, r, :] = jnp.where(head_lanes[kvh], k128[u], 0.0).astype(BF16)
                    vt[kvh, 0:HEAD_DIM, r] = v_t[u][kvh * HEAD_DIM:(kvh + 1) * HEAD_DIM]
                    vt[kvh, HEAD_DIM:M_AUG, r] = ones8
            return carry

        lax.fori_loop(0, ntiles // tiles, body, 0)

    @pl.when(pl.program_id(1) == 0)
    def _():
        prepare(kvc_ref, kmc_ref, vtc_ref)
        if local:
            prepare(kv_ref, km_ref, vt_ref)
            rel = lax.broadcasted_iota(jnp.int32, bias_ref.shape, 0) - 2 * ATTN_BLOCK
            qoff = lax.broadcasted_iota(jnp.int32, bias_ref.shape, 1) & (tq - 1)
            bias_ref[...] = jnp.where(jnp.abs(rel - qoff) <= WINDOW, 0.0, NEG_BIG)

    units = [(sb, kvh) for sb in range(nsub) for kvh in range(ATTN_KV_HEADS)]
    qall, start, band = {}, {}, {}
    for sb in range(nsub):
        q = q_ref[0, sb * tq:(sb + 1) * tq, :].astype(BF16)
        qall[sb] = jnp.concatenate([q[:, p * LANES:(p + 1) * LANES] for p in range(npairs)], axis=0)
        if local:
            j = pl.program_id(1) * nsub + sb
            start[sb] = pl.multiple_of(jnp.clip((j - 1) * ATTN_BLOCK, 0, kv_ref.shape[1] - span), ATTN_BLOCK)
            band[sb] = bias_ref[pl.ds(pl.multiple_of(start[sb] - j * ATTN_BLOCK + 2 * ATTN_BLOCK, ATTN_BLOCK),
                                      span), :]
    sink = [jnp.concatenate([jnp.full((1, tq), sink_ref[kvh * ATTN_GROUP + p] * LOG2E, F32)
                             for p in range(npairs)], axis=1) for kvh in range(ATTN_KV_HEADS)]
    s_ctx, s_loc, m, acc = {}, {}, {}, {}
    for sb, kvh in units:
        s_ctx[sb, kvh] = _dot_nt(kmc_ref[kvh], qall[sb])
        if local:
            s_loc[sb, kvh] = _dot_nt(km_ref[kvh, pl.ds(start[sb], span), :], qall[sb]) + band[sb]
    for u in units:
        m[u] = jnp.maximum(jnp.max(s_ctx[u], axis=0, keepdims=True), sink[u[1]])
        if local:
            m[u] = jnp.maximum(m[u], jnp.max(s_loc[u], axis=0, keepdims=True))
    for sb, kvh in units:
        u = (sb, kvh)
        acc[u] = _dot(vtc_ref[kvh].astype(BF16), jnp.exp2(s_ctx[u] - m[u]).astype(BF16))
        if local:
            acc[u] = acc[u] + _dot(vt_ref[kvh, :, pl.ds(start[sb], span)].astype(BF16),
                                   jnp.exp2(s_loc[u] - m[u]).astype(BF16))
    o_t = {u: acc[u][0:HEAD_DIM] / (acc[u][HEAD_DIM:HEAD_DIM + 1] + jnp.exp2(sink[u[1]] - m[u])) for u in units}
    for sb in range(nsub):
        for p in range(npairs):
            for c in range(tq // LANES):
                sl = slice(p * tq + c * LANES, p * tq + (c + 1) * LANES)
                tile = jnp.concatenate([o_t[sb, kvh][:, sl] for kvh in range(ATTN_KV_HEADS)], axis=0)
                r0 = sb * tq + c * LANES
                o_ref[0, r0:r0 + LANES, p * LANES:(p + 1) * LANES] = tile.T


def _attn_call(sink, q, kv, kvc):
    b, nq, _ = q.shape
    nc = kvc.shape[1]
    local = kv is not None
    tq = ATTN_BLOCK if local else nq
    nsub = ATTN_SUBBLOCKS if local else 1
    assert tq % LANES == 0 and nc % LANES == 0 and tq & (tq - 1) == 0 and nq % (nsub * tq) == 0
    in_specs = [pl.BlockSpec(memory_space=pltpu.SMEM),
                pl.BlockSpec((1, nsub * tq, Q_COLS), lambda bi, j: (bi, j, 0))]
    args = [sink, q]
    scratch = []
    if local:
        in_specs.append(pl.BlockSpec((1, nq, KV_COLS), lambda bi, j: (bi, 0, 0)))
        args.append(kv)
        scratch += [pltpu.VMEM((ATTN_KV_HEADS, nq, LANES), BF16), pltpu.VMEM((ATTN_KV_HEADS, M_AUG, nq), F32),
                    pltpu.VMEM((5 * ATTN_BLOCK, ATTN_GROUP * tq), F32)]
    in_specs.append(pl.BlockSpec((1, nc, KV_COLS), lambda bi, j: (bi, 0, 0)))
    args.append(kvc)
    scratch += [pltpu.VMEM((ATTN_KV_HEADS, nc, LANES), BF16), pltpu.VMEM((ATTN_KV_HEADS, M_AUG, nc), F32)]
    return pl.pallas_call(
        functools.partial(_attn_kernel, local=local, tq=tq, nsub=nsub),
        grid=(b, nq // (nsub * tq)),
        in_specs=in_specs,
        out_specs=pl.BlockSpec((1, nsub * tq, Q_COLS), lambda bi, j: (bi, j, 0)),
        out_shape=jax.ShapeDtypeStruct((b, nq, Q_COLS), F32),
        scratch_shapes=scratch,
        compiler_params=_cparams(("arbitrary", "arbitrary")),
        name="attn_window" if local else "attn_ctx",
    )(*args)


def _tri(n, rev):
    ri = lax.broadcasted_iota(jnp.int32, (n, n), 0)
    ci = lax.broadcasted_iota(jnp.int32, (n, n), 1)
    return ci >= ri if rev else ci <= ri


def _seq_specs(n):
    return [pl.BlockSpec((1, n, LANES), lambda b, hp, off=off: (b, 0, off + hp)) for off in (0, 2, 4, 6)]


def _group(nchunks, pref):
    g = min(pref, nchunks)
    assert nchunks % g == 0
    return g


def _mlstm_kernel(*refs, with_ctx_out):
    (qc_ref, kc_ref, vc_ref, oc_gate_ref, grc_ref,
     ql_ref, kl_ref, vl_ref, ol_gate_ref, grl_ref, bias_ref, g_ref) = refs[:12]
    if with_ctx_out:
        o_lat_ref, o_ctx_ref = refs[12:14]
        scratch = refs[14:]
    else:
        o_lat_ref, o_ctx_ref = refs[12], None
        scratch = refs[13:]
    w_ref, b_ref, ck_ref, tot_ref, mc_ref, mp_ref, kt_ref, cc_ref, vt_ref = scratch
    L = M_CHUNK
    ncc, ncl = qc_ref.shape[1] // L, ql_ref.shape[1] // L
    kt_c = -(-ncc // KT_CHUNKS)

    mask_t = (_tri(L, True), _tri(L, False))
    tri_f, tri_b = (mk.astype(BF16) for mk in mask_t)
    ones_m = jnp.ones((L, L), BF16)
    row8 = lax.broadcasted_iota(jnp.int32, (8, L), 0)
    rev8 = (row8 & 2) != 0
    ones8 = jnp.where(row8 == 0, 1.0, 0.0).astype(F32)

    def prep(gr_ref, v_ref, base, kt_base, v_base):
        nch = gr_ref.shape[3]
        li = (gr_ref[0, 0, 0] + bias_ref[0, 0]).reshape(nch * 8, L)
        lf = _log_sigmoid(gr_ref[0, 0, 1] + bias_ref[0, 1]).reshape(nch * 8, L)
        rev = (lax.broadcasted_iota(jnp.int32, li.shape, 0) & 2) != 0
        lane = lax.broadcasted_iota(jnp.int32, li.shape, 1)
        l1, l2, l3 = _split3(lf)
        scan = lambda m: _dot(l1, m) + (_dot(l2, m) + _dot(l3, m))
        b = jnp.where(rev, scan(tri_b), scan(tri_f))
        tot = scan(ones_m)
        g = tot - b + li
        mc = jnp.max(g, axis=1, keepdims=True)
        kap = li - b
        ckf, ckb = kap, kap
        s = 1
        while s < L:
            ckf = jnp.maximum(ckf, jnp.where(lane >= s, pltpu.roll(ckf, s, 1), NEG_BIG))
            ckb = jnp.maximum(ckb, jnp.where(lane < L - s, pltpu.roll(ckb, L - s, 1), NEG_BIG))
            s *= 2
        sl = pl.ds(base, nch)
        w_ref[sl] = jnp.exp(g - mc).reshape(nch, 8, L)
        b_ref[sl] = b.reshape(nch, 8, L)
        ck_ref[sl] = jnp.where(rev, ckb, ckf).reshape(nch, 8, L)
        tot_ref[sl] = tot.reshape(nch, 8, L)
        mc_ref[sl] = jnp.broadcast_to(mc, (nch * 8, L)).reshape(nch, 8, L)
        for gi in range(-(-nch // KT_CHUNKS)):
            blk = kap[gi * LANES:(gi + 1) * LANES]
            if blk.shape[0] < LANES:
                blk = jnp.concatenate([blk, jnp.zeros((LANES - blk.shape[0], L), F32)], axis=0)
            kt_ref[kt_base + gi] = blk.T

        tiles = _group(nch, 4)

        def vt_body(i, carry):
            blocks = [v_ref[0, pl.ds(pl.multiple_of((i * tiles + u) * L, L), L), :].T for u in range(tiles)]
            for u, blk in enumerate(blocks):
                vt_ref[:, pl.ds(pl.multiple_of(v_base + (i * tiles + u) * L, L), L)] = blk
            return carry

        lax.fori_loop(0, nch // tiles, vt_body, 0)

    def vaug(vt2, hh):
        return jnp.concatenate([vt2[hh * HEAD_DIM:(hh + 1) * HEAD_DIM], ones8], axis=0)

    def summarize(k_ref, base, v_base):
        nch = k_ref.shape[1] // L
        grp = _group(nch, M_GROUP)

        def body(t, carry):
            lhs, rhs = {}, {}
            for gi in range(grp):
                cl = t * grp + gi
                k2 = (k_ref[0, pl.ds(pl.multiple_of(cl * L, L), L), :] * (HEAD_DIM ** -0.5)).astype(BF16)
                vt2 = vt_ref[:, pl.ds(pl.multiple_of(v_base + cl * L, L), L)]
                w8 = w_ref[base + cl]
                for hh in range(2):
                    va = vaug(vt2, hh)
                    lhs[gi, hh] = jnp.concatenate([va * w8[d * 2 + hh:d * 2 + hh + 1] for d in range(2)],
                                                  axis=0).astype(BF16)
                    rhs[gi, hh] = k2[:, hh * HEAD_DIM:(hh + 1) * HEAD_DIM]
            cc = {key: _dot(lhs[key], rhs[key]) for key in lhs}
            for (gi, hh), val in cc.items():
                for d in range(2):
                    cc_ref[base + t * grp + gi, d * 2 + hh] = val[d * M_AUG:(d + 1) * M_AUG]
            return carry

        lax.fori_loop(0, nch // grp, body, 0)

    def scan_states(base, nch, state):
        def body(t, st):
            m8, cs = st[0], list(st[1:])
            cf, cb = base + t, base + nch - 1 - t
            tot8 = jnp.where(rev8, tot_ref[cb], tot_ref[cf])
            mc8 = jnp.where(rev8, mc_ref[cb], mc_ref[cf])
            m_new = jnp.maximum(tot8 + m8, mc8)
            s_old = jnp.exp(tot8 + m8 - m_new)
            s_new = jnp.exp(mc8 - m_new)
            mp_ref[cf, 0:2, :] = m8[0:2]
            mp_ref[cb, 2:4, :] = m8[2:4]
            for j in range(4):
                c = cf if j < 2 else cb
                summary = cc_ref[c, j]
                cc_ref[c, j] = cs[j]
                cs[j] = s_old[j:j + 1, 0:HEAD_DIM] * cs[j] + s_new[j:j + 1, 0:HEAD_DIM] * summary
            return (m_new, *cs)

        return lax.fori_loop(0, nch, body, state)

    def emit(q_ref, k_ref, gate_ref, o_ref, base, kt_base, v_base):
        nch = q_ref.shape[1] // L
        grp = _group(nch, M_GROUP)

        def body(t, carry):
            units = [(gi, hh) for gi in range(grp) for hh in range(2)]
            rows, vas, rho, alpha, floor, kcol, lhs, rhs = {}, {}, {}, {}, {}, {}, {}, {}
            for gi in range(grp):
                cl = t * grp + gi
                c = base + cl
                rows[gi] = pl.ds(pl.multiple_of(cl * L, L), L)
                q2 = q_ref[0, rows[gi], :].astype(BF16)
                k2 = k_ref[0, rows[gi], :] * (HEAD_DIM ** -0.5)
                vt2 = vt_ref[:, pl.ds(pl.multiple_of(v_base + cl * L, L), L)]
                mp8 = mp_ref[c]
                rho[gi] = -jnp.maximum(mp8, ck_ref[c])
                alpha[gi] = jnp.exp(mp8 + rho[gi])
                floor[gi] = jnp.exp(rho[gi] - b_ref[c])
                kt = kt_ref[kt_base + (cl >> 4)]
                kcol[gi] = pltpu.roll(kt, (LANES - (cl & (KT_CHUNKS - 1)) * 8) & (LANES - 1), 1)
                for hh in range(2):
                    ln = slice(hh * HEAD_DIM, (hh + 1) * HEAD_DIM)
                    vas[gi, hh] = vaug(vt2, hh).astype(BF16)
                    lhs[gi, hh] = jnp.concatenate([k2[:, ln], cc_ref[c, hh], cc_ref[c, 2 + hh]], axis=0).astype(BF16)
                    rhs[gi, hh] = q2[:, ln]
            prod = {u: _dot_nt(lhs[u], rhs[u]) for u in units}
            pt = {}
            for gi, hh in units:
                st = prod[gi, hh][0:L]
                for d in range(2):
                    j = d * 2 + hh
                    e = kcol[gi][:, j:j + 1] + rho[gi][j:j + 1, :]
                    pt[gi, hh, d] = (jnp.where(mask_t[d], jnp.exp(e), 0.0) * st).astype(BF16)
            pv = {key: _dot(vas[key[0], key[1]], val) for key, val in pt.items()}
            for gi in range(grp):
                halves = []
                for hh in range(2):
                    hsum = None
                    for d in range(2):
                        j = d * 2 + hh
                        lo = L + d * M_AUG
                        num = alpha[gi][j:j + 1, :] * prod[gi, hh][lo:lo + M_AUG] + pv[gi, hh, d]
                        h = num[0:HEAD_DIM] / jnp.maximum(jnp.abs(num[HEAD_DIM:HEAD_DIM + 1]), floor[gi][j:j + 1, :])
                        hsum = h if hsum is None else hsum + h
                    ms = jnp.sum(hsum * hsum, axis=0, keepdims=True) * (1.0 / HEAD_DIM)
                    halves.append(hsum * lax.rsqrt(ms + EPS))
                y = jnp.concatenate(halves, axis=0).T
                o_ref[0, rows[gi], :] = y * g_ref[...] * jax.nn.sigmoid(gate_ref[0, rows[gi], :])
            return carry

        lax.fori_loop(0, nch // grp, body, 0)

    n_ctx = qc_ref.shape[1]
    prep(grc_ref, vc_ref, 0, 0, 0)
    prep(grl_ref, vl_ref, ncc, kt_c, n_ctx)
    summarize(kc_ref, 0, 0)
    summarize(kl_ref, ncc, n_ctx)
    zero = (jnp.zeros((8, L), F32),) + (jnp.zeros((M_AUG, HEAD_DIM), F32),) * 4
    state = scan_states(0, ncc, zero)
    scan_states(ncc, ncl, state)
    if with_ctx_out:
        emit(qc_ref, kc_ref, oc_gate_ref, o_ctx_ref, 0, 0, 0)
    emit(ql_ref, kl_ref, ol_gate_ref, o_lat_ref, ncc, kt_c, n_ctx)


def _mlstm_gates(s):
    b, n, _ = s.shape
    g = s[:, :, :2 * N_MGATES].reshape(b, n, 2, 2, 2, 2)
    g = g.transpose(0, 4, 2, 3, 5, 1).reshape(b, 2, 2, 4, n // M_CHUNK, M_CHUNK).transpose(0, 1, 2, 4, 3, 5)
    return jnp.concatenate([g, g], axis=4)


def _mlstm_call(mc, sc, ml, sl, i_bias, f_bias, g, with_ctx_out):
    b, n, _ = ml.shape
    nc = mc.shape[1]
    assert n % M_CHUNK == 0 and nc % M_CHUNK == 0
    ncc, ncl = nc // M_CHUNK, n // M_CHUNK
    bias = jnp.stack([i_bias, f_bias]).astype(F32).reshape(2, 2, 2, 2).transpose(2, 0, 1, 3).reshape(2, 2, 4)
    bias = jnp.concatenate([bias, bias], axis=2)[..., None]
    gr_spec = lambda nch: pl.BlockSpec((1, 1, 2, nch, 8, M_CHUNK), lambda bi, hp: (bi, hp, 0, 0, 0, 0))
    in_specs = (_seq_specs(nc) + [gr_spec(ncc)] + _seq_specs(n) + [gr_spec(ncl)]
                + [pl.BlockSpec((1, 2, 8, 1), lambda bi, hp: (hp, 0, 0, 0)),
                   pl.BlockSpec((1, LANES), lambda bi, hp: (0, hp))])
    out_specs = [pl.BlockSpec((1, n, LANES), lambda bi, hp: (bi, 0, hp))]
    out_shape = [jax.ShapeDtypeStruct((b, n, M_HEADS * HEAD_DIM), F32)]
    if with_ctx_out:
        out_specs.append(pl.BlockSpec((1, nc, LANES), lambda bi, hp: (bi, 0, hp)))
        out_shape.append(jax.ShapeDtypeStruct((b, nc, M_HEADS * HEAD_DIM), F32))
    tot = ncc + ncl
    per_chunk = pltpu.VMEM((tot, 8, M_CHUNK), F32)
    scratch = [per_chunk] * 6 + [
        pltpu.VMEM((-(-ncc // KT_CHUNKS) + -(-ncl // KT_CHUNKS), M_CHUNK, LANES), F32),
        pltpu.VMEM((tot, 4, M_AUG, HEAD_DIM), F32),
        pltpu.VMEM((LANES, nc + n), F32)]
    outs = pl.pallas_call(
        functools.partial(_mlstm_kernel, with_ctx_out=with_ctx_out),
        grid=(b, M_HEADS // 2),
        in_specs=in_specs,
        out_specs=out_specs,
        out_shape=out_shape,
        scratch_shapes=scratch,
        compiler_params=_cparams(("parallel", "arbitrary")),
        name="mlstm",
    )(mc, mc, mc, mc, _mlstm_gates(sc), ml, ml, ml, ml, _mlstm_gates(sl), bias, g.reshape(1, -1))
    return (outs[0], outs[1]) if with_ctx_out else (outs[0], None)


def _head_norm_gate(hsum, g, gate):
    sq = hsum * hsum
    lane = lax.broadcasted_iota(jnp.int32, hsum.shape, 1)
    first = lane < HEAD_DIM
    s0 = jnp.sum(jnp.where(first, sq, 0.0), axis=-1, keepdims=True)
    s1 = jnp.sum(jnp.where(first, 0.0, sq), axis=-1, keepdims=True)
    ms = jnp.where(first, s0, s1) * (1.0 / HEAD_DIM)
    return hsum * lax.rsqrt(ms + EPS) * g * gate


def _seg_scan_sum(x, rev):
    rows = x.shape[0]
    pos = lax.broadcasted_iota(jnp.int32, x.shape, 0) & (CHUNK - 1)
    s = 1
    while s < CHUNK:
        if rev:
            x = x + jnp.where(pos < CHUNK - s, pltpu.roll(x, rows - s, 0), 0.0)
        else:
            x = x + jnp.where(pos >= s, pltpu.roll(x, s, 0), 0.0)
        s *= 2
    return x


def _gla_kernel(*refs, with_ctx_out):
    (qc_ref, kc_ref, vc_ref, oc_gate_ref, sc_ref,
     ql_ref, kl_ref, vl_ref, ol_gate_ref, sl_ref, wa_ref, ba_ref, g_ref) = refs[:13]
    if with_ctx_out:
        o_lat_ref, o_ctx_ref, bc_ref, u_ref, dec_ref = refs[13:]
    else:
        o_lat_ref, bc_ref, u_ref, dec_ref = refs[13:]
        o_ctx_ref = None
    ncc, ncl = qc_ref.shape[1] // CHUNK, ql_ref.shape[1] // CHUNK
    lo = 2 * N_MGATES
    mask_f, mask_b = _tri(CHUNK, False), _tri(CHUNK, True)
    first = lax.broadcasted_iota(jnp.int32, (HEAD_DIM, LANES), 1) < HEAD_DIM
    halves = (slice(0, HEAD_DIM), slice(HEAD_DIM, 2 * HEAD_DIM))
    silu = lambda t: t * jax.nn.sigmoid(t)

    def slab_of(t, grp):
        return pl.multiple_of(t * (grp * CHUNK), grp * CHUNK)

    def summarize(k_ref, v_ref, s_ref, base):
        nch = k_ref.shape[1] // CHUNK
        grp = _group(nch, GLA_GROUP)
        assert grp % 2 == 0

        def body(t, carry):
            r0 = slab_of(t, grp)
            slab = pl.ds(r0, grp * CHUNK)
            k3 = k_ref[0, slab, :].reshape(grp, CHUNK, LANES)
            v2 = v_ref[0, slab, :]
            kws, ends = [], []
            for d in range(2):
                lr = s_ref[0, slab, lo + d * GLA_RANK:lo + (d + 1) * GLA_RANK]
                glog = _log_sigmoid(_dot(lr.astype(BF16), wa_ref[d].astype(BF16)) + ba_ref[d]) * (1.0 / GLA_TAU)
                bc = _seg_scan_sum(glog, d == 1)
                bc_ref[d, pl.ds(pl.multiple_of(base * CHUNK + r0, CHUNK), grp * CHUNK), :] = bc
                bc3 = bc.reshape(grp, CHUNK, LANES)
                end = bc3[:, 0:1, :] if d == 1 else bc3[:, CHUNK - 1:CHUNK, :]
                kws.append((k3 * jnp.exp(end - bc3)).astype(BF16))
                ends.append(end)
            vts = [v2[p * LANES:(p + 1) * LANES, :].T.astype(BF16) for p in range(grp // 2)]
            us = {}
            for gi in range(grp):
                for hh in range(2):
                    vt = vts[gi // 2][halves[hh], halves[gi % 2]]
                    for d in range(2):
                        us[gi, hh, d] = _dot(vt, kws[d][gi][:, halves[hh]])
            for gi in range(grp):
                c = base + t * grp + gi
                for hh in range(2):
                    u_ref[hh, c] = jnp.concatenate([us[gi, hh, 0], us[gi, hh, 1]], axis=1)
                    end2 = jnp.concatenate([ends[0][gi][:, halves[hh]], ends[1][gi][:, halves[hh]]], axis=1)
                    dec_ref[hh, c] = jnp.broadcast_to(jnp.exp(end2), (8, LANES))
            return carry

        lax.fori_loop(0, nch // grp, body, 0)

    def scan_states(base, nch, state):
        def body(t, st):
            cf, cb = base + t, base + nch - 1 - t
            new = []
            for hh in range(2):
                u = jnp.where(first, u_ref[hh, cf], u_ref[hh, cb])
                dec = jnp.where(first[0:1], dec_ref[hh, cf][0:1], dec_ref[hh, cb][0:1])
                u_ref[hh, cf, :, 0:HEAD_DIM] = st[hh][:, 0:HEAD_DIM]
                u_ref[hh, cb, :, HEAD_DIM:] = st[hh][:, HEAD_DIM:]
                new.append(st[hh] * dec + u)
            return tuple(new)

        return lax.fori_loop(0, nch, body, state)

    def emit(q_ref, k_ref, v_ref, gate_ref, o_ref, base):
        nch = q_ref.shape[1] // CHUNK
        grp = _group(nch, GLA_GROUP)

        def body(t, carry):
            r0 = slab_of(t, grp)
            slab = pl.ds(r0, grp * CHUNK)
            q2 = q_ref[0, slab, :] * (HEAD_DIM ** -0.5)
            k2 = k_ref[0, slab, :]
            vb = v_ref[0, slab, :].astype(BF16)
            qd, kd = [], []
            for d in range(2):
                bc = bc_ref[d, pl.ds(pl.multiple_of(base * CHUNK + r0, CHUNK), grp * CHUNK), :]
                qd.append((q2 * jnp.exp(bc)).astype(BF16))
                kd.append((k2 * jnp.exp(-bc)).astype(BF16))
            units = [(gi, hh) for gi in range(grp) for hh in range(2)]
            rs = lambda gi: slice(gi * CHUNK, (gi + 1) * CHUNK)
            att = {}
            for gi, hh in units:
                a_f = jnp.where(mask_f, _dot_nt(qd[0][rs(gi), halves[hh]], kd[0][rs(gi), halves[hh]]), 0.0)
                a_b = jnp.where(mask_b, _dot_nt(qd[1][rs(gi), halves[hh]], kd[1][rs(gi), halves[hh]]), 0.0)
                att[gi, hh] = (a_f + a_b).astype(BF16)
            outs = []
            for gi in range(grp):
                c = base + t * grp + gi
                parts = []
                for hh in range(2):
                    qcat = jnp.concatenate([qd[0][rs(gi), halves[hh]], qd[1][rs(gi), halves[hh]]], axis=1)
                    parts.append(_dot(att[gi, hh], vb[rs(gi), halves[hh]])
                                 + _dot_nt(qcat, u_ref[hh, c].astype(BF16)))
                outs.append(jnp.concatenate(parts, axis=1))
            o2 = jnp.concatenate(outs, axis=0)
            o_ref[0, slab, :] = _head_norm_gate(o2, g_ref[...], silu(gate_ref[0, slab, :]))
            return carry

        lax.fori_loop(0, nch // grp, body, 0)

    summarize(kc_ref, vc_ref, sc_ref, 0)
    summarize(kl_ref, vl_ref, sl_ref, ncc)
    state = scan_states(0, ncc, (jnp.zeros((HEAD_DIM, LANES), F32),) * 2)
    scan_states(ncc, ncl, state)
    if with_ctx_out:
        emit(qc_ref, kc_ref, vc_ref, oc_gate_ref, o_ctx_ref, 0)
    emit(ql_ref, kl_ref, vl_ref, ol_gate_ref, o_lat_ref, ncc)


def _gla_call(gc, sc, gl, sl, wa2, ba, g, with_ctx_out):
    b, n, _ = gl.shape
    nc = gc.shape[1]
    tot = (nc + n) // CHUNK
    small = lambda nn: pl.BlockSpec((1, nn, SMALL_COLS), lambda bi, hp: (bi, 0, 0))
    in_specs = (_seq_specs(nc) + [small(nc)] + _seq_specs(n) + [small(n)]
                + [pl.BlockSpec((2, GLA_RANK, LANES), lambda bi, hp: (0, 0, hp)),
                   pl.BlockSpec((2, 1, LANES), lambda bi, hp: (0, 0, hp)),
                   pl.BlockSpec((1, LANES), lambda bi, hp: (0, hp))])
    out_specs = [pl.BlockSpec((1, n, LANES), lambda bi, hp: (bi, 0, hp))]
    out_shape = [jax.ShapeDtypeStruct((b, n, G_HEADS * HEAD_DIM), F32)]
    if with_ctx_out:
        out_specs.append(pl.BlockSpec((1, nc, LANES), lambda bi, hp: (bi, 0, hp)))
        out_shape.append(jax.ShapeDtypeStruct((b, nc, G_HEADS * HEAD_DIM), F32))
    scratch = [pltpu.VMEM((2, nc + n, LANES), F32),
               pltpu.VMEM((2, tot, HEAD_DIM, LANES), F32),
               pltpu.VMEM((2, tot, 8, LANES), F32)]
    outs = pl.pallas_call(
        functools.partial(_gla_kernel, with_ctx_out=with_ctx_out),
        grid=(b, G_HEADS // 2),
        in_specs=in_specs,
        out_specs=out_specs,
        out_shape=out_shape,
        scratch_shapes=scratch,
        compiler_params=_cparams(("parallel", "arbitrary")),
        name="gla",
    )(gc, gc, gc, gc, sc, gl, gl, gl, gl, sl, wa2, ba.reshape(2, 1, -1), g.reshape(1, -1))
    return (outs[0], outs[1]) if with_ctx_out else (outs[0], None)


def _outmlp_kernel(*refs, final, ff_tile):
    if final:
        x_ref, oa_ref, om_ref, og_ref, mod_ref, g_ref, wo_ref, w1_ref, w2_ref, fg_ref, y_ref = refs
    else:
        x_ref, oa_ref, om_ref, og_ref, mod_ref, g_ref, wo_ref, w1_ref, w2_ref, y_ref = refs
    mod = mod_ref[0]
    a_w, m_w = oa_ref.shape[2], om_ref.shape[2]
    o = (_dot(oa_ref[0].astype(BF16), wo_ref[0:a_w, :])
         + _dot(om_ref[0].astype(BF16), wo_ref[a_w:a_w + m_w, :])
         + _dot(og_ref[0].astype(BF16), wo_ref[a_w + m_w:, :]))
    x1 = x_ref[0] + mod[2:3] * o
    hb = _norm_mod(x1, g_ref[...], mod[3:4], mod[4:5]).astype(BF16)
    d_ff = w1_ref.shape[1]
    acc = jnp.zeros(x1.shape, F32)
    for j in range(d_ff // ff_tile):
        t = jnp.maximum(_dot(hb, w1_ref[:, j * ff_tile:(j + 1) * ff_tile]), 0.0)
        acc = acc + _dot((t * t).astype(BF16), w2_ref[j * ff_tile:(j + 1) * ff_tile, :])
    x2 = x1 + mod[5:6] * acc
    if final:
        ms = jnp.mean(x2 * x2, axis=-1, keepdims=True)
        x2 = x2 * lax.rsqrt(ms + EPS) * fg_ref[...]
    y_ref[0] = x2


def _outmlp_call(x, oa, om, og, mod, g, wo, w1, w2, final_g):
    bx, n, d = x.shape
    tm = _pick_tile(n, 512)
    d_ff = w1.shape[1]
    row = lambda wd: pl.BlockSpec((1, tm, wd), lambda b, i: (b, i, 0))
    whole = lambda arr: pl.BlockSpec(arr.shape, lambda b, i: (0,) * arr.ndim, pipeline_mode=pl.Buffered(1))
    g2 = g.reshape(1, d)
    in_specs = [row(d), row(oa.shape[2]), row(om.shape[2]), row(og.shape[2]),
                pl.BlockSpec((1, 6, d), lambda b, i: (b, 0, 0)), whole(g2), whole(wo), whole(w1), whole(w2)]
    args = [x, oa, om, og, mod, g2, wo, w1, w2]
    if final_g is not None:
        fg = final_g.reshape(1, d)
        in_specs.append(whole(fg))
        args.append(fg)
    return pl.pallas_call(
        functools.partial(_outmlp_kernel, final=final_g is not None, ff_tile=_pick_tile(d_ff, 1024)),
        grid=(bx, n // tm),
        in_specs=in_specs,
        out_specs=row(d),
        out_shape=jax.ShapeDtypeStruct((bx, n, d), F32),
        compiler_params=_cparams(("parallel", "parallel")),
        name="outproj_mlp",
    )(*args)


def _rope_tables(n):
    t = jnp.arange(n)
    inv = ROPE_BASE ** (-jnp.arange(0, ROPE_AXIS_DIM, 2, dtype=F32) / ROPE_AXIS_DIM)
    ang_r = (t // GRID_W).astype(F32)[:, None] * inv[None, :]
    ang_c = (t % GRID_W).astype(F32)[:, None] * inv[None, :]
    half = ROPE_AXIS_DIM // 2
    cos = jnp.concatenate([jnp.cos(ang_r)] * 2 + [jnp.cos(ang_c)] * 2, axis=1)
    zero = jnp.zeros((n, half), F32)
    sin_a = jnp.concatenate([zero, jnp.sin(ang_r), zero, jnp.sin(ang_c)], axis=1)
    sin_b = jnp.concatenate([-jnp.sin(ang_r), zero, -jnp.sin(ang_c), zero], axis=1)
    return tuple(jnp.tile(a, (1, LANES // HEAD_DIM)).astype(F32) for a in (cos, sin_a, sin_b))


def _pair_heads(a, axis):
    shape = a.shape
    a = a.reshape(shape[:axis] + (ATTN_KV_HEADS, ATTN_GROUP, HEAD_DIM) + shape[axis + 1:])
    return jnp.swapaxes(a, axis, axis + 1).reshape(shape)


def _permute_w_in(w_in):
    aq, ak, av, mq, mk, mv, mo, mi, mf, gq, gk, gv, gg, glr = jnp.split(
        w_in, np.cumsum([Q_COLS, 128, 128, 256, 256, 256, 256, 8, 8, 256, 256, 256, 256])[:13].tolist(), axis=-1)
    pad = jnp.zeros(w_in.shape[:-1] + (SMALL_COLS - 16 - 2 * GLA_RANK,), w_in.dtype)
    return jnp.concatenate([_pair_heads(aq, aq.ndim - 1), ak, av, mq, mk, mv, mo, gq, gk, gv, gg, mi, mf, glr, pad],
                           axis=-1).astype(BF16)


def _permute_w_out(w_out):
    return jnp.concatenate([_pair_heads(w_out[:, :Q_COLS], 1), w_out[:, Q_COLS:]], axis=1).astype(BF16)


def kernel(x, c, ctx, c_ctx, w_ada, b_ada, norm1_g, norm2_g, w_in, attn_sink, m_i_bias, m_f_bias, m_norm_g,
           g_wa2, g_ba, g_norm_g, w_out, w_mlp1, w_mlp2, final_g):
    B, N, D = x.shape
    Nc = ctx.shape[1]
    depth = w_ada.shape[0]
    rope = _rope_tables(N)
    rows = -(-(B + 1) // 8) * 8
    cc = jnp.zeros((rows, D), F32).at[:B].set(c).at[B].set(c_ctx)
    mods = _ada_call(cc, w_ada, b_ada)
    mods_x = mods[:, :B].reshape(depth, B, 6, D)
    mods_c = mods[:, B:B + 1].reshape(depth, 1, 6, D)
    w_in_p = _permute_w_in(w_in)
    wo, w1, w2 = _permute_w_out(w_out), w_mlp1.astype(BF16), w_mlp2.astype(BF16)

    xc = ctx
    for l in range(depth):
        last = l == depth - 1
        q, kv, ml, gl, sl = _inproj_call(x, mods_x[l], norm1_g[l], w_in_p[l], rope)
        ctx_parts = _inproj_call(xc.reshape(1, B * Nc, D), mods_c[l], norm1_g[l], w_in_p[l], None)
        qc, kvc, mc, gc, sc = (t.reshape(B, Nc, -1) for t in ctx_parts)
        oa = _attn_call(attn_sink[l], q, kv, kvc)
        om, omc = _mlstm_call(mc, sc, ml, sl, m_i_bias[l], m_f_bias[l], m_norm_g[l], not last)
        og, ogc = _gla_call(gc, sc, gl, sl, g_wa2[l], g_ba[l], g_norm_g[l], not last)
        x = _outmlp_call(x, oa, om, og, mods_x[l], norm2_g[l], wo[l], w1[l], w2[l], final_g if last else None)
        if not last:
            oac = _attn_call(attn_sink[l], qc, None, kvc)
            flat = lambda t: t.reshape(1, B * Nc, -1)
            xc = _outmlp_call(flat(xc), flat(oac), flat(omc), flat(ogc), mods_c[l], norm2_g[l],
                              wo[l], w1[l], w2[l], None).reshape(B, Nc, D)
    return x
```

```python
import functools

import numpy as np
import jax
import jax.numpy as jnp
from jax import lax
from jax.experimental import pallas as pl
from jax.experimental.pallas import tpu as pltpu

F32 = jnp.float32
BF16 = jnp.bfloat16

HEAD_DIM = 64
GRID_W = 64
ATTN_HEADS = 8
ATTN_KV_HEADS = 2
ATTN_GROUP = ATTN_HEADS // ATTN_KV_HEADS
WINDOW = 128
ATTN_BLOCK = 128
ATTN_SUBBLOCKS = 8
ROPE_BASE = 10000.0
ROPE_AXIS_DIM = HEAD_DIM // 2
M_HEADS = 4
G_HEADS = 4
CHUNK = 64
GLA_GROUP = 32
M_CHUNK = 128
M_GROUP = 8
M_AUG = HEAD_DIM + 8
KT_CHUNKS = 128 // 8
KT_SHIFT = KT_CHUNKS.bit_length() - 1
GLA_RANK = 16
GLA_TAU = 16.0
EPS = 1e-6
NEG_BIG = -1e30
LOG2E = 1.4426950408889634

LANES = 128
Q_COLS = ATTN_HEADS * HEAD_DIM
KV_COLS = 2 * ATTN_KV_HEADS * HEAD_DIM
MIX_COLS = 4 * M_HEADS * HEAD_DIM
SMALL_COLS = LANES
N_MGATES = 2 * 2 * 2
VMEM_LIMIT = 56 * 1024 * 1024


def _cparams(sem):
    return pltpu.CompilerParams(dimension_semantics=sem, vmem_limit_bytes=VMEM_LIMIT)


def _pick_tile(n, pref):
    t = pref
    while n % t:
        t //= 2
    return t


def _split3(a):
    a1 = a.astype(BF16)
    r1 = a - a1.astype(F32)
    a2 = r1.astype(BF16)
    a3 = (r1 - a2.astype(F32)).astype(BF16)
    return a1, a2, a3


def _dot(a, b):
    return jnp.dot(a, b, preferred_element_type=F32)


def _dot_nt(a, b):
    return lax.dot_general(a, b, (((1,), (1,)), ((), ())), preferred_element_type=F32)


def _dot_hi(a, b):
    a1, a2, _ = _split3(a)
    b1, b2, _ = _split3(b)
    return _dot(a1, b1) + (_dot(a1, b2) + _dot(a2, b1))


def _log_sigmoid(x):
    return jnp.minimum(x, 0.0) - jnp.log(1.0 + jnp.exp(-jnp.abs(x)))


def _ada_kernel(cc_ref, w_ref, b_ref, o_ref):
    cc = cc_ref[...]
    s = cc * jax.nn.sigmoid(cc)
    o_ref[0] = _dot_hi(s, w_ref[0]) + b_ref[0]


def _ada_call(cc, w_ada, b_ada):
    depth, d, six_d = w_ada.shape
    rows = cc.shape[0]
    tn = _pick_tile(six_d, 1024)
    return pl.pallas_call(
        _ada_kernel,
        grid=(depth, six_d // tn),
        in_specs=[pl.BlockSpec((rows, d), lambda l, j: (0, 0)),
                  pl.BlockSpec((1, d, tn), lambda l, j: (l, 0, j)),
                  pl.BlockSpec((1, 1, tn), lambda l, j: (l, 0, j))],
        out_specs=pl.BlockSpec((1, rows, tn), lambda l, j: (l, 0, j)),
        out_shape=jax.ShapeDtypeStruct((depth, rows, six_d), F32),
        compiler_params=_cparams(("arbitrary", "arbitrary")),
        name="ada_mod",
    )(cc, w_ada, b_ada.reshape(depth, 1, six_d))


def _norm_mod(x, g, shift, scale):
    ms = jnp.mean(x * x, axis=-1, keepdims=True)
    h = x * lax.rsqrt(ms + EPS) * g
    return h * (1.0 + scale) + shift


def _inproj_kernel(*refs, use_rope):
    if use_rope:
        x_ref, mod_ref, g_ref, w_ref, cos_ref, sa_ref, sb_ref, q_ref, kv_ref, m_ref, gl_ref, s_ref = refs
    else:
        x_ref, mod_ref, g_ref, w_ref, q_ref, kv_ref, m_ref, gl_ref, s_ref = refs
    mod = mod_ref[0]
    hb = _norm_mod(x_ref[0], g_ref[...], mod[0:1], mod[1:2]).astype(BF16)

    def proj(lo, width):
        return _dot(hb, w_ref[:, lo:lo + width])

    pa = proj(0, Q_COLS + KV_COLS)
    if use_rope:
        cos, sa, sb = cos_ref[...], sa_ref[...], sb_ref[...]
        segs = []
        for j in range((Q_COLS + KV_COLS // 2) // LANES):
            seg = pa[:, j * LANES:(j + 1) * LANES]
            segs.append(seg * cos + pltpu.roll(seg, ROPE_AXIS_DIM // 2, 1) * sa
                        + pltpu.roll(seg, LANES - ROPE_AXIS_DIM // 2, 1) * sb)
        qk = jnp.concatenate(segs, axis=1)
    else:
        qk = pa[:, :Q_COLS + KV_COLS // 2]
    q_ref[0] = qk[:, :Q_COLS] * (HEAD_DIM ** -0.5 * LOG2E)
    kv_ref[0, :, :KV_COLS // 2] = qk[:, Q_COLS:]
    kv_ref[0, :, KV_COLS // 2:] = pa[:, Q_COLS + KV_COLS // 2:]
    lo = Q_COLS + KV_COLS
    m_ref[0] = proj(lo, MIX_COLS)
    gl_ref[0] = proj(lo + MIX_COLS, MIX_COLS)
    s_ref[0] = proj(lo + 2 * MIX_COLS, SMALL_COLS)


def _inproj_call(x, mod, g, w, rope):
    bx, n, d = x.shape
    tm = _pick_tile(n, 1024)
    cols = w.shape[1]
    in_specs = [pl.BlockSpec((1, tm, d), lambda b, i: (b, i, 0)),
                pl.BlockSpec((1, 6, d), lambda b, i: (b, 0, 0)),
                pl.BlockSpec((1, d), lambda b, i: (0, 0)),
                pl.BlockSpec((d, cols), lambda b, i: (0, 0), pipeline_mode=pl.Buffered(1))]
    args = [x, mod, g.reshape(1, d), w]
    if rope is not None:
        in_specs += [pl.BlockSpec((tm, LANES), lambda b, i: (i, 0))] * 3
        args += list(rope)
    widths = (Q_COLS, KV_COLS, MIX_COLS, MIX_COLS, SMALL_COLS)
    return pl.pallas_call(
        functools.partial(_inproj_kernel, use_rope=rope is not None),
        grid=(bx, n // tm),
        in_specs=in_specs,
        out_specs=[pl.BlockSpec((1, tm, wd), lambda b, i: (b, i, 0)) for wd in widths],
        out_shape=[jax.ShapeDtypeStruct((bx, n, wd), F32) for wd in widths],
        compiler_params=_cparams(("parallel", "parallel")),
        name="inproj_rope" if rope is not None else "inproj",
    )(*args)


def _attn_kernel(*refs, local, tq, nsub):
    if local:
        sink_ref, q_ref, kv_ref, kvc_ref, o_ref, km_ref, vt_ref, bias_ref, kmc_ref, vtc_ref = refs
    else:
        sink_ref, q_ref, kvc_ref, o_ref, kmc_ref, vtc_ref = refs
    npairs = ATTN_GROUP
    cols = npairs * tq
    span = 3 * ATTN_BLOCK
    lane = lax.broadcasted_iota(jnp.int32, (1, LANES), 1)
    head_lanes = (lane < HEAD_DIM, lane >= HEAD_DIM)
    row8 = lax.broadcasted_iota(jnp.int32, (8, LANES), 0)
    ones8 = jnp.where(row8 == 0, 1.0, 0.0).astype(F32)

    def prepare(src_ref, km, vt):
        ntiles = src_ref.shape[1] // LANES
        tiles = _group(ntiles, 4)

        def body(i, carry):
            rows = [pl.ds(pl.multiple_of((i * tiles + u) * LANES, LANES), LANES) for u in range(tiles)]
            k128 = [src_ref[0, r, 0:LANES] for r in rows]
            v_t = [src_ref[0, r, LANES:2 * LANES].T for r in rows]
            for u, r in enumerate(rows):
                for kvh in range(ATTN_KV_HEADS):
                    km[kvh, r, :] = jnp.where(head_lanes[kvh], k128[u], 0.0).astype(BF16)
                    vt[kvh, 0:HEAD_DIM, r] = v_t[u][kvh * HEAD_DIM:(kvh + 1) * HEAD_DIM]
                    vt[kvh, HEAD_DIM:M_AUG, r] = ones8
            return carry

        lax.fori_loop(0, ntiles // tiles, body, 0)

    @pl.when(pl.program_id(1) == 0)
    def _():
        prepare(kvc_ref, kmc_ref, vtc_ref)
        if local:
            prepare(kv_ref, km_ref, vt_ref)
            rel = lax.broadcasted_iota(jnp.int32, bias_ref.shape, 0) - 2 * ATTN_BLOCK
            qoff = lax.broadcasted_iota(jnp.int32, bias_ref.shape, 1) & (tq - 1)
            bias_ref[...] = jnp.where(jnp.abs(rel - qoff) <= WINDOW, 0.0, NEG_BIG)

    units = [(sb, kvh) for sb in range(nsub) for kvh in range(ATTN_KV_HEADS)]
    qall, start, band = {}, {}, {}
    for sb in range(nsub):
        q = q_ref[0, sb * tq:(sb + 1) * tq, :].astype(BF16)
        qall[sb] = jnp.concatenate([q[:, p * LANES:(p + 1) * LANES] for p in range(npairs)], axis=0)
        if local:
            j = pl.program_id(1) * nsub + sb
            start[sb] = pl.multiple_of(jnp.clip((j - 1) * ATTN_BLOCK, 0, kv_ref.shape[1] - span), ATTN_BLOCK)
            band[sb] = bias_ref[pl.ds(pl.multiple_of(start[sb] - j * ATTN_BLOCK + 2 * ATTN_BLOCK, ATTN_BLOCK),
                                      span), :]
    sink = [jnp.concatenate([jnp.full((1, tq), sink_ref[kvh * ATTN_GROUP + p] * LOG2E, F32)
                             for p in range(npairs)], axis=1) for kvh in range(ATTN_KV_HEADS)]
    s_ctx, s_loc, m, acc = {}, {}, {}, {}
    for sb, kvh in units:
        s_ctx[sb, kvh] = _dot_nt(kmc_ref[kvh], qall[sb])
        if local:
            s_loc[sb, kvh] = _dot_nt(km_ref[kvh, pl.ds(start[sb], span), :], qall[sb]) + band[sb]
    for u in units:
        m[u] = jnp.maximum(jnp.max(s_ctx[u], axis=0, keepdims=True), sink[u[1]])
        if local:
            m[u] = jnp.maximum(m[u], jnp.max(s_loc[u], axis=0, keepdims=True))
    for sb, kvh in units:
        u = (sb, kvh)
        acc[u] = _dot(vtc_ref[kvh].astype(BF16), jnp.exp2(s_ctx[u] - m[u]).astype(BF16))
        if local:
            acc[u] = acc[u] + _dot(vt_ref[kvh, :, pl.ds(start[sb], span)].astype(BF16),
                                   jnp.exp2(s_loc[u] - m[u]).astype(BF16))
    o_t = {u: acc[u][0:HEAD_DIM] / (acc[u][HEAD_DIM:HEAD_DIM + 1] + jnp.exp2(sink[u[1]] - m[u])) for u in units}
    for sb in range(nsub):
        for p in range(npairs):
            for c in range(tq // LANES):
                sl = slice(p * tq + c * LANES, p * tq + (c + 1) * LANES)
                tile = jnp.concatenate([o_t[sb, kvh][:, sl] for kvh in range(ATTN_KV_HEADS)], axis=0)
                r0 = sb * tq + c * LANES
                o_ref[0, r0:r0 + LANES, p * LANES:(p + 1) * LANES] = tile.T


def _attn_call(sink, q, kv, kvc):
    b, nq, _ = q.shape
    nc = kvc.shape[1]
    local = kv is not None
    tq = ATTN_BLOCK if local else nq
    nsub = ATTN_SUBBLOCKS if local else 1
    assert tq % LANES == 0 and nc % LANES == 0 and tq & (tq - 1) == 0 and nq % (nsub * tq) == 0
    in_specs = [pl.BlockSpec(memory_space=pltpu.SMEM),
                pl.BlockSpec((1, nsub * tq, Q_COLS), lambda bi, j: (bi, j, 0))]
    args = [sink, q]
    scratch = []
    if local:
        in_specs.append(pl.BlockSpec((1, nq, KV_COLS), lambda bi, j: (bi, 0, 0)))
        args.append(kv)
        scratch += [pltpu.VMEM((ATTN_KV_HEADS, nq, LANES), BF16), pltpu.VMEM((ATTN_KV_HEADS, M_AUG, nq), F32),
                    pltpu.VMEM((5 * ATTN_BLOCK, ATTN_GROUP * tq), F32)]
    in_specs.append(pl.BlockSpec((1, nc, KV_COLS), lambda bi, j: (bi, 0, 0)))
    args.append(kvc)
    scratch += [pltpu.VMEM((ATTN_KV_HEADS, nc, LANES), BF16), pltpu.VMEM((ATTN_KV_HEADS, M_AUG, nc), F32)]
    return pl.pallas_call(
        functools.partial(_attn_kernel, local=local, tq=tq, nsub=nsub),
        grid=(b, nq // (nsub * tq)),
        in_specs=in_specs,
        out_specs=pl.BlockSpec((1, nsub * tq, Q_COLS), lambda bi, j: (bi, j, 0)),
        out_shape=jax.ShapeDtypeStruct((b, nq, Q_COLS), F32),
        scratch_shapes=scratch,
        compiler_params=_cparams(("arbitrary", "arbitrary")),
        name="attn_window" if local else "attn_ctx",
    )(*args)


def _tri(n, rev):
    ri = lax.broadcasted_iota(jnp.int32, (n, n), 0)
    ci = lax.broadcasted_iota(jnp.int32, (n, n), 1)
    return ci >= ri if rev else ci <= ri


def _seq_specs(n):
    return [pl.BlockSpec((1, n, LANES), lambda b, hp, off=off: (b, 0, off + hp)) for off in (0, 2, 4, 6)]


def _group(nchunks, pref):
    g = min(pref, nchunks)
    assert nchunks % g == 0
    return g


def _mlstm_kernel(*refs, with_ctx_out):
    (qc_ref, kc_ref, vc_ref, oc_gate_ref, grc_ref,
     ql_ref, kl_ref, vl_ref, ol_gate_ref, grl_ref, bias_ref, g_ref) = refs[:12]
    if with_ctx_out:
        o_lat_ref, o_ctx_ref = refs[12:14]
        scratch = refs[14:]
    else:
        o_lat_ref, o_ctx_ref = refs[12], None
        scratch = refs[13:]
    w_ref, b_ref, ck_ref, tot_ref, mc_ref, mp_ref, kt_ref, cc_ref, vt_ref = scratch
    L = M_CHUNK
    ncc, ncl = qc_ref.shape[1] // L, ql_ref.shape[1] // L
    kt_c = -(-ncc // KT_CHUNKS)

    mask_t = (_tri(L, True), _tri(L, False))
    tri_f, tri_b = (mk.astype(BF16) for mk in mask_t)
    ones_m = jnp.ones((L, L), BF16)
    row8 = lax.broadcasted_iota(jnp.int32, (8, L), 0)
    rev8 = (row8 & 2) != 0
    ones8 = jnp.where(row8 == 0, 1.0, 0.0).astype(F32)

    def prep(gr_ref, v_ref, base, kt_base, v_base):
        nch = gr_ref.shape[3]
        li = (gr_ref[0, 0, 0] + bias_ref[0, 0]).reshape(nch * 8, L)
        lf = _log_sigmoid(gr_ref[0, 0, 1] + bias_ref[0, 1]).reshape(nch * 8, L)
        rev = (lax.broadcasted_iota(jnp.int32, li.shape, 0) & 2) != 0
        lane = lax.broadcasted_iota(jnp.int32, li.shape, 1)
        l1, l2, l3 = _split3(lf)
        scan = lambda m: _dot(l1, m) + (_dot(l2, m) + _dot(l3, m))
        b = jnp.where(rev, scan(tri_b), scan(tri_f))
        tot = scan(ones_m)
        g = tot - b + li
        mc = jnp.max(g, axis=1, keepdims=True)
        kap = li - b
        ckf, ckb = kap, kap
        s = 1
        while s < L:
            ckf = jnp.maximum(ckf, jnp.where(lane >= s, pltpu.roll(ckf, s, 1), NEG_BIG))
            ckb = jnp.maximum(ckb, jnp.where(lane < L - s, pltpu.roll(ckb, L - s, 1), NEG_BIG))
            s *= 2
        sl = pl.ds(base, nch)
        w_ref[sl] = jnp.exp(g - mc).reshape(nch, 8, L)
        b_ref[sl] = b.reshape(nch, 8, L)
        ck_ref[sl] = jnp.where(rev, ckb, ckf).reshape(nch, 8, L)
        tot_ref[sl] = tot.reshape(nch, 8, L)
        mc_ref[sl] = jnp.broadcast_to(mc, (nch * 8, L)).reshape(nch, 8, L)
        for gi in range(-(-nch // KT_CHUNKS)):
            blk = kap[gi * LANES:(gi + 1) * LANES]
            if blk.shape[0] < LANES:
                blk = jnp.concatenate([blk, jnp.zeros((LANES - blk.shape[0], L), F32)], axis=0)
            kt_ref[kt_base + gi] = blk.T

        tiles = _group(nch, 4)

        def vt_body(i, carry):
            blocks = [v_ref[0, pl.ds(pl.multiple_of((i * tiles + u) * L, L), L), :].T for u in range(tiles)]
            for u, blk in enumerate(blocks):
                vt_ref[:, pl.ds(pl.multiple_of(v_base + (i * tiles + u) * L, L), L)] = blk
            return carry

        lax.fori_loop(0, nch // tiles, vt_body, 0)

    def vaug(vt2, hh):
        return jnp.concatenate([vt2[hh * HEAD_DIM:(hh + 1) * HEAD_DIM], ones8], axis=0)

    def summarize(k_ref, base, v_base):
        nch = k_ref.shape[1] // L
        grp = _group(nch, M_GROUP)

        def body(t, carry):
            lhs, rhs = {}, {}
            for gi in range(grp):
                cl = t * grp + gi
                k2 = (k_ref[0, pl.ds(pl.multiple_of(cl * L, L), L), :] * (HEAD_DIM ** -0.5)).astype(BF16)
                vt2 = vt_ref[:, pl.ds(pl.multiple_of(v_base + cl * L, L), L)]
                w8 = w_ref[base + cl]
                for hh in range(2):
                    va = vaug(vt2, hh)
                    lhs[gi, hh] = jnp.concatenate([va * w8[d * 2 + hh:d * 2 + hh + 1] for d in range(2)],
                                                  axis=0).astype(BF16)
                    rhs[gi, hh] = k2[:, hh * HEAD_DIM:(hh + 1) * HEAD_DIM]
            cc = {key: _dot(lhs[key], rhs[key]) for key in lhs}
            for (gi, hh), val in cc.items():
                for d in range(2):
                    cc_ref[base + t * grp + gi, d * 2 + hh] = val[d * M_AUG:(d + 1) * M_AUG]
            return carry

        lax.fori_loop(0, nch // grp, body, 0)

    def scan_states(base, nch, state):
        def body(t, st):
            m8, cs = st[0], list(st[1:])
            cf, cb = base + t, base + nch - 1 - t
            tot8 = jnp.where(rev8, tot_ref[cb], tot_ref[cf])
            mc8 = jnp.where(rev8, mc_ref[cb], mc_ref[cf])
            m_new = jnp.maximum(tot8 + m8, mc8)
            s_old = jnp.exp(tot8 + m8 - m_new)
            s_new = jnp.exp(mc8 - m_new)
            mp_ref[cf, 0:2, :] = m8[0:2]
            mp_ref[cb, 2:4, :] = m8[2:4]
            for j in range(4):
                c = cf if j < 2 else cb
                summary = cc_ref[c, j]
                cc_ref[c, j] = cs[j]
                cs[j] = s_old[j:j + 1, 0:HEAD_DIM] * cs[j] + s_new[j:j + 1, 0:HEAD_DIM] * summary
            return (m_new, *cs)

        return lax.fori_loop(0, nch, body, state)

    def emit(q_ref, k_ref, gate_ref, o_ref, base, kt_base, v_base):
        nch = q_ref.shape[1] // L
        grp = _group(nch, M_GROUP)

        def body(t, carry):
            units = [(gi, hh) for gi in range(grp) for hh in range(2)]
            rows, vas, rho, alpha, floor, kcol, lhs, rhs = {}, {}, {}, {}, {}, {}, {}, {}
            for gi in range(grp):
                cl = t * grp + gi
                c = base + cl
                rows[gi] = pl.ds(pl.multiple_of(cl * L, L), L)
                q2 = q_ref[0, rows[gi], :].astype(BF16)
                k2 = k_ref[0, rows[gi], :] * (HEAD_DIM ** -0.5)
                vt2 = vt_ref[:, pl.ds(pl.multiple_of(v_base + cl * L, L), L)]
                mp8 = mp_ref[c]
                rho[gi] = -jnp.maximum(mp8, ck_ref[c])
                alpha[gi] = jnp.exp(mp8 + rho[gi])
                floor[gi] = jnp.exp(rho[gi] - b_ref[c])
                kt = kt_ref[kt_base + (cl >> KT_SHIFT)]
                kcol[gi] = pltpu.roll(kt, (LANES - (cl & (KT_CHUNKS - 1)) * 8) & (LANES - 1), 1)
                for hh in range(2):
                    ln = slice(hh * HEAD_DIM, (hh + 1) * HEAD_DIM)
                    vas[gi, hh] = vaug(vt2, hh).astype(BF16)
                    lhs[gi, hh] = jnp.concatenate([k2[:, ln], cc_ref[c, hh], cc_ref[c, 2 + hh]], axis=0).astype(BF16)
                    rhs[gi, hh] = q2[:, ln]
            prod = {u: _dot_nt(lhs[u], rhs[u]) for u in units}
            pt = {}
            for gi, hh in units:
                st = prod[gi, hh][0:L]
                for d in range(2):
                    j = d * 2 + hh
                    e = kcol[gi][:, j:j + 1] + rho[gi][j:j + 1, :]
                    pt[gi, hh, d] = (jnp.where(mask_t[d], jnp.exp(e), 0.0) * st).astype(BF16)
            pv = {key: _dot(vas[key[0], key[1]], val) for key, val in pt.items()}
            for gi in range(grp):
                halves = []
                for hh in range(2):
                    hsum = None
                    for d in range(2):
                        j = d * 2 + hh
                        lo = L + d * M_AUG
                        num = alpha[gi][j:j + 1, :] * prod[gi, hh][lo:lo + M_AUG] + pv[gi, hh, d]
                        h = num[0:HEAD_DIM] / jnp.maximum(jnp.abs(num[HEAD_DIM:HEAD_DIM + 1]), floor[gi][j:j + 1, :])
                        hsum = h if hsum is None else hsum + h
                    ms = jnp.sum(hsum * hsum, axis=0, keepdims=True) * (1.0 / HEAD_DIM)
                    halves.append(hsum * lax.rsqrt(ms + EPS))
                y = jnp.concatenate(halves, axis=0).T
                o_ref[0, rows[gi], :] = y * g_ref[...] * jax.nn.sigmoid(gate_ref[0, rows[gi], :])
            return carry

        lax.fori_loop(0, nch // grp, body, 0)

    n_ctx = qc_ref.shape[1]
    prep(grc_ref, vc_ref, 0, 0, 0)
    prep(grl_ref, vl_ref, ncc, kt_c, n_ctx)
    summarize(kc_ref, 0, 0)
    summarize(kl_ref, ncc, n_ctx)
    zero = (jnp.zeros((8, L), F32),) + (jnp.zeros((M_AUG, HEAD_DIM), F32),) * 4
    state = scan_states(0, ncc, zero)
    scan_states(ncc, ncl, state)
    if with_ctx_out:
        emit(qc_ref, kc_ref, oc_gate_ref, o_ctx_ref, 0, 0, 0)
    emit(ql_ref, kl_ref, ol_gate_ref, o_lat_ref, ncc, kt_c, n_ctx)


def _mlstm_gates(s):
    b, n, _ = s.shape
    g = s[:, :, :2 * N_MGATES].reshape(b, n, 2, 2, 2, 2)
    g = g.transpose(0, 4, 2, 3, 5, 1).reshape(b, 2, 2, 4, n // M_CHUNK, M_CHUNK).transpose(0, 1, 2, 4, 3, 5)
    return jnp.concatenate([g, g], axis=4)


def _mlstm_call(mc, sc, ml, sl, i_bias, f_bias, g, with_ctx_out):
    b, n, _ = ml.shape
    nc = mc.shape[1]
    assert n % M_CHUNK == 0 and nc % M_CHUNK == 0
    ncc, ncl = nc // M_CHUNK, n // M_CHUNK
    bias = jnp.stack([i_bias, f_bias]).astype(F32).reshape(2, 2, 2, 2).transpose(2, 0, 1, 3).reshape(2, 2, 4)
    bias = jnp.concatenate([bias, bias], axis=2)[..., None]
    gr_spec = lambda nch: pl.BlockSpec((1, 1, 2, nch, 8, M_CHUNK), lambda bi, hp: (bi, hp, 0, 0, 0, 0))
    in_specs = (_seq_specs(nc) + [gr_spec(ncc)] + _seq_specs(n) + [gr_spec(ncl)]
                + [pl.BlockSpec((1, 2, 8, 1), lambda bi, hp: (hp, 0, 0, 0)),
                   pl.BlockSpec((1, LANES), lambda bi, hp: (0, hp))])
    out_specs = [pl.BlockSpec((1, n, LANES), lambda bi, hp: (bi, 0, hp))]
    out_shape = [jax.ShapeDtypeStruct((b, n, M_HEADS * HEAD_DIM), F32)]
    if with_ctx_out:
        out_specs.append(pl.BlockSpec((1, nc, LANES), lambda bi, hp: (bi, 0, hp)))
        out_shape.append(jax.ShapeDtypeStruct((b, nc, M_HEADS * HEAD_DIM), F32))
    tot = ncc + ncl
    per_chunk = pltpu.VMEM((tot, 8, M_CHUNK), F32)
    scratch = [per_chunk] * 6 + [
        pltpu.VMEM((-(-ncc // KT_CHUNKS) + -(-ncl // KT_CHUNKS), M_CHUNK, LANES), F32),
        pltpu.VMEM((tot, 4, M_AUG, HEAD_DIM), F32),
        pltpu.VMEM((LANES, nc + n), F32)]
    outs = pl.pallas_call(
        functools.partial(_mlstm_kernel, with_ctx_out=with_ctx_out),
        grid=(b, M_HEADS // 2),
        in_specs=in_specs,
        out_specs=out_specs,
        out_shape=out_shape,
        scratch_shapes=scratch,
        compiler_params=_cparams(("parallel", "arbitrary")),
        name="mlstm",
    )(mc, mc, mc, mc, _mlstm_gates(sc), ml, ml, ml, ml, _mlstm_gates(sl), bias, g.reshape(1, -1))
    return (outs[0], outs[1]) if with_ctx_out else (outs[0], None)


def _head_norm_gate(hsum, g, gate):
    sq = hsum * hsum
    lane = lax.broadcasted_iota(jnp.int32, hsum.shape, 1)
    first = lane < HEAD_DIM
    s0 = jnp.sum(jnp.where(first, sq, 0.0), axis=-1, keepdims=True)
    s1 = jnp.sum(jnp.where(first, 0.0, sq), axis=-1, keepdims=True)
    ms = jnp.where(first, s0, s1) * (1.0 / HEAD_DIM)
    return hsum * lax.rsqrt(ms + EPS) * g * gate


def _seg_scan_sum(x, rev):
    rows = x.shape[0]
    pos = lax.broadcasted_iota(jnp.int32, x.shape, 0) & (CHUNK - 1)
    s = 1
    while s < CHUNK:
        if rev:
            x = x + jnp.where(pos < CHUNK - s, pltpu.roll(x, rows - s, 0), 0.0)
        else:
            x = x + jnp.where(pos >= s, pltpu.roll(x, s, 0), 0.0)
        s *= 2
    return x


def _gla_kernel(*refs, with_ctx_out):
    (qc_ref, kc_ref, vc_ref, oc_gate_ref, sc_ref,
     ql_ref, kl_ref, vl_ref, ol_gate_ref, sl_ref, wa_ref, ba_ref, g_ref) = refs[:13]
    if with_ctx_out:
        o_lat_ref, o_ctx_ref, bc_ref, u_ref, dec_ref = refs[13:]
    else:
        o_lat_ref, bc_ref, u_ref, dec_ref = refs[13:]
        o_ctx_ref = None
    ncc, ncl = qc_ref.shape[1] // CHUNK, ql_ref.shape[1] // CHUNK
    lo = 2 * N_MGATES
    mask_f, mask_b = _tri(CHUNK, False), _tri(CHUNK, True)
    first = lax.broadcasted_iota(jnp.int32, (HEAD_DIM, LANES), 1) < HEAD_DIM
    halves = (slice(0, HEAD_DIM), slice(HEAD_DIM, 2 * HEAD_DIM))
    silu = lambda t: t * jax.nn.sigmoid(t)

    def slab_of(t, grp):
        return pl.multiple_of(t * (grp * CHUNK), grp * CHUNK)

    def summarize(k_ref, v_ref, s_ref, base):
        nch = k_ref.shape[1] // CHUNK
        grp = _group(nch, GLA_GROUP)
        assert grp % 2 == 0

        def body(t, carry):
            r0 = slab_of(t, grp)
            slab = pl.ds(r0, grp * CHUNK)
            k3 = k_ref[0, slab, :].reshape(grp, CHUNK, LANES)
            v2 = v_ref[0, slab, :]
            kws, ends = [], []
            for d in range(2):
                lr = s_ref[0, slab, lo + d * GLA_RANK:lo + (d + 1) * GLA_RANK]
                glog = _log_sigmoid(_dot(lr.astype(BF16), wa_ref[d].astype(BF16)) + ba_ref[d]) * (1.0 / GLA_TAU)
                bc = _seg_scan_sum(glog, d == 1)
                bc_ref[d, pl.ds(pl.multiple_of(base * CHUNK + r0, CHUNK), grp * CHUNK), :] = bc
                bc3 = bc.reshape(grp, CHUNK, LANES)
                end = bc3[:, 0:1, :] if d == 1 else bc3[:, CHUNK - 1:CHUNK, :]
                kws.append((k3 * jnp.exp(end - bc3)).astype(BF16))
                ends.append(end)
            vts = [v2[p * LANES:(p + 1) * LANES, :].T.astype(BF16) for p in range(grp // 2)]
            us = {}
            for gi in range(grp):
                for hh in range(2):
                    vt = vts[gi // 2][halves[hh], halves[gi % 2]]
                    for d in range(2):
                        us[gi, hh, d] = _dot(vt, kws[d][gi][:, halves[hh]])
            for gi in range(grp):
                c = base + t * grp + gi
                for hh in range(2):
                    u_ref[hh, c] = jnp.concatenate([us[gi, hh, 0], us[gi, hh, 1]], axis=1)
                    end2 = jnp.concatenate([ends[0][gi][:, halves[hh]], ends[1][gi][:, halves[hh]]], axis=1)
                    dec_ref[hh, c] = jnp.broadcast_to(jnp.exp(end2), (8, LANES))
            return carry

        lax.fori_loop(0, nch // grp, body, 0)

    def scan_states(base, nch, state):
        def body(t, st):
            cf, cb = base + t, base + nch - 1 - t
            new = []
            for hh in range(2):
                u = jnp.where(first, u_ref[hh, cf], u_ref[hh, cb])
                dec = jnp.where(first[0:1], dec_ref[hh, cf][0:1], dec_ref[hh, cb][0:1])
                u_ref[hh, cf, :, 0:HEAD_DIM] = st[hh][:, 0:HEAD_DIM]
                u_ref[hh, cb, :, HEAD_DIM:] = st[hh][:, HEAD_DIM:]
                new.append(st[hh] * dec + u)
            return tuple(new)

        return lax.fori_loop(0, nch, body, state)

    def emit(q_ref, k_ref, v_ref, gate_ref, o_ref, base):
        nch = q_ref.shape[1] // CHUNK
        grp = _group(nch, GLA_GROUP)

        def body(t, carry):
            r0 = slab_of(t, grp)
            slab = pl.ds(r0, grp * CHUNK)
            q2 = q_ref[0, slab, :] * (HEAD_DIM ** -0.5)
            k2 = k_ref[0, slab, :]
            vb = v_ref[0, slab, :].astype(BF16)
            qd, kd = [], []
            for d in range(2):
                bc = bc_ref[d, pl.ds(pl.multiple_of(base * CHUNK + r0, CHUNK), grp * CHUNK), :]
                qd.append((q2 * jnp.exp(bc)).astype(BF16))
                kd.append((k2 * jnp.exp(-bc)).astype(BF16))
            units = [(gi, hh) for gi in range(grp) for hh in range(2)]
            rs = lambda gi: slice(gi * CHUNK, (gi + 1) * CHUNK)
            att = {}
            for gi, hh in units:
                a_f = jnp.where(mask_f, _dot_nt(qd[0][rs(gi), halves[hh]], kd[0][rs(gi), halves[hh]]), 0.0)
                a_b = jnp.where(mask_b, _dot_nt(qd[1][rs(gi), halves[hh]], kd[1][rs(gi), halves[hh]]), 0.0)
                att[gi, hh] = (a_f + a_b).astype(BF16)
            outs = []
            for gi in range(grp):
                c = base + t * grp + gi
                parts = []
                for hh in range(2):
                    qcat = jnp.concatenate([qd[0][rs(gi), halves[hh]], qd[1][rs(gi), halves[hh]]], axis=1)
                    parts.append(_dot(att[gi, hh], vb[rs(gi), halves[hh]])
                                 + _dot_nt(qcat, u_ref[hh, c].astype(BF16)))
                outs.append(jnp.concatenate(parts, axis=1))
            o2 = jnp.concatenate(outs, axis=0)
            o_ref[0, slab, :] = _head_norm_gate(o2, g_ref[...], silu(gate_ref[0, slab, :]))
            return carry

        lax.fori_loop(0, nch // grp, body, 0)

    summarize(kc_ref, vc_ref, sc_ref, 0)
    summarize(kl_ref, vl_ref, sl_ref, ncc)
    state = scan_states(0, ncc, (jnp.zeros((HEAD_DIM, LANES), F32),) * 2)
    scan_states(ncc, ncl, state)
    if with_ctx_out:
        emit(qc_ref, kc_ref, vc_ref, oc_gate_ref, o_ctx_ref, 0)
    emit(ql_ref, kl_ref, vl_ref, ol_gate_ref, o_lat_ref, ncc)


def _gla_call(gc, sc, gl, sl, wa2, ba, g, with_ctx_out):
    b, n, _ = gl.shape
    nc = gc.shape[1]
    tot = (nc + n) // CHUNK
    small = lambda nn: pl.BlockSpec((1, nn, SMALL_COLS), lambda bi, hp: (bi, 0, 0))
    in_specs = (_seq_specs(nc) + [small(nc)] + _seq_specs(n) + [small(n)]
                + [pl.BlockSpec((2, GLA_RANK, LANES), lambda bi, hp: (0, 0, hp)),
                   pl.BlockSpec((2, 1, LANES), lambda bi, hp: (0, 0, hp)),
                   pl.BlockSpec((1, LANES), lambda bi, hp: (0, hp))])
    out_specs = [pl.BlockSpec((1, n, LANES), lambda bi, hp: (bi, 0, hp))]
    out_shape = [jax.ShapeDtypeStruct((b, n, G_HEADS * HEAD_DIM), F32)]
    if with_ctx_out:
        out_specs.append(pl.BlockSpec((1, nc, LANES), lambda bi, hp: (bi, 0, hp)))
        out_shape.append(jax.ShapeDtypeStruct((b, nc, G_HEADS * HEAD_DIM), F32))
    scratch = [pltpu.VMEM((2, nc + n, LANES), F32),
               pltpu.VMEM((2, tot, HEAD_DIM, LANES), F32),
               pltpu.VMEM((2, tot, 8, LANES), F32)]
    outs = pl.pallas_call(
        functools.partial(_gla_kernel, with_ctx_out=with_ctx_out),
        grid=(b, G_HEADS // 2),
        in_specs=in_specs,
        out_specs=out_specs,
        out_shape=out_shape,
        scratch_shapes=scratch,
        compiler_params=_cparams(("parallel", "arbitrary")),
        name="gla",
    )(gc, gc, gc, gc, sc, gl, gl, gl, gl, sl, wa2, ba.reshape(2, 1, -1), g.reshape(1, -1))
    return (outs[0], outs[1]) if with_ctx_out else (outs[0], None)


def _outmlp_kernel(*refs, final, ff_tile):
    if final:
        x_ref, oa_ref, om_ref, og_ref, mod_ref, g_ref, wo_ref, w1_ref, w2_ref, fg_ref, y_ref = refs
    else:
        x_ref, oa_ref, om_ref, og_ref, mod_ref, g_ref, wo_ref, w1_ref, w2_ref, y_ref = refs
    mod = mod_ref[0]
    a_w, m_w = oa_ref.shape[2], om_ref.shape[2]
    o = (_dot(oa_ref[0].astype(BF16), wo_ref[0:a_w, :])
         + _dot(om_ref[0].astype(BF16), wo_ref[a_w:a_w + m_w, :])
         + _dot(og_ref[0].astype(BF16), wo_ref[a_w + m_w:, :]))
    x1 = x_ref[0] + mod[2:3] * o
    hb = _norm_mod(x1, g_ref[...], mod[3:4], mod[4:5]).astype(BF16)
    d_ff = w1_ref.shape[1]
    acc = jnp.zeros(x1.shape, F32)
    for j in range(d_ff // ff_tile):
        t = jnp.maximum(_dot(hb, w1_ref[:, j * ff_tile:(j + 1) * ff_tile]), 0.0)
        acc = acc + _dot((t * t).astype(BF16), w2_ref[j * ff_tile:(j + 1) * ff_tile, :])
    x2 = x1 + mod[5:6] * acc
    if final:
        ms = jnp.mean(x2 * x2, axis=-1, keepdims=True)
        x2 = x2 * lax.rsqrt(ms + EPS) * fg_ref[...]
    y_ref[0] = x2


def _outmlp_call(x, oa, om, og, mod, g, wo, w1, w2, final_g):
    bx, n, d = x.shape
    tm = _pick_tile(n, 512)
    d_ff = w1.shape[1]
    row = lambda wd: pl.BlockSpec((1, tm, wd), lambda b, i: (b, i, 0))
    whole = lambda arr: pl.BlockSpec(arr.shape, lambda b, i: (0,) * arr.ndim, pipeline_mode=pl.Buffered(1))
    g2 = g.reshape(1, d)
    in_specs = [row(d), row(oa.shape[2]), row(om.shape[2]), row(og.shape[2]),
                pl.BlockSpec((1, 6, d), lambda b, i: (b, 0, 0)), whole(g2), whole(wo), whole(w1), whole(w2)]
    args = [x, oa, om, og, mod, g2, wo, w1, w2]
    if final_g is not None:
        fg = final_g.reshape(1, d)
        in_specs.append(whole(fg))
        args.append(fg)
    return pl.pallas_call(
        functools.partial(_outmlp_kernel, final=final_g is not None, ff_tile=_pick_tile(d_ff, 1024)),
        grid=(bx, n // tm),
        in_specs=in_specs,
        out_specs=row(d),
        out_shape=jax.ShapeDtypeStruct((bx, n, d), F32),
        compiler_params=_cparams(("parallel", "parallel")),
        name="outproj_mlp",
    )(*args)


def _rope_tables(n):
    t = jnp.arange(n)
    inv = ROPE_BASE ** (-jnp.arange(0, ROPE_AXIS_DIM, 2, dtype=F32) / ROPE_AXIS_DIM)
    ang_r = (t // GRID_W).astype(F32)[:, None] * inv[None, :]
    ang_c = (t % GRID_W).astype(F32)[:, None] * inv[None, :]
    half = ROPE_AXIS_DIM // 2
    cos = jnp.concatenate([jnp.cos(ang_r)] * 2 + [jnp.cos(ang_c)] * 2, axis=1)
    zero = jnp.zeros((n, half), F32)
    sin_a = jnp.concatenate([zero, jnp.sin(ang_r), zero, jnp.sin(ang_c)], axis=1)
    sin_b = jnp.concatenate([-jnp.sin(ang_r), zero, -jnp.sin(ang_c), zero], axis=1)
    return tuple(jnp.tile(a, (1, LANES // HEAD_DIM)).astype(F32) for a in (cos, sin_a, sin_b))


def _pair_heads(a, axis):
    shape = a.shape
    a = a.reshape(shape[:axis] + (ATTN_KV_HEADS, ATTN_GROUP, HEAD_DIM) + shape[axis + 1:])
    return jnp.swapaxes(a, axis, axis + 1).reshape(shape)


def _permute_w_in(w_in):
    aq, ak, av, mq, mk, mv, mo, mi, mf, gq, gk, gv, gg, glr = jnp.split(
        w_in, np.cumsum([Q_COLS, 128, 128, 256, 256, 256, 256, 8, 8, 256, 256, 256, 256])[:13].tolist(), axis=-1)
    pad = jnp.zeros(w_in.shape[:-1] + (SMALL_COLS - 16 - 2 * GLA_RANK,), w_in.dtype)
    return jnp.concatenate([_pair_heads(aq, aq.ndim - 1), ak, av, mq, mk, mv, mo, gq, gk, gv, gg, mi, mf, glr, pad],
                           axis=-1).astype(BF16)


def _permute_w_out(w_out):
    return jnp.concatenate([_pair_heads(w_out[:, :Q_COLS], 1), w_out[:, Q_COLS:]], axis=1).astype(BF16)


def kernel(x, c, ctx, c_ctx, w_ada, b_ada, norm1_g, norm2_g, w_in, attn_sink, m_i_bias, m_f_bias, m_norm_g,
           g_wa2, g_ba, g_norm_g, w_out, w_mlp1, w_mlp2, final_g):
    B, N, D = x.shape
    Nc = ctx.shape[1]
    depth = w_ada.shape[0]
    rope = _rope_tables(N)
    rows = -(-(B + 1) // 8) * 8
    cc = jnp.zeros((rows, D), F32).at[:B].set(c).at[B].set(c_ctx)
    mods = _ada_call(cc, w_ada, b_ada)
    mods_x = mods[:, :B].reshape(depth, B, 6, D)
    mods_c = mods[:, B:B + 1].reshape(depth, 1, 6, D)
    w_in_p = _permute_w_in(w_in)
    wo, w1, w2 = _permute_w_out(w_out), w_mlp1.astype(BF16), w_mlp2.astype(BF16)

    xc = ctx
    for l in range(depth):
        last = l == depth - 1
        q, kv, ml, gl, sl = _inproj_call(x, mods_x[l], norm1_g[l], w_in_p[l], rope)
        ctx_parts = _inproj_call(xc.reshape(1, B * Nc, D), mods_c[l], norm1_g[l], w_in_p[l], None)
        qc, kvc, mc, gc, sc = (t.reshape(B, Nc, -1) for t in ctx_parts)
        oa = _attn_call(attn_sink[l], q, kv, kvc)
        om, omc = _mlstm_call(mc, sc, ml, sl, m_i_bias[l], m_f_bias[l], m_norm_g[l], not last)
        og, ogc = _gla_call(gc, sc, gl, sl, g_wa2[l], g_ba[l], g_norm_g[l], not last)
        x = _outmlp_call(x, oa, om, og, mods_x[l], norm2_g[l], wo[l], w1[l], w2[l], final_g if last else None)
        if not last:
            oac = _attn_call(attn_sink[l], qc, None, kvc)
            flat = lambda t: t.reshape(1, B * Nc, -1)
            xc = _outmlp_call(flat(xc), flat(oac), flat(omc), flat(ogc), mods_c[l], norm2_g[l],
                              wo[l], w1[l], w2[l], None).reshape(B, Nc, D)
    return x
```

```python
import functools

import numpy as np
import jax
import jax.numpy as jnp
from jax import lax
from jax.experimental import pallas as pl
from jax.experimental.pallas import tpu as pltpu

F32 = jnp.float32
BF16 = jnp.bfloat16

HEAD_DIM = 64
GRID_W = 64
ATTN_HEADS = 8
ATTN_KV_HEADS = 2
ATTN_GROUP = ATTN_HEADS // ATTN_KV_HEADS
WINDOW = 128
ATTN_BLOCK = 128
ATTN_SUBBLOCKS = 8
ROPE_BASE = 10000.0
ROPE_AXIS_DIM = HEAD_DIM // 2
M_HEADS = 4
G_HEADS = 4
CHUNK = 64
GLA_GROUP = 32
M_CHUNK = 128
M_GROUP = 16
M_AUG = HEAD_DIM + 8
KT_CHUNKS = 128 // 8
KT_SHIFT = KT_CHUNKS.bit_length() - 1
GLA_RANK = 16
GLA_TAU = 16.0
EPS = 1e-6
NEG_BIG = -1e30
LOG2E = 1.4426950408889634

LANES = 128
Q_COLS = ATTN_HEADS * HEAD_DIM
KV_COLS = 2 * ATTN_KV_HEADS * HEAD_DIM
MIX_COLS = 4 * M_HEADS * HEAD_DIM
SMALL_COLS = LANES
N_MGATES = 2 * 2 * 2
VMEM_LIMIT = 56 * 1024 * 1024


def _cparams(sem):
    return pltpu.CompilerParams(dimension_semantics=sem, vmem_limit_bytes=VMEM_LIMIT)


def _pick_tile(n, pref):
    t = pref
    while n % t:
        t //= 2
    return t


def _split3(a):
    a1 = a.astype(BF16)
    r1 = a - a1.astype(F32)
    a2 = r1.astype(BF16)
    a3 = (r1 - a2.astype(F32)).astype(BF16)
    return a1, a2, a3


def _dot(a, b):
    return jnp.dot(a, b, preferred_element_type=F32)


def _dot_nt(a, b):
    return lax.dot_general(a, b, (((1,), (1,)), ((), ())), preferred_element_type=F32)


def _dot_hi(a, b):
    a1, a2, _ = _split3(a)
    b1, b2, _ = _split3(b)
    return _dot(a1, b1) + (_dot(a1, b2) + _dot(a2, b1))


def _log_sigmoid(x):
    return jnp.minimum(x, 0.0) - jnp.log(1.0 + jnp.exp(-jnp.abs(x)))


def _ada_kernel(cc_ref, w_ref, b_ref, o_ref):
    cc = cc_ref[...]
    s = cc * jax.nn.sigmoid(cc)
    o_ref[0] = _dot_hi(s, w_ref[0]) + b_ref[0]


def _ada_call(cc, w_ada, b_ada):
    depth, d, six_d = w_ada.shape
    rows = cc.shape[0]
    tn = _pick_tile(six_d, 1024)
    return pl.pallas_call(
        _ada_kernel,
        grid=(depth, six_d // tn),
        in_specs=[pl.BlockSpec((rows, d), lambda l, j: (0, 0)),
                  pl.BlockSpec((1, d, tn), lambda l, j: (l, 0, j)),
                  pl.BlockSpec((1, 1, tn), lambda l, j: (l, 0, j))],
        out_specs=pl.BlockSpec((1, rows, tn), lambda l, j: (l, 0, j)),
        out_shape=jax.ShapeDtypeStruct((depth, rows, six_d), F32),
        compiler_params=_cparams(("arbitrary", "arbitrary")),
        name="ada_mod",
    )(cc, w_ada, b_ada.reshape(depth, 1, six_d))


def _norm_mod(x, g, shift, scale):
    ms = jnp.mean(x * x, axis=-1, keepdims=True)
    h = x * lax.rsqrt(ms + EPS) * g
    return h * (1.0 + scale) + shift


def _inproj_kernel(*refs, use_rope):
    if use_rope:
        x_ref, mod_ref, g_ref, w_ref, cos_ref, sa_ref, sb_ref, q_ref, kv_ref, m_ref, gl_ref, s_ref = refs
    else:
        x_ref, mod_ref, g_ref, w_ref, q_ref, kv_ref, m_ref, gl_ref, s_ref = refs
    mod = mod_ref[0]
    hb = _norm_mod(x_ref[0], g_ref[...], mod[0:1], mod[1:2]).astype(BF16)

    def proj(lo, width):
        return _dot(hb, w_ref[:, lo:lo + width])

    pa = proj(0, Q_COLS + KV_COLS)
    if use_rope:
        cos, sa, sb = cos_ref[...], sa_ref[...], sb_ref[...]
        segs = []
        for j in range((Q_COLS + KV_COLS // 2) // LANES):
            seg = pa[:, j * LANES:(j + 1) * LANES]
            segs.append(seg * cos + pltpu.roll(seg, ROPE_AXIS_DIM // 2, 1) * sa
                        + pltpu.roll(seg, LANES - ROPE_AXIS_DIM // 2, 1) * sb)
        qk = jnp.concatenate(segs, axis=1)
    else:
        qk = pa[:, :Q_COLS + KV_COLS // 2]
    q_ref[0] = qk[:, :Q_COLS] * (HEAD_DIM ** -0.5 * LOG2E)
    kv_ref[0, :, :KV_COLS // 2] = qk[:, Q_COLS:]
    kv_ref[0, :, KV_COLS // 2:] = pa[:, Q_COLS + KV_COLS // 2:]
    lo = Q_COLS + KV_COLS
    m_ref[0] = proj(lo, MIX_COLS)
    gl_ref[0] = proj(lo + MIX_COLS, MIX_COLS)
    s_ref[0] = proj(lo + 2 * MIX_COLS, SMALL_COLS)


def _inproj_call(x, mod, g, w, rope):
    bx, n, d = x.shape
    tm = _pick_tile(n, 1024)
    cols = w.shape[1]
    in_specs = [pl.BlockSpec((1, tm, d), lambda b, i: (b, i, 0)),
                pl.BlockSpec((1, 6, d), lambda b, i: (b, 0, 0)),
                pl.BlockSpec((1, d), lambda b, i: (0, 0)),
                pl.BlockSpec((d, cols), lambda b, i: (0, 0), pipeline_mode=pl.Buffered(1))]
    args = [x, mod, g.reshape(1, d), w]
    if rope is not None:
        in_specs += [pl.BlockSpec((tm, LANES), lambda b, i: (i, 0))] * 3
        args += list(rope)
    widths = (Q_COLS, KV_COLS, MIX_COLS, MIX_COLS, SMALL_COLS)
    return pl.pallas_call(
        functools.partial(_inproj_kernel, use_rope=rope is not None),
        grid=(bx, n // tm),
        in_specs=in_specs,
        out_specs=[pl.BlockSpec((1, tm, wd), lambda b, i: (b, i, 0)) for wd in widths],
        out_shape=[jax.ShapeDtypeStruct((bx, n, wd), F32) for wd in widths],
        compiler_params=_cparams(("parallel", "parallel")),
        name="inproj_rope" if rope is not None else "inproj",
    )(*args)


def _attn_kernel(*refs, local, tq, nsub):
    if local:
        sink_ref, q_ref, kv_ref, kvc_ref, o_ref, km_ref, vt_ref, bias_ref, kmc_ref, vtc_ref = refs
    else:
        sink_ref, q_ref, kvc_ref, o_ref, kmc_ref, vtc_ref = refs
    npairs = ATTN_GROUP
    cols = npairs * tq
    span = 3 * ATTN_BLOCK
    lane = lax.broadcasted_iota(jnp.int32, (1, LANES), 1)
    head_lanes = (lane < HEAD_DIM, lane >= HEAD_DIM)
    row8 = lax.broadcasted_iota(jnp.int32, (8, LANES), 0)
    ones8 = jnp.where(row8 == 0, 1.0, 0.0).astype(F32)

    def prepare(src_ref, km, vt):
        ntiles = src_ref.shape[1] // LANES
        tiles = _group(ntiles, 4)

        def body(i, carry):
            rows = [pl.ds(pl.multiple_of((i * tiles + u) * LANES, LANES), LANES) for u in range(tiles)]
            k128 = [src_ref[0, r, 0:LANES] for r in rows]
            v_t = [src_ref[0, r, LANES:2 * LANES].T for r in rows]
            for u, r in enumerate(rows):
                for kvh in range(ATTN_KV_HEADS):
                    km[kvh, r, :] = jnp.where(head_lanes[kvh], k128[u], 0.0).astype(BF16)
                    vt[kvh, 0:HEAD_DIM, r] = v_t[u][kvh * HEAD_DIM:(kvh + 1) * HEAD_DIM]
                    vt[kvh, HEAD_DIM:M_AUG, r] = ones8
            return carry

        lax.fori_loop(0, ntiles // tiles, body, 0)

    @pl.when(pl.program_id(1) == 0)
    def _():
        prepare(kvc_ref, kmc_ref, vtc_ref)
        if local:
            prepare(kv_ref, km_ref, vt_ref)
            rel = lax.broadcasted_iota(jnp.int32, bias_ref.shape, 0) - 2 * ATTN_BLOCK
            qoff = lax.broadcasted_iota(jnp.int32, bias_ref.shape, 1) & (tq - 1)
            bias_ref[...] = jnp.where(jnp.abs(rel - qoff) <= WINDOW, 0.0, NEG_BIG)

    units = [(sb, kvh) for sb in range(nsub) for kvh in range(ATTN_KV_HEADS)]
    qall, start, band = {}, {}, {}
    for sb in range(nsub):
        q = q_ref[0, sb * tq:(sb + 1) * tq, :].astype(BF16)
        qall[sb] = jnp.concatenate([q[:, p * LANES:(p + 1) * LANES] for p in range(npairs)], axis=0)
        if local:
            j = pl.program_id(1) * nsub + sb
            start[sb] = pl.multiple_of(jnp.clip((j - 1) * ATTN_BLOCK, 0, kv_ref.shape[1] - span), ATTN_BLOCK)
            band[sb] = bias_ref[pl.ds(pl.multiple_of(start[sb] - j * ATTN_BLOCK + 2 * ATTN_BLOCK, ATTN_BLOCK),
                                      span), :]
    sink = [jnp.concatenate([jnp.full((1, tq), sink_ref[kvh * ATTN_GROUP + p] * LOG2E, F32)
                             for p in range(npairs)], axis=1) for kvh in range(ATTN_KV_HEADS)]
    s_ctx, s_loc, m, acc = {}, {}, {}, {}
    for sb, kvh in units:
        s_ctx[sb, kvh] = _dot_nt(kmc_ref[kvh], qall[sb])
        if local:
            s_loc[sb, kvh] = _dot_nt(km_ref[kvh, pl.ds(start[sb], span), :], qall[sb]) + band[sb]
    for u in units:
        m[u] = jnp.maximum(jnp.max(s_ctx[u], axis=0, keepdims=True), sink[u[1]])
        if local:
            m[u] = jnp.maximum(m[u], jnp.max(s_loc[u], axis=0, keepdims=True))
    for sb, kvh in units:
        u = (sb, kvh)
        acc[u] = _dot(vtc_ref[kvh].astype(BF16), jnp.exp2(s_ctx[u] - m[u]).astype(BF16))
        if local:
            acc[u] = acc[u] + _dot(vt_ref[kvh, :, pl.ds(start[sb], span)].astype(BF16),
                                   jnp.exp2(s_loc[u] - m[u]).astype(BF16))
    o_t = {u: acc[u][0:HEAD_DIM] / (acc[u][HEAD_DIM:HEAD_DIM + 1] + jnp.exp2(sink[u[1]] - m[u])) for u in units}
    for sb in range(nsub):
        for p in range(npairs):
            for c in range(tq // LANES):
                sl = slice(p * tq + c * LANES, p * tq + (c + 1) * LANES)
                tile = jnp.concatenate([o_t[sb, kvh][:, sl] for kvh in range(ATTN_KV_HEADS)], axis=0)
                r0 = sb * tq + c * LANES
                o_ref[0, r0:r0 + LANES, p * LANES:(p + 1) * LANES] = tile.T


def _attn_call(sink, q, kv, kvc):
    b, nq, _ = q.shape
    nc = kvc.shape[1]
    local = kv is not None
    tq = ATTN_BLOCK if local else nq
    nsub = ATTN_SUBBLOCKS if local else 1
    assert tq % LANES == 0 and nc % LANES == 0 and tq & (tq - 1) == 0 and nq % (nsub * tq) == 0
    in_specs = [pl.BlockSpec(memory_space=pltpu.SMEM),
                pl.BlockSpec((1, nsub * tq, Q_COLS), lambda bi, j: (bi, j, 0))]
    args = [sink, q]
    scratch = []
    if local:
        in_specs.append(pl.BlockSpec((1, nq, KV_COLS), lambda bi, j: (bi, 0, 0)))
        args.append(kv)
        scratch += [pltpu.VMEM((ATTN_KV_HEADS, nq, LANES), BF16), pltpu.VMEM((ATTN_KV_HEADS, M_AUG, nq), F32),
                    pltpu.VMEM((5 * ATTN_BLOCK, ATTN_GROUP * tq), F32)]
    in_specs.append(pl.BlockSpec((1, nc, KV_COLS), lambda bi, j: (bi, 0, 0)))
    args.append(kvc)
    scratch += [pltpu.VMEM((ATTN_KV_HEADS, nc, LANES), BF16), pltpu.VMEM((ATTN_KV_HEADS, M_AUG, nc), F32)]
    return pl.pallas_call(
        functools.partial(_attn_kernel, local=local, tq=tq, nsub=nsub),
        grid=(b, nq // (nsub * tq)),
        in_specs=in_specs,
        out_specs=pl.BlockSpec((1, nsub * tq, Q_COLS), lambda bi, j: (bi, j, 0)),
        out_shape=jax.ShapeDtypeStruct((b, nq, Q_COLS), F32),
        scratch_shapes=scratch,
        compiler_params=_cparams(("arbitrary", "arbitrary")),
        name="attn_window" if local else "attn_ctx",
    )(*args)


def _tri(n, rev):
    ri = lax.broadcasted_iota(jnp.int32, (n, n), 0)
    ci = lax.broadcasted_iota(jnp.int32, (n, n), 1)
    return ci >= ri if rev else ci <= ri


def _seq_specs(n):
    return [pl.BlockSpec((1, n, LANES), lambda b, hp, off=off: (b, 0, off + hp)) for off in (0, 2, 4, 6)]


def _group(nchunks, pref):
    g = min(pref, nchunks)
    assert nchunks % g == 0
    return g


def _mlstm_kernel(*refs, with_ctx_out):
    (qc_ref, kc_ref, vc_ref, oc_gate_ref, grc_ref,
     ql_ref, kl_ref, vl_ref, ol_gate_ref, grl_ref, bias_ref, g_ref) = refs[:12]
    if with_ctx_out:
        o_lat_ref, o_ctx_ref = refs[12:14]
        scratch = refs[14:]
    else:
        o_lat_ref, o_ctx_ref = refs[12], None
        scratch = refs[13:]
    w_ref, b_ref, ck_ref, tot_ref, mc_ref, mp_ref, kt_ref, cc_ref, vt_ref = scratch
    L = M_CHUNK
    ncc, ncl = qc_ref.shape[1] // L, ql_ref.shape[1] // L
    kt_c = -(-ncc // KT_CHUNKS)

    mask_t = (_tri(L, True), _tri(L, False))
    tri_f, tri_b = (mk.astype(BF16) for mk in mask_t)
    ones_m = jnp.ones((L, L), BF16)
    row8 = lax.broadcasted_iota(jnp.int32, (8, L), 0)
    rev8 = (row8 & 2) != 0
    ones8 = jnp.where(row8 == 0, 1.0, 0.0).astype(F32)

    def prep(gr_ref, v_ref, base, kt_base, v_base):
        nch = gr_ref.shape[3]
        li = (gr_ref[0, 0, 0] + bias_ref[0, 0]).reshape(nch * 8, L)
        lf = _log_sigmoid(gr_ref[0, 0, 1] + bias_ref[0, 1]).reshape(nch * 8, L)
        rev = (lax.broadcasted_iota(jnp.int32, li.shape, 0) & 2) != 0
        lane = lax.broadcasted_iota(jnp.int32, li.shape, 1)
        l1, l2, l3 = _split3(lf)
        scan = lambda m: _dot(l1, m) + (_dot(l2, m) + _dot(l3, m))
        b = jnp.where(rev, scan(tri_b), scan(tri_f))
        tot = scan(ones_m)
        g = tot - b + li
        mc = jnp.max(g, axis=1, keepdims=True)
        kap = li - b
        ckf, ckb = kap, kap
        s = 1
        while s < L:
            ckf = jnp.maximum(ckf, jnp.where(lane >= s, pltpu.roll(ckf, s, 1), NEG_BIG))
            ckb = jnp.maximum(ckb, jnp.where(lane < L - s, pltpu.roll(ckb, L - s, 1), NEG_BIG))
            s *= 2
        sl = pl.ds(base, nch)
        w_ref[sl] = jnp.exp(g - mc).reshape(nch, 8, L)
        b_ref[sl] = b.reshape(nch, 8, L)
        ck_ref[sl] = jnp.where(rev, ckb, ckf).reshape(nch, 8, L)
        tot_ref[sl] = tot.reshape(nch, 8, L)
        mc_ref[sl] = jnp.broadcast_to(mc, (nch * 8, L)).reshape(nch, 8, L)
        for gi in range(-(-nch // KT_CHUNKS)):
            blk = kap[gi * LANES:(gi + 1) * LANES]
            if blk.shape[0] < LANES:
                blk = jnp.concatenate([blk, jnp.zeros((LANES - blk.shape[0], L), F32)], axis=0)
            kt_ref[kt_base + gi] = blk.T

        tiles = _group(nch, 4)

        def vt_body(i, carry):
            blocks = [v_ref[0, pl.ds(pl.multiple_of((i * tiles + u) * L, L), L), :].T for u in range(tiles)]
            for u, blk in enumerate(blocks):
                vt_ref[:, pl.ds(pl.multiple_of(v_base + (i * tiles + u) * L, L), L)] = blk
            return carry

        lax.fori_loop(0, nch // tiles, vt_body, 0)

    def vaug(vt2, hh):
        return jnp.concatenate([vt2[hh * HEAD_DIM:(hh + 1) * HEAD_DIM], ones8], axis=0)

    def summarize(k_ref, base, v_base):
        nch = k_ref.shape[1] // L
        grp = _group(nch, M_GROUP)

        def body(t, carry):
            lhs, rhs = {}, {}
            for gi in range(grp):
                cl = t * grp + gi
                k2 = (k_ref[0, pl.ds(pl.multiple_of(cl * L, L), L), :] * (HEAD_DIM ** -0.5)).astype(BF16)
                vt2 = vt_ref[:, pl.ds(pl.multiple_of(v_base + cl * L, L), L)]
                w8 = w_ref[base + cl]
                for hh in range(2):
                    va = vaug(vt2, hh)
                    lhs[gi, hh] = jnp.concatenate([va * w8[d * 2 + hh:d * 2 + hh + 1] for d in range(2)],
                                                  axis=0).astype(BF16)
                    rhs[gi, hh] = k2[:, hh * HEAD_DIM:(hh + 1) * HEAD_DIM]
            cc = {key: _dot(lhs[key], rhs[key]) for key in lhs}
            for (gi, hh), val in cc.items():
                for d in range(2):
                    cc_ref[base + t * grp + gi, d * 2 + hh] = val[d * M_AUG:(d + 1) * M_AUG]
            return carry

        lax.fori_loop(0, nch // grp, body, 0)

    def scan_states(base, nch, state):
        def body(t, st):
            m8, cs = st[0], list(st[1:])
            cf, cb = base + t, base + nch - 1 - t
            tot8 = jnp.where(rev8, tot_ref[cb], tot_ref[cf])
            mc8 = jnp.where(rev8, mc_ref[cb], mc_ref[cf])
            m_new = jnp.maximum(tot8 + m8, mc8)
            s_old = jnp.exp(tot8 + m8 - m_new)
            s_new = jnp.exp(mc8 - m_new)
            mp_ref[cf, 0:2, :] = m8[0:2]
            mp_ref[cb, 2:4, :] = m8[2:4]
            for j in range(4):
                c = cf if j < 2 else cb
                summary = cc_ref[c, j]
                cc_ref[c, j] = cs[j]
                cs[j] = s_old[j:j + 1, 0:HEAD_DIM] * cs[j] + s_new[j:j + 1, 0:HEAD_DIM] * summary
            return (m_new, *cs)

        return lax.fori_loop(0, nch, body, state)

    def emit(q_ref, k_ref, gate_ref, o_ref, base, kt_base, v_base):
        nch = q_ref.shape[1] // L
        grp = _group(nch, M_GROUP)

        def body(t, carry):
            units = [(gi, hh) for gi in range(grp) for hh in range(2)]
            rows, vas, rho, alpha, floor, kcol, lhs, rhs = {}, {}, {}, {}, {}, {}, {}, {}
            for gi in range(grp):
                cl = t * grp + gi
                c = base + cl
                rows[gi] = pl.ds(pl.multiple_of(cl * L, L), L)
                q2 = q_ref[0, rows[gi], :].astype(BF16)
                k2 = k_ref[0, rows[gi], :] * (HEAD_DIM ** -0.5)
                vt2 = vt_ref[:, pl.ds(pl.multiple_of(v_base + cl * L, L), L)]
                mp8 = mp_ref[c]
                rho[gi] = -jnp.maximum(mp8, ck_ref[c])
                alpha[gi] = jnp.exp(mp8 + rho[gi])
                floor[gi] = jnp.exp(rho[gi] - b_ref[c])
                kt = kt_ref[kt_base + (cl >> KT_SHIFT)]
                kcol[gi] = pltpu.roll(kt, (LANES - (cl & (KT_CHUNKS - 1)) * 8) & (LANES - 1), 1)
                for hh in range(2):
                    ln = slice(hh * HEAD_DIM, (hh + 1) * HEAD_DIM)
                    vas[gi, hh] = vaug(vt2, hh).astype(BF16)
                    lhs[gi, hh] = jnp.concatenate([k2[:, ln], cc_ref[c, hh], cc_ref[c, 2 + hh]], axis=0).astype(BF16)
                    rhs[gi, hh] = q2[:, ln]
            prod = {u: _dot_nt(lhs[u], rhs[u]) for u in units}
            pt = {}
            for gi, hh in units:
                st = prod[gi, hh][0:L]
                for d in range(2):
                    j = d * 2 + hh
                    e = kcol[gi][:, j:j + 1] + rho[gi][j:j + 1, :]
                    pt[gi, hh, d] = (jnp.where(mask_t[d], jnp.exp(e), 0.0) * st).astype(BF16)
            pv = {key: _dot(vas[key[0], key[1]], val) for key, val in pt.items()}
            for gi in range(grp):
                halves = []
                for hh in range(2):
                    hsum = None
                    for d in range(2):
                        j = d * 2 + hh
                        lo = L + d * M_AUG
                        num = alpha[gi][j:j + 1, :] * prod[gi, hh][lo:lo + M_AUG] + pv[gi, hh, d]
                        h = num[0:HEAD_DIM] / jnp.maximum(jnp.abs(num[HEAD_DIM:HEAD_DIM + 1]), floor[gi][j:j + 1, :])
                        hsum = h if hsum is None else hsum + h
                    ms = jnp.sum(hsum * hsum, axis=0, keepdims=True) * (1.0 / HEAD_DIM)
                    halves.append(hsum * lax.rsqrt(ms + EPS))
                y = jnp.concatenate(halves, axis=0).T
                o_ref[0, rows[gi], :] = y * g_ref[...] * jax.nn.sigmoid(gate_ref[0, rows[gi], :])
            return carry

        lax.fori_loop(0, nch // grp, body, 0)

    n_ctx = qc_ref.shape[1]
    prep(grc_ref, vc_ref, 0, 0, 0)
    prep(grl_ref, vl_ref, ncc, kt_c, n_ctx)
    summarize(kc_ref, 0, 0)
    summarize(kl_ref, ncc, n_ctx)
    zero = (jnp.zeros((8, L), F32),) + (jnp.zeros((M_AUG, HEAD_DIM), F32),) * 4
    state = scan_states(0, ncc, zero)
    scan_states(ncc, ncl, state)
    if with_ctx_out:
        emit(qc_ref, kc_ref, oc_gate_ref, o_ctx_ref, 0, 0, 0)
    emit(ql_ref, kl_ref, ol_gate_ref, o_lat_ref, ncc, kt_c, n_ctx)


def _mlstm_gates(s):
    b, n, _ = s.shape
    g = s[:, :, :2 * N_MGATES].reshape(b, n, 2, 2, 2, 2)
    g = g.transpose(0, 4, 2, 3, 5, 1).reshape(b, 2, 2, 4, n // M_CHUNK, M_CHUNK).transpose(0, 1, 2, 4, 3, 5)
    return jnp.concatenate([g, g], axis=4)


def _mlstm_call(mc, sc, ml, sl, i_bias, f_bias, g, with_ctx_out):
    b, n, _ = ml.shape
    nc = mc.shape[1]
    assert n % M_CHUNK == 0 and nc % M_CHUNK == 0
    ncc, ncl = nc // M_CHUNK, n // M_CHUNK
    bias = jnp.stack([i_bias, f_bias]).astype(F32).reshape(2, 2, 2, 2).transpose(2, 0, 1, 3).reshape(2, 2, 4)
    bias = jnp.concatenate([bias, bias], axis=2)[..., None]
    gr_spec = lambda nch: pl.BlockSpec((1, 1, 2, nch, 8, M_CHUNK), lambda bi, hp: (bi, hp, 0, 0, 0, 0))
    in_specs = (_seq_specs(nc) + [gr_spec(ncc)] + _seq_specs(n) + [gr_spec(ncl)]
                + [pl.BlockSpec((1, 2, 8, 1), lambda bi, hp: (hp, 0, 0, 0)),
                   pl.BlockSpec((1, LANES), lambda bi, hp: (0, hp))])
    out_specs = [pl.BlockSpec((1, n, LANES), lambda bi, hp: (bi, 0, hp))]
    out_shape = [jax.ShapeDtypeStruct((b, n, M_HEADS * HEAD_DIM), F32)]
    if with_ctx_out:
        out_specs.append(pl.BlockSpec((1, nc, LANES), lambda bi, hp: (bi, 0, hp)))
        out_shape.append(jax.ShapeDtypeStruct((b, nc, M_HEADS * HEAD_DIM), F32))
    tot = ncc + ncl
    per_chunk = pltpu.VMEM((tot, 8, M_CHUNK), F32)
    scratch = [per_chunk] * 6 + [
        pltpu.VMEM((-(-ncc // KT_CHUNKS) + -(-ncl // KT_CHUNKS), M_CHUNK, LANES), F32),
        pltpu.VMEM((tot, 4, M_AUG, HEAD_DIM), F32),
        pltpu.VMEM((LANES, nc + n), F32)]
    outs = pl.pallas_call(
        functools.partial(_mlstm_kernel, with_ctx_out=with_ctx_out),
        grid=(b, M_HEADS // 2),
        in_specs=in_specs,
        out_specs=out_specs,
        out_shape=out_shape,
        scratch_shapes=scratch,
        compiler_params=_cparams(("parallel", "arbitrary")),
        name="mlstm",
    )(mc, mc, mc, mc, _mlstm_gates(sc), ml, ml, ml, ml, _mlstm_gates(sl), bias, g.reshape(1, -1))
    return (outs[0], outs[1]) if with_ctx_out else (outs[0], None)


def _head_norm_gate(hsum, g, gate):
    sq = hsum * hsum
    lane = lax.broadcasted_iota(jnp.int32, hsum.shape, 1)
    first = lane < HEAD_DIM
    s0 = jnp.sum(jnp.where(first, sq, 0.0), axis=-1, keepdims=True)
    s1 = jnp.sum(jnp.where(first, 0.0, sq), axis=-1, keepdims=True)
    ms = jnp.where(first, s0, s1) * (1.0 / HEAD_DIM)
    return hsum * lax.rsqrt(ms + EPS) * g * gate


def _seg_scan_sum(x, rev):
    rows = x.shape[0]
    pos = lax.broadcasted_iota(jnp.int32, x.shape, 0) & (CHUNK - 1)
    s = 1
    while s < CHUNK:
        if rev:
            x = x + jnp.where(pos < CHUNK - s, pltpu.roll(x, rows - s, 0), 0.0)
        else:
            x = x + jnp.where(pos >= s, pltpu.roll(x, s, 0), 0.0)
        s *= 2
    return x


def _gla_kernel(*refs, with_ctx_out):
    (qc_ref, kc_ref, vc_ref, oc_gate_ref, sc_ref,
     ql_ref, kl_ref, vl_ref, ol_gate_ref, sl_ref, wa_ref, ba_ref, g_ref) = refs[:13]
    if with_ctx_out:
        o_lat_ref, o_ctx_ref, bc_ref, u_ref, dec_ref = refs[13:]
    else:
        o_lat_ref, bc_ref, u_ref, dec_ref = refs[13:]
        o_ctx_ref = None
    ncc, ncl = qc_ref.shape[1] // CHUNK, ql_ref.shape[1] // CHUNK
    lo = 2 * N_MGATES
    mask_f, mask_b = _tri(CHUNK, False), _tri(CHUNK, True)
    first = lax.broadcasted_iota(jnp.int32, (HEAD_DIM, LANES), 1) < HEAD_DIM
    halves = (slice(0, HEAD_DIM), slice(HEAD_DIM, 2 * HEAD_DIM))
    silu = lambda t: t * jax.nn.sigmoid(t)

    def slab_of(t, grp):
        return pl.multiple_of(t * (grp * CHUNK), grp * CHUNK)

    def summarize(k_ref, v_ref, s_ref, base):
        nch = k_ref.shape[1] // CHUNK
        grp = _group(nch, GLA_GROUP)
        assert grp % 2 == 0

        def body(t, carry):
            r0 = slab_of(t, grp)
            slab = pl.ds(r0, grp * CHUNK)
            k3 = k_ref[0, slab, :].reshape(grp, CHUNK, LANES)
            v2 = v_ref[0, slab, :]
            kws, ends = [], []
            for d in range(2):
                lr = s_ref[0, slab, lo + d * GLA_RANK:lo + (d + 1) * GLA_RANK]
                glog = _log_sigmoid(_dot(lr.astype(BF16), wa_ref[d].astype(BF16)) + ba_ref[d]) * (1.0 / GLA_TAU)
                bc = _seg_scan_sum(glog, d == 1)
                bc_ref[d, pl.ds(pl.multiple_of(base * CHUNK + r0, CHUNK), grp * CHUNK), :] = bc
                bc3 = bc.reshape(grp, CHUNK, LANES)
                end = bc3[:, 0:1, :] if d == 1 else bc3[:, CHUNK - 1:CHUNK, :]
                kws.append((k3 * jnp.exp(end - bc3)).astype(BF16))
                ends.append(end)
            vts = [v2[p * LANES:(p + 1) * LANES, :].T.astype(BF16) for p in range(grp // 2)]
            us = {}
            for gi in range(grp):
                for hh in range(2):
                    vt = vts[gi // 2][halves[hh], halves[gi % 2]]
                    for d in range(2):
                        us[gi, hh, d] = _dot(vt, kws[d][gi][:, halves[hh]])
            for gi in range(grp):
                c = base + t * grp + gi
                for hh in range(2):
                    u_ref[hh, c] = jnp.concatenate([us[gi, hh, 0], us[gi, hh, 1]], axis=1)
                    end2 = jnp.concatenate([ends[0][gi][:, halves[hh]], ends[1][gi][:, halves[hh]]], axis=1)
                    dec_ref[hh, c] = jnp.broadcast_to(jnp.exp(end2), (8, LANES))
            return carry

        lax.fori_loop(0, nch // grp, body, 0)

    def scan_states(base, nch, state):
        def body(t, st):
            cf, cb = base + t, base + nch - 1 - t
            new = []
            for hh in range(2):
                u = jnp.where(first, u_ref[hh, cf], u_ref[hh, cb])
                dec = jnp.where(first[0:1], dec_ref[hh, cf][0:1], dec_ref[hh, cb][0:1])
                u_ref[hh, cf, :, 0:HEAD_DIM] = st[hh][:, 0:HEAD_DIM]
                u_ref[hh, cb, :, HEAD_DIM:] = st[hh][:, HEAD_DIM:]
                new.append(st[hh] * dec + u)
            return tuple(new)

        return lax.fori_loop(0, nch, body, state)

    def emit(q_ref, k_ref, v_ref, gate_ref, o_ref, base):
        nch = q_ref.shape[1] // CHUNK
        grp = _group(nch, GLA_GROUP)

        def body(t, carry):
            r0 = slab_of(t, grp)
            slab = pl.ds(r0, grp * CHUNK)
            q2 = q_ref[0, slab, :] * (HEAD_DIM ** -0.5)
            k2 = k_ref[0, slab, :]
            vb = v_ref[0, slab, :].astype(BF16)
            qd, kd = [], []
            for d in range(2):
                bc = bc_ref[d, pl.ds(pl.multiple_of(base * CHUNK + r0, CHUNK), grp * CHUNK), :]
                qd.append((q2 * jnp.exp(bc)).astype(BF16))
                kd.append((k2 * jnp.exp(-bc)).astype(BF16))
            units = [(gi, hh) for gi in range(grp) for hh in range(2)]
            rs = lambda gi: slice(gi * CHUNK, (gi + 1) * CHUNK)
            att = {}
            for gi, hh in units:
                a_f = jnp.where(mask_f, _dot_nt(qd[0][rs(gi), halves[hh]], kd[0][rs(gi), halves[hh]]), 0.0)
                a_b = jnp.where(mask_b, _dot_nt(qd[1][rs(gi), halves[hh]], kd[1][rs(gi), halves[hh]]), 0.0)
                att[gi, hh] = (a_f + a_b).astype(BF16)
            outs = []
            for gi in range(grp):
                c = base + t * grp + gi
                parts = []
                for hh in range(2):
                    qcat = jnp.concatenate([qd[0][rs(gi), halves[hh]], qd[1][rs(gi), halves[hh]]], axis=1)
                    parts.append(_dot(att[gi, hh], vb[rs(gi), halves[hh]])
                                 + _dot_nt(qcat, u_ref[hh, c].astype(BF16)))
                outs.append(jnp.concatenate(parts, axis=1))
            o2 = jnp.concatenate(outs, axis=0)
            o_ref[0, slab, :] = _head_norm_gate(o2, g_ref[...], silu(gate_ref[0, slab, :]))
            return carry

        lax.fori_loop(0, nch // grp, body, 0)

    summarize(kc_ref, vc_ref, sc_ref, 0)
    summarize(kl_ref, vl_ref, sl_ref, ncc)
    state = scan_states(0, ncc, (jnp.zeros((HEAD_DIM, LANES), F32),) * 2)
    scan_states(ncc, ncl, state)
    if with_ctx_out:
        emit(qc_ref, kc_ref, vc_ref, oc_gate_ref, o_ctx_ref, 0)
    emit(ql_ref, kl_ref, vl_ref, ol_gate_ref, o_lat_ref, ncc)


def _gla_call(gc, sc, gl, sl, wa2, ba, g, with_ctx_out):
    b, n, _ = gl.shape
    nc = gc.shape[1]
    tot = (nc + n) // CHUNK
    small = lambda nn: pl.BlockSpec((1, nn, SMALL_COLS), lambda bi, hp: (bi, 0, 0))
    in_specs = (_seq_specs(nc) + [small(nc)] + _seq_specs(n) + [small(n)]
                + [pl.BlockSpec((2, GLA_RANK, LANES), lambda bi, hp: (0, 0, hp)),
                   pl.BlockSpec((2, 1, LANES), lambda bi, hp: (0, 0, hp)),
                   pl.BlockSpec((1, LANES), lambda bi, hp: (0, hp))])
    out_specs = [pl.BlockSpec((1, n, LANES), lambda bi, hp: (bi, 0, hp))]
    out_shape = [jax.ShapeDtypeStruct((b, n, G_HEADS * HEAD_DIM), F32)]
    if with_ctx_out:
        out_specs.append(pl.BlockSpec((1, nc, LANES), lambda bi, hp: (bi, 0, hp)))
        out_shape.append(jax.ShapeDtypeStruct((b, nc, G_HEADS * HEAD_DIM), F32))
    scratch = [pltpu.VMEM((2, nc + n, LANES), F32),
               pltpu.VMEM((2, tot, HEAD_DIM, LANES), F32),
               pltpu.VMEM((2, tot, 8, LANES), F32)]
    outs = pl.pallas_call(
        functools.partial(_gla_kernel, with_ctx_out=with_ctx_out),
        grid=(b, G_HEADS // 2),
        in_specs=in_specs,
        out_specs=out_specs,
        out_shape=out_shape,
        scratch_shapes=scratch,
        compiler_params=_cparams(("parallel", "arbitrary")),
        name="gla",
    )(gc, gc, gc, gc, sc, gl, gl, gl, gl, sl, wa2, ba.reshape(2, 1, -1), g.reshape(1, -1))
    return (outs[0], outs[1]) if with_ctx_out else (outs[0], None)


def _outmlp_kernel(*refs, final, ff_tile):
    if final:
        x_ref, oa_ref, om_ref, og_ref, mod_ref, g_ref, wo_ref, w1_ref, w2_ref, fg_ref, y_ref = refs
    else:
        x_ref, oa_ref, om_ref, og_ref, mod_ref, g_ref, wo_ref, w1_ref, w2_ref, y_ref = refs
    mod = mod_ref[0]
    a_w, m_w = oa_ref.shape[2], om_ref.shape[2]
    o = (_dot(oa_ref[0].astype(BF16), wo_ref[0:a_w, :])
         + _dot(om_ref[0].astype(BF16), wo_ref[a_w:a_w + m_w, :])
         + _dot(og_ref[0].astype(BF16), wo_ref[a_w + m_w:, :]))
    x1 = x_ref[0] + mod[2:3] * o
    hb = _norm_mod(x1, g_ref[...], mod[3:4], mod[4:5]).astype(BF16)
    d_ff = w1_ref.shape[1]
    acc = jnp.zeros(x1.shape, F32)
    for j in range(d_ff // ff_tile):
        t = jnp.maximum(_dot(hb, w1_ref[:, j * ff_tile:(j + 1) * ff_tile]), 0.0)
        acc = acc + _dot((t * t).astype(BF16), w2_ref[j * ff_tile:(j + 1) * ff_tile, :])
    x2 = x1 + mod[5:6] * acc
    if final:
        ms = jnp.mean(x2 * x2, axis=-1, keepdims=True)
        x2 = x2 * lax.rsqrt(ms + EPS) * fg_ref[...]
    y_ref[0] = x2


def _outmlp_call(x, oa, om, og, mod, g, wo, w1, w2, final_g):
    bx, n, d = x.shape
    tm = _pick_tile(n, 512)
    d_ff = w1.shape[1]
    row = lambda wd: pl.BlockSpec((1, tm, wd), lambda b, i: (b, i, 0))
    whole = lambda arr: pl.BlockSpec(arr.shape, lambda b, i: (0,) * arr.ndim, pipeline_mode=pl.Buffered(1))
    g2 = g.reshape(1, d)
    in_specs = [row(d), row(oa.shape[2]), row(om.shape[2]), row(og.shape[2]),
                pl.BlockSpec((1, 6, d), lambda b, i: (b, 0, 0)), whole(g2), whole(wo), whole(w1), whole(w2)]
    args = [x, oa, om, og, mod, g2, wo, w1, w2]
    if final_g is not None:
        fg = final_g.reshape(1, d)
        in_specs.append(whole(fg))
        args.append(fg)
    return pl.pallas_call(
        functools.partial(_outmlp_kernel, final=final_g is not None, ff_tile=_pick_tile(d_ff, 1024)),
        grid=(bx, n // tm),
        in_specs=in_specs,
        out_specs=row(d),
        out_shape=jax.ShapeDtypeStruct((bx, n, d), F32),
        compiler_params=_cparams(("parallel", "parallel")),
        name="outproj_mlp",
    )(*args)


def _rope_tables(n):
    t = jnp.arange(n)
    inv = ROPE_BASE ** (-jnp.arange(0, ROPE_AXIS_DIM, 2, dtype=F32) / ROPE_AXIS_DIM)
    ang_r = (t // GRID_W).astype(F32)[:, None] * inv[None, :]
    ang_c = (t % GRID_W).astype(F32)[:, None] * inv[None, :]
    half = ROPE_AXIS_DIM // 2
    cos = jnp.concatenate([jnp.cos(ang_r)] * 2 + [jnp.cos(ang_c)] * 2, axis=1)
    zero = jnp.zeros((n, half), F32)
    sin_a = jnp.concatenate([zero, jnp.sin(ang_r), zero, jnp.sin(ang_c)], axis=1)
    sin_b = jnp.concatenate([-jnp.sin(ang_r), zero, -jnp.sin(ang_c), zero], axis=1)
    return tuple(jnp.tile(a, (1, LANES // HEAD_DIM)).astype(F32) for a in (cos, sin_a, sin_b))


def _pair_heads(a, axis):
    shape = a.shape
    a = a.reshape(shape[:axis] + (ATTN_KV_HEADS, ATTN_GROUP, HEAD_DIM) + shape[axis + 1:])
    return jnp.swapaxes(a, axis, axis + 1).reshape(shape)


def _permute_w_in(w_in):
    aq, ak, av, mq, mk, mv, mo, mi, mf, gq, gk, gv, gg, glr = jnp.split(
        w_in, np.cumsum([Q_COLS, 128, 128, 256, 256, 256, 256, 8, 8, 256, 256, 256, 256])[:13].tolist(), axis=-1)
    pad = jnp.zeros(w_in.shape[:-1] + (SMALL_COLS - 16 - 2 * GLA_RANK,), w_in.dtype)
    return jnp.concatenate([_pair_heads(aq, aq.ndim - 1), ak, av, mq, mk, mv, mo, gq, gk, gv, gg, mi, mf, glr, pad],
                           axis=-1).astype(BF16)


def _permute_w_out(w_out):
    return jnp.concatenate([_pair_heads(w_out[:, :Q_COLS], 1), w_out[:, Q_COLS:]], axis=1).astype(BF16)


def kernel(x, c, ctx, c_ctx, w_ada, b_ada, norm1_g, norm2_g, w_in, attn_sink, m_i_bias, m_f_bias, m_norm_g,
           g_wa2, g_ba, g_norm_g, w_out, w_mlp1, w_mlp2, final_g):
    B, N, D = x.shape
    Nc = ctx.shape[1]
    depth = w_ada.shape[0]
    rope = _rope_tables(N)
    rows = -(-(B + 1) // 8) * 8
    cc = jnp.zeros((rows, D), F32).at[:B].set(c).at[B].set(c_ctx)
    mods = _ada_call(cc, w_ada, b_ada)
    mods_x = mods[:, :B].reshape(depth, B, 6, D)
    mods_c = mods[:, B:B + 1].reshape(depth, 1, 6, D)
    w_in_p = _permute_w_in(w_in)
    wo, w1, w2 = _permute_w_out(w_out), w_mlp1.astype(BF16), w_mlp2.astype(BF16)

    xc = ctx
    for l in range(depth):
        last = l == depth - 1
        q, kv, ml, gl, sl = _inproj_call(x, mods_x[l], norm1_g[l], w_in_p[l], rope)
        ctx_parts = _inproj_call(xc.reshape(1, B * Nc, D), mods_c[l], norm1_g[l], w_in_p[l], None)
        qc, kvc, mc, gc, sc = (t.reshape(B, Nc, -1) for t in ctx_parts)
        oa = _attn_call(attn_sink[l], q, kv, kvc)
        om, omc = _mlstm_call(mc, sc, ml, sl, m_i_bias[l], m_f_bias[l], m_norm_g[l], not last)
        og, ogc = _gla_call(gc, sc, gl, sl, g_wa2[l], g_ba[l], g_norm_g[l], not last)
        x = _outmlp_call(x, oa, om, og, mods_x[l], norm2_g[l], wo[l], w1[l], w2[l], final_g if last else None)
        if not last:
            oac = _attn_call(attn_sink[l], qc, None, kvc)
            flat = lambda t: t.reshape(1, B * Nc, -1)
            xc = _outmlp_call(flat(xc), flat(oac), flat(omc), flat(ogc), mods_c[l], norm2_g[l],
                              wo[l], w1[l], w2[l], None).reshape(B, Nc, D)
    return x
```
